```python
import math
import jax, jax.numpy as jnp
from jax import lax
import numpy as np

D_MODEL = 1024
BATCH = 16
SEQ = 2048
DEPTH = 2

HEAD_DIM = 64
N_HEADS_SB = 4
N_HEADS_DIL = 4
N_HEADS_FOX = 4
W_SB = N_HEADS_SB * HEAD_DIM
W_DIL = N_HEADS_DIL * HEAD_DIM
W_FOX = N_HEADS_FOX * HEAD_DIM
CONV_WIDTH = D_MODEL - W_SB - W_DIL - W_FOX
N_CONV_GROUPS = 4
CONV_K = 3
DIL_PATTERNS = ((128, 1), (512, 4), (2048, 16))
BLOCK_Q = 128
REL_BUCKETS = 32
REL_MAX_DIST = 2048
D_FF = ((8 * D_MODEL + 3 * 256 - 1) // (3 * 256)) * 256
PROJ_WIDTH = 3 * W_SB + 3 * W_DIL + 3 * W_FOX + 3 * CONV_WIDTH + N_HEADS_FOX
DN_ALPHA = (2 * DEPTH) ** 0.25
DN_BETA = (8 * DEPTH) ** -0.25
LN_EPS = 1e-5

kernel_name = "hybrid_parallel_sb_dilated_fox_shortconv"


def split_heads(t, n_heads):
    b, s, _ = t.shape
    return t.reshape(b, s, n_heads, HEAD_DIM).transpose(0, 2, 1, 3)


def merge_heads(t):
    b, h, s, d = t.shape
    return t.transpose(0, 2, 1, 3).reshape(b, s, h * d)


def layer_norm(x, g, b):
    xf = x.astype(jnp.float32)
    mu = jnp.mean(xf, axis=-1, keepdims=True)
    var = jnp.mean(jnp.square(xf - mu), axis=-1, keepdims=True)
    y = (xf - mu) * lax.rsqrt(var + LN_EPS) * g.astype(jnp.float32) + b.astype(jnp.float32)
    return y.astype(x.dtype)


def stick_breaking_attention(q, k, v):
    b, h, s, d = q.shape
    nb = s // BLOCK_Q
    scale = d ** -0.5
    qb = q.reshape(b, h, nb, BLOCK_Q, d).transpose(2, 0, 1, 3, 4)
    k_pos = jnp.arange(s)

    def block(args):
        q_blk, i = args
        q_pos = i * BLOCK_Q + jnp.arange(BLOCK_Q)
        z = jnp.einsum('bhqd,bhkd->bhqk', q_blk, k).astype(jnp.float32) * scale
        strict = k_pos[None, :] < q_pos[:, None]
        log_beta = jax.nn.log_sigmoid(z)
        log_rest = jnp.where(strict, jax.nn.log_sigmoid(-z), 0.0)
        tail = lax.cumsum(log_rest, axis=3, reverse=True) - log_rest
        a = jnp.where(strict, jnp.exp(log_beta + tail), 0.0)
        return jnp.einsum('bhqk,bhkd->bhqd', a.astype(v.dtype), v)

    out = lax.map(block, (qb, jnp.arange(nb)))
    return out.transpose(1, 2, 0, 3, 4).reshape(b, h, s, d)


def forgetting_attention(q, k, v, log_f):
    b, h, s, d = q.shape
    nb = s // BLOCK_Q
    scale = d ** -0.5
    cum = jnp.cumsum(log_f, axis=-1)
    qb = q.reshape(b, h, nb, BLOCK_Q, d).transpose(2, 0, 1, 3, 4)
    cb = cum.reshape(b, h, nb, BLOCK_Q).transpose(2, 0, 1, 3)
    k_pos = jnp.arange(s)

    def block(args):
        q_blk, c_blk, i = args
        q_pos = i * BLOCK_Q + jnp.arange(BLOCK_Q)
        z = (jnp.einsum('bhqd,bhkd->bhqk', q_blk, k).astype(jnp.float32) * scale
             + (c_blk[..., :, None] - cum[..., None, :]))
        z = jnp.where(k_pos[None, :] <= q_pos[:, None], z, -jnp.inf)
        p = jax.nn.softmax(z, axis=-1)
        return jnp.einsum('bhqk,bhkd->bhqd', p.astype(v.dtype), v)

    out = lax.map(block, (qb, cb, jnp.arange(nb)))
    return out.transpose(1, 2, 0, 3, 4).reshape(b, h, s, d)


def t5_bucket(dist):
    max_exact = REL_BUCKETS // 2
    nf = jnp.maximum(dist, 1).astype(jnp.float32)
    large = max_exact + (jnp.log(nf / max_exact) / math.log(REL_MAX_DIST / max_exact)
                         * (REL_BUCKETS - max_exact)).astype(jnp.int32)
    large = jnp.minimum(large, REL_BUCKETS - 1)
    return jnp.where(dist < max_exact, dist, large)


def to_residue_blocks(t, dil, n_blk):
    b, h, s, e = t.shape
    sub_len = s // dil
    t = t.reshape(b, h, sub_len, dil, e).transpose(0, 1, 3, 2, 4)
    t = jnp.pad(t, ((0, 0), (0, 0), (0, 0), (0, n_blk * BLOCK_Q - sub_len), (0, 0)))
    return t.reshape(b, h, dil, n_blk, BLOCK_Q, e)


def from_residue_blocks(t, s):
    b, h, dil, n_blk, bq, e = t.shape
    t = t.reshape(b, h, dil, n_blk * bq, e)[:, :, :, : s // dil]
    return t.transpose(0, 1, 3, 2, 4).reshape(b, h, s, e)


def with_previous_block(t):
    prev = jnp.pad(t[:, :, :, :-1], ((0, 0), (0, 0), (0, 0), (1, 0), (0, 0), (0, 0)))
    return jnp.concatenate([prev, t], axis=4)


def dilated_window_attention(q, k, v, rel_bias):
    b, h, s, d = q.shape
    scale = d ** -0.5
    outs, maxes, dens = [], [], []
    for window, dil in DIL_PATTERNS:
        n_back = window // dil
        n_blk = -(-(s // dil) // BLOCK_Q)
        qs = to_residue_blocks(q, dil, n_blk)
        kb = with_previous_block(to_residue_blocks(k, dil, n_blk))
        vb = with_previous_block(to_residue_blocks(v, dil, n_blk))
        qi = jnp.arange(BLOCK_Q)[:, None]
        kj = jnp.arange(2 * BLOCK_Q)[None, :]
        sub_dist = qi + BLOCK_Q - kj
        in_band = (sub_dist >= 0) & (sub_dist <= n_back)
        key_valid = (jnp.arange(n_blk)[:, None, None] > 0) | (kj[None] >= BLOCK_Q)
        mask = in_band[None] & key_valid
        bias = rel_bias[t5_bucket(jnp.maximum(sub_dist, 0) * dil)]
        bias = bias.transpose(2, 0, 1).astype(jnp.float32)
        z = (jnp.einsum('bhrnqd,bhrnkd->bhrnqk', qs, kb).astype(jnp.float32) * scale
             + bias[None, :, None, None])
        z = jnp.where(mask[None, None, None], z, -jnp.inf)
        m = jnp.max(z, axis=-1, keepdims=True)
        p = jnp.exp(z - m)
        den = jnp.sum(p, axis=-1, keepdims=True)
        o = jnp.einsum('bhrnqk,bhrnkd->bhrnqd', p.astype(v.dtype), vb).astype(jnp.float32) / den
        outs.append(from_residue_blocks(o, s))
        maxes.append(from_residue_blocks(m, s))
        dens.append(from_residue_blocks(den, s))
    m_all = jnp.stack(maxes)
    w = jnp.stack(dens) * jnp.exp(m_all - jnp.max(m_all, axis=0, keepdims=True))
    out = jnp.sum(w * jnp.stack(outs), axis=0) / jnp.sum(w, axis=0)
    return out.astype(q.dtype)


def short_gated_conv(b_gate, c_gate, h, conv_w):
    u = c_gate * h
    y = lax.conv_general_dilated(
        u, conv_w[:, None, :].astype(u.dtype), window_strides=(1,), padding=[(CONV_K - 1, 0)],
        dimension_numbers=('NWC', 'WIO', 'NWC'), feature_group_count=CONV_WIDTH)
    return b_gate * y


def hybrid_mixer(x, w_in, f_bias, conv_w, w_out, rel_bias):
    proj = x @ w_in
    widths = (W_SB,) * 3 + (W_DIL,) * 3 + (W_FOX,) * 3 + (CONV_WIDTH,) * 3 + (N_HEADS_FOX,)
    points, acc = [], 0
    for wdt in widths[:-1]:
        acc += wdt
        points.append(acc)
    (sb_q, sb_k, sb_v, dl_q, dl_k, dl_v, fx_q, fx_k, fx_v,
     cv_b, cv_c, cv_h, fx_f) = jnp.split(proj, points, axis=-1)
    out_a = stick_breaking_attention(split_heads(sb_q, N_HEADS_SB), split_heads(sb_k, N_HEADS_SB),
                                     split_heads(sb_v, N_HEADS_SB))
    out_b = dilated_window_attention(split_heads(dl_q, N_HEADS_DIL), split_heads(dl_k, N_HEADS_DIL),
                                     split_heads(dl_v, N_HEADS_DIL), rel_bias)
    log_f = jax.nn.log_sigmoid((fx_f + f_bias).astype(jnp.float32)).transpose(0, 2, 1)
    out_c = forgetting_attention(split_heads(fx_q, N_HEADS_FOX), split_heads(fx_k, N_HEADS_FOX),
                                 split_heads(fx_v, N_HEADS_FOX), log_f)
    out_d = short_gated_conv(cv_b, cv_c, cv_h, conv_w)
    mixed = jnp.concatenate([merge_heads(out_a), merge_heads(out_b), merge_heads(out_c),
                             out_d.astype(x.dtype)], axis=-1)
    return mixed @ w_out


def swiglu_ffn(x, w_gate, w_up, w_down):
    return (jax.nn.silu(x @ w_gate) * (x @ w_up)) @ w_down


def setup_inputs(seed: int = 0) -> dict:
    key = jax.random.key(seed)
    ks = jax.random.split(key, 13)
    f32 = jnp.float32
    nrm = lambda k, shape, s: jax.random.normal(k, shape, f32) * s
    return {
        "x": jax.random.normal(ks[0], (BATCH, SEQ, D_MODEL), f32),
        "w_in": nrm(ks[1], (DEPTH, D_MODEL, PROJ_WIDTH), D_MODEL ** -0.5),
        "f_bias": 1.0 + nrm(ks[2], (DEPTH, N_HEADS_FOX), 0.1),
        "conv_w": nrm(ks[3], (DEPTH, CONV_K, CONV_WIDTH), CONV_K ** -0.5),
        "w_out": nrm(ks[4], (DEPTH, D_MODEL, D_MODEL), D_MODEL ** -0.5 * DN_BETA),
        "rel_bias": nrm(ks[5], (REL_BUCKETS, N_HEADS_DIL), 0.1),
        "ln1_g": 1.0 + nrm(ks[6], (DEPTH, D_MODEL), 0.02),
        "ln1_b": nrm(ks[7], (DEPTH, D_MODEL), 0.02),
        "w_gate": nrm(ks[8], (DEPTH, D_MODEL, D_FF), D_MODEL ** -0.5),
        "w_up": nrm(ks[9], (DEPTH, D_MODEL, D_FF), D_MODEL ** -0.5),
        "w_down": nrm(ks[10], (DEPTH, D_FF, D_MODEL), D_FF ** -0.5 * DN_BETA),
        "ln2_g": 1.0 + nrm(ks[11], (DEPTH, D_MODEL), 0.02),
        "ln2_b": nrm(ks[12], (DEPTH, D_MODEL), 0.02),
    }


def reference(x, w_in, f_bias, conv_w, w_out, rel_bias, ln1_g, ln1_b, w_gate, w_up, w_down,
              ln2_g, ln2_b):
    for layer in range(DEPTH):
        mix = hybrid_mixer(x, w_in[layer], f_bias[layer], conv_w[layer], w_out[layer], rel_bias)
        x = layer_norm(DN_ALPHA * x + mix, ln1_g[layer], ln1_b[layer])
        ffn = swiglu_ffn(x, w_gate[layer], w_up[layer], w_down[layer])
        x = layer_norm(DN_ALPHA * x + ffn, ln2_g[layer], ln2_b[layer])
    return x
```

```python
import functools
import math

import jax
import jax.numpy as jnp
import numpy as np
from jax import lax
from jax.experimental import pallas as pl
from jax.experimental.pallas import tpu as pltpu

HEAD_DIM = 64
N_HEADS = 4
GROUP = N_HEADS * HEAD_DIM
LANES = 128
N_PAIRS = GROUP // LANES
CONV_K = 3
DIL_PATTERNS = ((128, 1), (512, 4), (2048, 16))
DIL_BLOCK = 128
REL_BUCKETS = 32
REL_MAX_DIST = 2048
LN_EPS = 1e-5
SCALE = HEAD_DIM ** -0.5
NEG = -1e30
VMEM_LIMIT = 56 * 1024 * 1024

BF16 = jnp.bfloat16
F32 = jnp.float32

PROJ_TM = 512
DENSE_TM = 512
FFN_CHUNK = 256
SB_BQ, SB_BK = 256, 128
FOX_BQ, FOX_BK = 256, 256
COMB_ROWS = 256


def _dot(a, b):
    return jnp.dot(a, b, preferred_element_type=F32)


def _dot_nt(a, b):
    return lax.dot_general(a, b, (((1,), (1,)), ((), ())), preferred_element_type=F32)


def _resident(shape):
    nd = len(shape)
    return pl.BlockSpec(shape, lambda *_: (0,) * nd, pipeline_mode=pl.Buffered(1))


def _params(n_axes):
    return pltpu.CompilerParams(dimension_semantics=("arbitrary",) * n_axes,
                                vmem_limit_bytes=VMEM_LIMIT)


def _split_heads_rows(q, lane):
    zero = jnp.zeros_like(q)
    return jnp.concatenate([jnp.where(lane < HEAD_DIM, q, zero),
                            jnp.where(lane >= HEAD_DIM, q, zero)], axis=0)


def _merge_heads_rows(a, lane, rows):
    return jnp.where(lane < HEAD_DIM, a[:rows], a[rows:])


def _in_proj_kernel(x_ref, wm_ref, wcv_ref, wf_ref, fb_ref, cw_ref, sel_ref,
                    main_ref, aug_ref, od_ref, carry_ref, ubuf_ref, *, tiles_per_seq):
    tm = x_ref.shape[0]
    first = (pl.program_id(0) % tiles_per_seq) == 0
    xb = x_ref[...].astype(BF16)

    n_main = main_ref.shape[1]
    for c0 in range(0, n_main, GROUP):
        main_ref[:, c0:c0 + GROUP] = _dot(xb, wm_ref[:, c0:c0 + GROUP]).astype(BF16)

    g = _dot(xb, wf_ref[...]) + fb_ref[...]
    logf = jnp.minimum(g, 0.0) - jnp.log(1.0 + jnp.exp(-jnp.abs(g)))
    r_i = lax.broadcasted_iota(jnp.int32, (tm, tm), 0)
    c_i = lax.broadcasted_iota(jnp.int32, (tm, tm), 1)
    tri = jnp.where(c_i <= r_i, 1.0, 0.0).astype(BF16)
    l_hi = logf.astype(BF16)
    l_r = logf - l_hi.astype(F32)
    l_mid = l_r.astype(BF16)
    l_lo = (l_r - l_mid.astype(F32)).astype(BF16)
    csum = _dot(tri, l_hi) + _dot(tri, l_mid) + _dot(tri, l_lo)
    prev = jnp.where(first, 0.0, carry_ref[...])
    csum = csum + prev
    carry_ref[...] = csum[tm - 1:tm, :]
    c_hi = csum.astype(BF16)
    c_r = csum - c_hi.astype(F32)
    c_mid = c_r.astype(BF16)
    c_lo = (c_r - c_mid.astype(F32)).astype(BF16)
    ones = jnp.ones((tm, LANES), BF16)
    aug_ref[...] = _dot(jnp.concatenate([c_hi, c_mid, c_lo, ones], axis=1), sel_ref[...]).astype(BF16)

    cvb = _dot(xb, wcv_ref[:, 0:GROUP])
    cvc = _dot(xb, wcv_ref[:, GROUP:2 * GROUP])
    cvh = _dot(xb, wcv_ref[:, 2 * GROUP:3 * GROUP])
    u = cvc * cvh

    @pl.when(first)
    def _():
        ubuf_ref[0:8, :] = jnp.zeros((8, GROUP), F32)

    ubuf_ref[8:8 + tm, :] = u
    y = (cw_ref[0:1, :] * ubuf_ref[6:6 + tm, :] + cw_ref[1:2, :] * ubuf_ref[7:7 + tm, :]
         + cw_ref[2:3, :] * u)
    od_ref[...] = (cvb * y).astype(BF16)
    ubuf_ref[0:8, :] = ubuf_ref[tm:tm + 8, :]


def _aug_selection():
    sel = np.zeros((4 * LANES, 4 * LANES), np.float32)
    ones_row = 3 * LANES
    for h in range(N_HEADS):
        bq = (h // 2) * LANES + (h % 2) * HEAD_DIM
        bk = 2 * LANES + bq
        for part in range(3):
            sel[part * LANES + h, bq + part] = 1.0
            sel[ones_row, bq + 3 + part] = 1.0
            sel[ones_row, bk + part] = 1.0
            sel[part * LANES + h, bk + 3 + part] = -1.0
    return sel


def _in_proj(x2d, w_in, f_bias, conv_w, seq):
    m, d = x2d.shape
    tm = PROJ_TM
    n_main = 9 * GROUP
    wm = w_in[:, :n_main].astype(BF16)
    wcv = w_in[:, n_main:n_main + 3 * GROUP].astype(BF16)
    wf = jnp.pad(w_in[:, n_main + 3 * GROUP:], ((0, 0), (0, LANES - N_HEADS))).astype(BF16)
    fb = jnp.pad(f_bias, (0, LANES - N_HEADS))[None, :]
    cw = jnp.pad(conv_w, ((0, 8 - CONV_K), (0, 0)))
    sel = jnp.asarray(_aug_selection(), BF16)
    row = lambda i: (i, 0)
    return pl.pallas_call(
        functools.partial(_in_proj_kernel, tiles_per_seq=seq // tm),
        grid=(m // tm,),
        in_specs=[pl.BlockSpec((tm, d), row), _resident(wm.shape), _resident(wcv.shape),
                  _resident(wf.shape), _resident(fb.shape), _resident(cw.shape), _resident(sel.shape)],
        out_specs=[pl.BlockSpec((tm, n_main), row), pl.BlockSpec((tm, 4 * LANES), row),
                   pl.BlockSpec((tm, GROUP), row)],
        out_shape=[jax.ShapeDtypeStruct((m, n_main), BF16), jax.ShapeDtypeStruct((m, 4 * LANES), BF16),
                   jax.ShapeDtypeStruct((m, GROUP), BF16)],
        scratch_shapes=[pltpu.VMEM((1, LANES), F32), pltpu.VMEM((tm + 8, GROUP), F32)],
        compiler_params=_params(1),
        name="in_proj",
    )(x2d, wm, wcv, wf, fb, cw, sel)


def _sb_kernel(q_ref, k_ref, v_ref, t_ref, o_ref, acc_ref, carry_ref):
    i = pl.program_id(2)
    bq, bk = SB_BQ, SB_BK
    lane_q = lax.broadcasted_iota(jnp.int32, (bq, LANES), 1)
    qs = _split_heads_rows(q_ref[...], lane_q) * SCALE
    acc_ref[...] = jnp.zeros_like(acc_ref)
    carry_ref[...] = jnp.zeros_like(carry_ref)
    row = lax.broadcasted_iota(jnp.int32, (2 * bq, bk), 0)
    q_pos = i * bq + jnp.where(row >= bq, row - bq, row)
    col = lax.broadcasted_iota(jnp.int32, (2 * bq, bk), 1)

    def block(kb, masked):
        start = pl.multiple_of(kb * bk, bk)
        ks = k_ref[pl.ds(start, bk), :]
        vs = v_ref[pl.ds(start, bk), :]
        z = _dot_nt(qs, ks)
        sp = jnp.log(1.0 + jnp.exp(-jnp.abs(z)))
        log_beta = jnp.minimum(z, 0.0) - sp
        log_rest = log_beta - z
        if masked:
            strict = (start + col) < q_pos
            log_rest = jnp.where(strict, log_rest, 0.0)
        hi = log_rest.astype(BF16)
        lo = (log_rest - hi.astype(F32)).astype(BF16)
        cs = _dot(jnp.concatenate([hi, lo], axis=1), t_ref[...])
        tail = cs[:, :bk] + carry_ref[...]
        carry_ref[...] += cs[:, bk:]
        a = jnp.exp(log_beta + tail)
        if masked:
            a = jnp.where(strict, a, 0.0)
        acc_ref[...] += _dot(a.astype(BF16), vs)

    block(2 * i + 1, True)
    block(2 * i, True)

    def body(t, c):
        block(2 * i - 1 - t, False)
        return c

    lax.fori_loop(0, 2 * i, body, 0)
    o_ref[...] = _merge_heads_rows(acc_ref[...], lane_q, bq).astype(BF16)


def _sb_tail_matrix():
    bk = SB_BK
    j = np.arange(2 * bk)[:, None] % bk
    s = np.arange(2 * bk)[None, :]
    return np.where(s < bk, j > s, True).astype(np.float32)


def _sb_attn(main3):
    b, s, _ = main3.shape
    bq = SB_BQ
    tmat = jnp.asarray(_sb_tail_matrix(), BF16)
    return pl.pallas_call(
        _sb_kernel,
        grid=(b, N_PAIRS, s // bq),
        in_specs=[pl.BlockSpec((None, bq, LANES), lambda bi, p, i: (bi, i, p)),
                  pl.BlockSpec((None, s, LANES), lambda bi, p, i: (bi, 0, N_PAIRS + p)),
                  pl.BlockSpec((None, s, LANES), lambda bi, p, i: (bi, 0, 2 * N_PAIRS + p)),
                  _resident(tmat.shape)],
        out_specs=pl.BlockSpec((None, bq, LANES), lambda bi, p, i: (bi, i, p)),
        out_shape=jax.ShapeDtypeStruct((b, s, GROUP), BF16),
        scratch_shapes=[pltpu.VMEM((2 * bq, LANES), F32), pltpu.VMEM((2 * bq, SB_BK), F32)],
        compiler_params=_params(3),
        name="sb_attn",
    )(main3, main3, main3, tmat)


def _fox_kernel(q_ref, aq_ref, k_ref, ak_ref, v_ref, o_ref, m_ref, l_ref, acc_ref):
    i = pl.program_id(2)
    bq, bk = FOX_BQ, FOX_BK
    lane_q = lax.broadcasted_iota(jnp.int32, (bq, LANES), 1)
    qs = jnp.concatenate([_split_heads_rows(q_ref[...], lane_q) * SCALE,
                          _split_heads_rows(aq_ref[...], lane_q)], axis=1)

    def scores(kb):
        start = pl.multiple_of(kb * bk, bk)
        kk = jnp.concatenate([k_ref[pl.ds(start, bk), :], ak_ref[pl.ds(start, bk), :]], axis=1)
        return _dot_nt(qs, kk), v_ref[pl.ds(start, bk), :]

    def wide(a):
        return jnp.broadcast_to(a, (2 * bq, LANES))

    z, vs = scores(i)
    row = lax.broadcasted_iota(jnp.int32, (2 * bq, bk), 0)
    col = lax.broadcasted_iota(jnp.int32, (2 * bq, bk), 1)
    z = jnp.where(col <= jnp.where(row >= bq, row - bq, row), z, NEG)
    m0 = jnp.max(z, axis=1, keepdims=True)
    p = jnp.exp(z - m0)
    m_ref[...] = wide(m0)
    l_ref[...] = wide(jnp.sum(p, axis=1, keepdims=True))
    acc_ref[...] = _dot(p.astype(BF16), vs)

    def body(kb, c):
        z, vs = scores(kb)
        m_old = m_ref[...]
        m_new = jnp.maximum(m_old, jnp.max(z, axis=1, keepdims=True))
        alpha = jnp.exp(m_old - m_new)
        p = jnp.exp(z - jnp.concatenate([m_new] * (bk // LANES), axis=1))
        l_ref[...] = alpha * l_ref[...] + jnp.sum(p, axis=1, keepdims=True)
        acc_ref[...] = alpha * acc_ref[...] + _dot(p.astype(BF16), vs)
        m_ref[...] = m_new
        return c

    lax.fori_loop(0, i, body, 0)
    o_ref[...] = _merge_heads_rows(acc_ref[...] / l_ref[...], lane_q, bq).astype(BF16)


def _fox_attn(main3, aug3):
    b, s, _ = main3.shape
    bq = FOX_BQ
    base = 6 * N_PAIRS
    return pl.pallas_call(
        _fox_kernel,
        grid=(b, N_PAIRS, s // bq),
        in_specs=[pl.BlockSpec((None, bq, LANES), lambda bi, p, i: (bi, i, base + p)),
                  pl.BlockSpec((None, bq, LANES), lambda bi, p, i: (bi, i, p)),
                  pl.BlockSpec((None, s, LANES), lambda bi, p, i: (bi, 0, base + N_PAIRS + p)),
                  pl.BlockSpec((None, s, LANES), lambda bi, p, i: (bi, 0, N_PAIRS + p)),
                  pl.BlockSpec((None, s, LANES), lambda bi, p, i: (bi, 0, base + 2 * N_PAIRS + p))],
        out_specs=pl.BlockSpec((None, bq, LANES), lambda bi, p, i: (bi, i, p)),
        out_shape=jax.ShapeDtypeStruct((b, s, GROUP), BF16),
        scratch_shapes=[pltpu.VMEM((2 * bq, LANES), F32)] * 3,
        compiler_params=_params(3),
        name="fox_attn",
    )(main3, aug3, main3, aug3, main3)


def _dil_steps():
    return sum(d for _, d in DIL_PATTERNS)


def _dil_kernel(*refs, seq):
    n_pat = len(DIL_PATTERNS)
    qkv = refs[:3 * n_pat]
    bias_ref = refs[3 * n_pat]
    o_ref = refs[3 * n_pat + 1]
    scratch = refs[3 * n_pat + 2:]
    u_refs, m_refs, l_refs = scratch[0:n_pat], scratch[n_pat:2 * n_pat], scratch[2 * n_pat:3 * n_pat]
    r = pl.program_id(2)
    blk = DIL_BLOCK
    lane = lax.broadcasted_iota(jnp.int32, (blk, LANES), 1)

    def run(pidx, dil, c):
        q_ref, k_ref, v_ref = qkv[3 * pidx:3 * pidx + 3]
        n_blk = (seq // dil) // blk

        def one(n, first):
            qstart = n * blk if first else pl.multiple_of(n * blk, blk)
            qs = _split_heads_rows(q_ref[pl.ds(qstart, blk), :], lane) * SCALE
            if first:
                keys, vals = k_ref[0:blk, :], v_ref[0:blk, :]
                bias = jnp.concatenate([bias_ref[pidx, 0, :, blk:], bias_ref[pidx, 1, :, blk:]], axis=0)
            else:
                kstart = pl.multiple_of((n - 1) * blk, blk)
                keys, vals = k_ref[pl.ds(kstart, 2 * blk), :], v_ref[pl.ds(kstart, 2 * blk), :]
                bias = jnp.concatenate([bias_ref[pidx, 0], bias_ref[pidx, 1]], axis=0)
            z = _dot_nt(qs, keys) + bias
            m = jnp.max(z, axis=1, keepdims=True)
            p = jnp.exp(z - m)
            den = jnp.sum(p, axis=1, keepdims=True)
            u = _dot(p.astype(BF16), vals)
            wide = lambda a: jnp.broadcast_to(a, (2 * blk, LANES))
            start = n * (blk * dil) + c
            idx = pl.ds(start, blk, stride=dil) if dil > 1 else pl.ds(start, blk)
            u_refs[pidx][idx, :] = _merge_heads_rows(u, lane, blk)
            m_refs[pidx][idx, :] = _merge_heads_rows(wide(m), lane, blk)
            l_refs[pidx][idx, :] = _merge_heads_rows(wide(den), lane, blk)

        one(0, True)
        if n_blk > 1:
            def body(n, carry):
                one(n, False)
                return carry
            lax.fori_loop(1, n_blk, body, 0)

    lo = 0
    for pidx, (_, dil) in enumerate(DIL_PATTERNS):
        pl.when((r >= lo) & (r < lo + dil))(functools.partial(run, pidx, dil, r - lo if dil > 1 else 0))
        lo += dil

    @pl.when(r == lo - 1)
    def _():
        def comb(j, carry):
            sl = pl.ds(pl.multiple_of(j * COMB_ROWS, COMB_ROWS), COMB_ROWS)
            ms = [m_ref[sl, :] for m_ref in m_refs]
            m_all = functools.reduce(jnp.maximum, ms)
            es = [jnp.exp(mp - m_all) for mp in ms]
            num = sum(e * u_ref[sl, :] for e, u_ref in zip(es, u_refs))
            den = sum(e * l_ref[sl, :] for e, l_ref in zip(es, l_refs))
            o_ref[sl, :] = (num / den).astype(BF16)
            return carry
        lax.fori_loop(0, seq // COMB_ROWS, comb, 0)


def _t5_bucket_of(dist):
    max_exact = REL_BUCKETS // 2
    nf = jnp.maximum(dist, 1).astype(jnp.float32)
    large = max_exact + (jnp.log(nf / max_exact) / math.log(REL_MAX_DIST / max_exact)
                         * (REL_BUCKETS - max_exact)).astype(jnp.int32)
    large = jnp.minimum(large, REL_BUCKETS - 1)
    return jnp.where(dist < max_exact, dist, large)


def _dil_bias_tables(rel_bias):
    blk = DIL_BLOCK
    qi = jnp.arange(blk)[:, None]
    kj = jnp.arange(2 * blk)[None, :]
    sub = qi + blk - kj
    tables = []
    for window, dil in DIL_PATTERNS:
        in_band = (sub >= 0) & (sub <= window // dil)
        bias = rel_bias[_t5_bucket_of(jnp.maximum(sub, 0) * dil)].transpose(2, 0, 1).astype(F32)
        tables.append(jnp.where(in_band[None], bias, NEG))
    return jnp.stack(tables)


def _dil_attn(main3, bias_tables):
    b, s, width = main3.shape
    n_lane_blocks = width // LANES
    base = 3 * N_PAIRS
    n_pat = len(DIL_PATTERNS)
    starts = np.cumsum([0] + [d for _, d in DIL_PATTERNS])
    operands, in_specs = [], []
    for pidx, (_, dil) in enumerate(DIL_PATTERNS):
        view = main3.reshape(b, s // dil, dil * width)
        lo = int(starts[pidx])
        for role in range(3):
            def imap(bi, p, r, lo=lo, dil=dil, role=role):
                c = jnp.clip(r - lo, 0, dil - 1)
                return (bi, 0, c * n_lane_blocks + base + role * N_PAIRS + p)
            operands.append(view)
            in_specs.append(pl.BlockSpec((None, s // dil, LANES), imap))
    operands.append(bias_tables)
    in_specs.append(pl.BlockSpec((n_pat, 2, DIL_BLOCK, 2 * DIL_BLOCK), lambda bi, p, r: (0, p, 0, 0)))
    return pl.pallas_call(
        functools.partial(_dil_kernel, seq=s),
        grid=(b, N_PAIRS, _dil_steps()),
        in_specs=in_specs,
        out_specs=pl.BlockSpec((None, s, LANES), lambda bi, p, r: (bi, 0, p)),
        out_shape=jax.ShapeDtypeStruct((b, s, GROUP), BF16),
        scratch_shapes=[pltpu.VMEM((s, LANES), F32)] * (3 * n_pat),
        compiler_params=_params(3),
        name="dil_attn",
    )(*operands)


def _layer_norm(v, g, b):
    mu = jnp.mean(v, axis=-1, keepdims=True)
    d = v - mu
    var = jnp.mean(d * d, axis=-1, keepdims=True)
    return d * lax.rsqrt(var + LN_EPS) * g + b


def _dense_kernel(x_ref, oa_ref, ob_ref, oc_ref, od_ref, wo_ref, g1_ref, b1_ref,
                  wg_ref, wu_ref, wd_ref, g2_ref, b2_ref, out_ref, h_ref, *, alpha):
    mix = (_dot(oa_ref[...], wo_ref[0:GROUP, :]) + _dot(ob_ref[...], wo_ref[GROUP:2 * GROUP, :])
           + _dot(oc_ref[...], wo_ref[2 * GROUP:3 * GROUP, :]) + _dot(od_ref[...], wo_ref[3 * GROUP:4 * GROUP, :]))
    x1 = _layer_norm(alpha * x_ref[...] + mix, g1_ref[...], b1_ref[...])
    xb = x1.astype(BF16)
    d_ff = wg_ref.shape[1]
    for c0 in range(0, d_ff, FFN_CHUNK):
        g = _dot(xb, wg_ref[:, c0:c0 + FFN_CHUNK])
        u = _dot(xb, wu_ref[:, c0:c0 + FFN_CHUNK])
        h_ref[:, c0:c0 + FFN_CHUNK] = (g * (1.0 / (1.0 + jnp.exp(-g))) * u).astype(BF16)
    y = _dot(h_ref[...], wd_ref[...])
    out_ref[...] = _layer_norm(alpha * x1 + y, g2_ref[...], b2_ref[...])


def _dense(x2d, outs, w_out, g1, b1, w_gate, w_up, w_down, g2, b2, alpha):
    m, d = x2d.shape
    tm = DENSE_TM
    d_ff = w_gate.shape[1]
    row = lambda i: (i, 0)
    vec = lambda a: a[None, :]
    o_spec = pl.BlockSpec((tm, GROUP), row)
    return pl.pallas_call(
        functools.partial(_dense_kernel, alpha=alpha),
        grid=(m // tm,),
        in_specs=[pl.BlockSpec((tm, d), row), o_spec, o_spec, o_spec, o_spec,
                  _resident((d, d)), _resident((1, d)), _resident((1, d)),
                  _resident((d, d_ff)), _resident((d, d_ff)), _resident((d_ff, d)),
                  _resident((1, d)), _resident((1, d))],
        out_specs=pl.BlockSpec((tm, d), row),
        out_shape=jax.ShapeDtypeStruct((m, d), F32),
        scratch_shapes=[pltpu.VMEM((tm, d_ff), BF16)],
        compiler_params=_params(1),
        name="dense",
    )(x2d, *outs, w_out.astype(BF16), vec(g1), vec(b1), w_gate.astype(BF16), w_up.astype(BF16),
      w_down.astype(BF16), vec(g2), vec(b2))


def kernel(x, w_in, f_bias, conv_w, w_out, rel_bias, ln1_g, ln1_b, w_gate, w_up, w_down, ln2_g, ln2_b):
    b, s, d = x.shape
    depth = w_in.shape[0]
    assert d == 4 * GROUP and w_in.shape[2] == 12 * GROUP + N_HEADS
    assert s % PROJ_TM == 0 and s % max(SB_BQ, FOX_BQ, COMB_ROWS) == 0
    assert all((s // dil) % DIL_BLOCK == 0 for _, dil in DIL_PATTERNS)
    alpha = (2 * depth) ** 0.25
    bias_tables = _dil_bias_tables(rel_bias)
    x2d = x.reshape(b * s, d)
    for layer in range(depth):
        main, aug, out_d = _in_proj(x2d, w_in[layer], f_bias[layer], conv_w[layer], s)
        main3 = main.reshape(b, s, main.shape[1])
        out_a = _sb_attn(main3)
        out_b = _dil_attn(main3, bias_tables)
        out_c = _fox_attn(main3, aug.reshape(b, s, aug.shape[1]))
        outs = [o.reshape(b * s, GROUP) for o in (out_a, out_b, out_c)] + [out_d]
        x2d = _dense(x2d, outs, w_out[layer], ln1_g[layer], ln1_b[layer], w_gate[layer], w_up[layer],
                     w_down[layer], ln2_g[layer], ln2_b[layer], alpha)
    return x2d.reshape(b, s, d)
```

```python
import functools
import math

import jax
import jax.numpy as jnp
import numpy as np
from jax import lax
from jax.experimental import pallas as pl
from jax.experimental.pallas import tpu as pltpu

HEAD_DIM = 64
N_HEADS = 4
GROUP = N_HEADS * HEAD_DIM
LANES = 128
N_PAIRS = GROUP // LANES
CONV_K = 3
DIL_PATTERNS = ((128, 1), (512, 4), (2048, 16))
DIL_BLOCK = 128
DIL_GROUP0 = 6
REL_BUCKETS = 32
REL_MAX_DIST = 2048
LN_EPS = 1e-5
SCALE = HEAD_DIM ** -0.5
NEG = -1e30
VMEM_LIMIT = 56 * 1024 * 1024

BF16 = jnp.bfloat16
F32 = jnp.float32

PROJ_TM = 512
DENSE_TM = 512
FFN_CHUNK = 256
SB_BQ = 256
SB_SUB = 128
FOX_BQ = 256
DIL_GEN_GROUP = 3
DIL_RES_GROUP = 4
COMB_ROWS = 256


def _dot(a, b):
    return jnp.dot(a, b, preferred_element_type=F32)


def _dot_nt(a, b):
    return lax.dot_general(a, b, (((1,), (1,)), ((), ())), preferred_element_type=F32)


def _resident(shape):
    nd = len(shape)
    return pl.BlockSpec(shape, lambda *_: (0,) * nd, pipeline_mode=pl.Buffered(1))


def _params(n_axes):
    return pltpu.CompilerParams(dimension_semantics=("arbitrary",) * n_axes,
                                vmem_limit_bytes=VMEM_LIMIT)


def _split_heads_rows(q, lane):
    zero = jnp.zeros_like(q)
    return jnp.concatenate([jnp.where(lane < HEAD_DIM, q, zero),
                            jnp.where(lane >= HEAD_DIM, q, zero)], axis=0)


def _merge_heads_rows(a, lane, rows):
    return jnp.where(lane < HEAD_DIM, a[:rows], a[rows:])


def _in_proj_kernel(x_ref, wm_ref, wcv_ref, wf_ref, fb_ref, cw_ref, sel_ref,
                    main_ref, dl4_ref, dl16_ref, aug_ref, od_ref, carry_ref, ubuf_ref, *stage_refs,
                    tiles_per_seq):
    tm = x_ref.shape[0]
    first = (pl.program_id(0) % tiles_per_seq) == 0
    xb = x_ref[...].astype(BF16)

    n_main = main_ref.shape[1]
    dl_col0 = DIL_GROUP0 * LANES
    for c0 in range(0, n_main, GROUP):
        r = _dot(xb, wm_ref[:, c0:c0 + GROUP])
        main_ref[:, c0:c0 + GROUP] = r.astype(BF16)
        if dl_col0 <= c0 < dl_col0 + 3 * GROUP:
            j = (c0 - dl_col0) // LANES
            stage_refs[j][...] = r[:, :LANES]
            stage_refs[j + 1][...] = r[:, LANES:]

    for dil, ref in ((DIL_PATTERNS[1][1], dl4_ref), (DIL_PATTERNS[2][1], dl16_ref)):
        rows = tm // dil
        for c in range(dil):
            for j in range(len(stage_refs)):
                ref[c, :, j * LANES:(j + 1) * LANES] = (
                    stage_refs[j][pl.ds(c, rows, stride=dil), :].astype(BF16))

    g = _dot(xb, wf_ref[...]) + fb_ref[...]
    logf = jnp.minimum(g, 0.0) - jnp.log(1.0 + jnp.exp(-jnp.abs(g)))
    r_i = lax.broadcasted_iota(jnp.int32, (tm, tm), 0)
    c_i = lax.broadcasted_iota(jnp.int32, (tm, tm), 1)
    tri = jnp.where(c_i <= r_i, 1.0, 0.0).astype(BF16)
    l_hi = logf.astype(BF16)
    l_r = logf - l_hi.astype(F32)
    l_mid = l_r.astype(BF16)
    l_lo = (l_r - l_mid.astype(F32)).astype(BF16)
    csum = _dot(tri, l_hi) + _dot(tri, l_mid) + _dot(tri, l_lo)
    prev = jnp.where(first, 0.0, carry_ref[...])
    csum = csum + prev
    carry_ref[...] = csum[tm - 1:tm, :]
    c_hi = csum.astype(BF16)
    c_r = csum - c_hi.astype(F32)
    c_mid = c_r.astype(BF16)
    c_lo = (c_r - c_mid.astype(F32)).astype(BF16)
    ones = jnp.ones((tm, LANES), BF16)
    aug_ref[...] = _dot(jnp.concatenate([c_hi, c_mid, c_lo, ones], axis=1), sel_ref[...]).astype(BF16)

    cvb = _dot(xb, wcv_ref[:, 0:GROUP])
    cvc = _dot(xb, wcv_ref[:, GROUP:2 * GROUP])
    cvh = _dot(xb, wcv_ref[:, 2 * GROUP:3 * GROUP])
    u = cvc * cvh

    @pl.when(first)
    def _():
        ubuf_ref[0:8, :] = jnp.zeros((8, GROUP), F32)

    ubuf_ref[8:8 + tm, :] = u
    y = (cw_ref[0:1, :] * ubuf_ref[6:6 + tm, :] + cw_ref[1:2, :] * ubuf_ref[7:7 + tm, :]
         + cw_ref[2:3, :] * u)
    od_ref[...] = (cvb * y).astype(BF16)
    ubuf_ref[0:8, :] = ubuf_ref[tm:tm + 8, :]


def _aug_selection():
    sel = np.zeros((4 * LANES, 4 * LANES), np.float32)
    ones_row = 3 * LANES
    for h in range(N_HEADS):
        bq = (h // 2) * LANES + (h % 2) * HEAD_DIM
        bk = 2 * LANES + bq
        for part in range(3):
            sel[part * LANES + h, bq + part] = 1.0
            sel[ones_row, bq + 3 + part] = 1.0
            sel[ones_row, bk + part] = 1.0
            sel[part * LANES + h, bk + 3 + part] = -1.0
    return sel


def _in_proj(x2d, w_in, f_bias, conv_w, batch, seq):
    m, d = x2d.shape
    tm = PROJ_TM
    tps = seq // tm
    n_main = 9 * GROUP
    d4, d16 = DIL_PATTERNS[1][1], DIL_PATTERNS[2][1]
    wm = w_in[:, :n_main].astype(BF16)
    wcv = w_in[:, n_main:n_main + 3 * GROUP].astype(BF16)
    wf = jnp.pad(w_in[:, n_main + 3 * GROUP:], ((0, 0), (0, LANES - N_HEADS))).astype(BF16)
    fb = jnp.pad(f_bias, (0, LANES - N_HEADS))[None, :]
    cw = jnp.pad(conv_w, ((0, 8 - CONV_K), (0, 0)))
    sel = jnp.asarray(_aug_selection(), BF16)
    row = lambda i: (i, 0)
    res = lambda i: (i // tps, 0, i % tps, 0)
    return pl.pallas_call(
        functools.partial(_in_proj_kernel, tiles_per_seq=tps),
        grid=(m // tm,),
        in_specs=[pl.BlockSpec((tm, d), row), _resident(wm.shape), _resident(wcv.shape),
                  _resident(wf.shape), _resident(fb.shape), _resident(cw.shape), _resident(sel.shape)],
        out_specs=[pl.BlockSpec((tm, n_main), row),
                   pl.BlockSpec((None, d4, tm // d4, 3 * GROUP), res),
                   pl.BlockSpec((None, d16, tm // d16, 3 * GROUP), res),
                   pl.BlockSpec((tm, 4 * LANES), row),
                   pl.BlockSpec((tm, GROUP), row)],
        out_shape=[jax.ShapeDtypeStruct((m, n_main), BF16),
                   jax.ShapeDtypeStruct((batch, d4, seq // d4, 3 * GROUP), BF16),
                   jax.ShapeDtypeStruct((batch, d16, seq // d16, 3 * GROUP), BF16),
                   jax.ShapeDtypeStruct((m, 4 * LANES), BF16),
                   jax.ShapeDtypeStruct((m, GROUP), BF16)],
        scratch_shapes=[pltpu.VMEM((1, LANES), F32), pltpu.VMEM((tm + 8, GROUP), F32)]
        + [pltpu.VMEM((tm, LANES), F32)] * (3 * N_PAIRS),
        compiler_params=_params(1),
        name="in_proj",
    )(x2d, wm, wcv, wf, fb, cw, sel)


def _sb_kernel(q_ref, k_ref, v_ref, t_ref, o_ref, acc_ref, carry_ref, *, seq):
    bq, sub = SB_BQ, SB_SUB
    lane_q = lax.broadcasted_iota(jnp.int32, (bq, LANES), 1)

    def q_block(i, outer):
        qstart = pl.multiple_of(i * bq, bq)
        qs = _split_heads_rows(q_ref[pl.ds(qstart, bq), :], lane_q) * SCALE
        acc_ref[...] = jnp.zeros_like(acc_ref)
        carry_ref[...] = jnp.zeros_like(carry_ref)

        def group(start, g, masked):
            n = g * sub
            z = _dot_nt(qs, k_ref[pl.ds(start, n), :])
            sp = jnp.log(1.0 + jnp.exp(-jnp.abs(z)))
            log_beta = jnp.minimum(z, 0.0) - sp
            log_rest = log_beta - z
            if masked:
                row = lax.broadcasted_iota(jnp.int32, (2 * bq, n), 0)
                col = lax.broadcasted_iota(jnp.int32, (2 * bq, n), 1)
                strict = (start + col) < (qstart + jnp.where(row >= bq, row - bq, row))
                log_rest = jnp.where(strict, log_rest, 0.0)
            hi = log_rest.astype(BF16)
            lo = (log_rest - hi.astype(F32)).astype(BF16)
            carry = carry_ref[...]
            parts = [None] * g
            for j in reversed(range(g)):
                sl = slice(j * sub, (j + 1) * sub)
                cs = _dot(jnp.concatenate([hi[:, sl], lo[:, sl]], axis=1), t_ref[...])
                a = jnp.exp(log_beta[:, sl] + (cs[:, :sub] + carry))
                carry = carry + cs[:, sub:]
                if masked:
                    a = jnp.where(strict[:, sl], a, 0.0)
                parts[j] = a.astype(BF16)
            carry_ref[...] = carry
            acc_ref[...] += _dot(jnp.concatenate(parts, axis=1), v_ref[pl.ds(start, n), :])

        group(qstart, 2, True)

        @pl.when((i & 1) == 1)
        def _():
            group(pl.multiple_of(qstart - 2 * sub, 2 * sub), 2, False)

        n4 = lax.shift_right_logical(i, 1)

        def body(t, c):
            group(pl.multiple_of((n4 - 1 - t) * (4 * sub), 4 * sub), 4, False)
            return c

        lax.fori_loop(0, n4, body, 0)
        o_ref[pl.ds(qstart, bq), :] = _merge_heads_rows(acc_ref[...], lane_q, bq).astype(BF16)
        return outer

    lax.fori_loop(0, seq // bq, q_block, 0)


def _sb_tail_matrix():
    sub = SB_SUB
    j = np.arange(2 * sub)[:, None] % sub
    s = np.arange(2 * sub)[None, :]
    return np.where(s < sub, j > s, True).astype(np.float32)


def _sb_attn(main3):
    b, s, _ = main3.shape
    tmat = jnp.asarray(_sb_tail_matrix(), BF16)
    spec = lambda g: pl.BlockSpec((None, s, LANES), lambda bi, p: (bi, 0, g * N_PAIRS + p))
    return pl.pallas_call(
        functools.partial(_sb_kernel, seq=s),
        grid=(b, N_PAIRS),
        in_specs=[spec(0), spec(1), spec(2), _resident(tmat.shape)],
        out_specs=pl.BlockSpec((None, s, LANES), lambda bi, p: (bi, 0, p)),
        out_shape=jax.ShapeDtypeStruct((b, s, GROUP), BF16),
        scratch_shapes=[pltpu.VMEM((2 * SB_BQ, LANES), F32), pltpu.VMEM((2 * SB_BQ, SB_SUB), F32)],
        compiler_params=_params(2),
        name="sb_attn",
    )(main3, main3, main3, tmat)


def _fox_kernel(q_ref, aq_ref, k_ref, ak_ref, v_ref, o_ref, m_ref, l_ref, acc_ref, *, seq):
    bq = FOX_BQ
    lane_q = lax.broadcasted_iota(jnp.int32, (bq, LANES), 1)

    def wide(a):
        return jnp.broadcast_to(a, (2 * bq, LANES))

    def q_block(i, outer):
        qstart = pl.multiple_of(i * bq, bq)
        qs = jnp.concatenate([_split_heads_rows(q_ref[pl.ds(qstart, bq), :], lane_q) * SCALE,
                              _split_heads_rows(aq_ref[pl.ds(qstart, bq), :], lane_q)], axis=1)

        def scores(start, n):
            kk = jnp.concatenate([k_ref[pl.ds(start, n), :], ak_ref[pl.ds(start, n), :]], axis=1)
            return _dot_nt(qs, kk), v_ref[pl.ds(start, n), :]

        z, vs = scores(qstart, bq)
        row = lax.broadcasted_iota(jnp.int32, (2 * bq, bq), 0)
        col = lax.broadcasted_iota(jnp.int32, (2 * bq, bq), 1)
        z = jnp.where(col <= jnp.where(row >= bq, row - bq, row), z, NEG)
        m0 = jnp.max(z, axis=1, keepdims=True)
        p = jnp.exp(z - m0)
        m_ref[...] = wide(m0)
        l_ref[...] = wide(jnp.sum(p, axis=1, keepdims=True))
        acc_ref[...] = _dot(p.astype(BF16), vs)

        def step(start, n):
            z, vs = scores(start, n)
            m_old = m_ref[...]
            m_new = jnp.maximum(m_old, jnp.max(z, axis=1, keepdims=True))
            alpha = jnp.exp(m_old - m_new)
            p = jnp.exp(z - jnp.concatenate([m_new] * (n // LANES), axis=1))
            l_ref[...] = alpha * l_ref[...] + jnp.sum(p, axis=1, keepdims=True)
            acc_ref[...] = alpha * acc_ref[...] + _dot(p.astype(BF16), vs)
            m_ref[...] = m_new

        @pl.when((i & 1) == 1)
        def _():
            step(pl.multiple_of(qstart - bq, bq), bq)

        def body(t, c):
            step(pl.multiple_of(t * (2 * bq), 2 * bq), 2 * bq)
            return c

        lax.fori_loop(0, lax.shift_right_logical(i, 1), body, 0)
        o_ref[pl.ds(qstart, bq), :] = _merge_heads_rows(acc_ref[...] / l_ref[...], lane_q, bq).astype(BF16)
        return outer

    lax.fori_loop(0, seq // bq, q_block, 0)


def _fox_attn(main3, aug3):
    b, s, _ = main3.shape
    base = 6 * N_PAIRS
    spec = lambda blk: pl.BlockSpec((None, s, LANES), lambda bi, p: (bi, 0, blk + p))
    return pl.pallas_call(
        functools.partial(_fox_kernel, seq=s),
        grid=(b, N_PAIRS),
        in_specs=[spec(base), spec(0), spec(base + N_PAIRS), spec(N_PAIRS), spec(base + 2 * N_PAIRS)],
        out_specs=pl.BlockSpec((None, s, LANES), lambda bi, p: (bi, 0, p)),
        out_shape=jax.ShapeDtypeStruct((b, s, GROUP), BF16),
        scratch_shapes=[pltpu.VMEM((2 * FOX_BQ, LANES), F32)] * 3,
        compiler_params=_params(2),
        name="fox_attn",
    )(main3, aug3, main3, aug3, main3)


def _dil_kernel(*refs, seq):
    n_pat = len(DIL_PATTERNS)
    qkv = refs[:3 * n_pat]
    bias_ref = refs[3 * n_pat]
    o_ref = refs[3 * n_pat + 1]
    scratch = refs[3 * n_pat + 2:]
    u_refs, m_refs, l_refs = scratch[0:n_pat], scratch[n_pat:2 * n_pat], scratch[2 * n_pat:3 * n_pat]
    blk = DIL_BLOCK
    lane = lax.broadcasted_iota(jnp.int32, (blk, LANES), 1)

    def block(pidx, dil, c, n, first):
        q_ref, k_ref, v_ref = qkv[3 * pidx:3 * pidx + 3]

        def rd(ref, start, size):
            return ref[pl.ds(start, size), :] if dil == 1 else ref[c, pl.ds(start, size), :]

        qstart = n * blk if isinstance(n, int) else pl.multiple_of(n * blk, blk)
        qs = _split_heads_rows(rd(q_ref, qstart, blk), lane) * SCALE
        if first:
            keys, vals = rd(k_ref, 0, blk), rd(v_ref, 0, blk)
            bias = jnp.concatenate([bias_ref[pidx, 0, :, blk:], bias_ref[pidx, 1, :, blk:]], axis=0)
        else:
            kstart = (n - 1) * blk if isinstance(n, int) else pl.multiple_of((n - 1) * blk, blk)
            keys, vals = rd(k_ref, kstart, 2 * blk), rd(v_ref, kstart, 2 * blk)
            bias = jnp.concatenate([bias_ref[pidx, 0], bias_ref[pidx, 1]], axis=0)
        z = _dot_nt(qs, keys) + bias
        m = jnp.max(z, axis=1, keepdims=True)
        p = jnp.exp(z - m)
        den = jnp.sum(p, axis=1, keepdims=True)
        u = _dot(p.astype(BF16), vals)
        wide = lambda a: jnp.broadcast_to(a, (2 * blk, LANES))
        idx = pl.ds(n * (blk * dil) + c, blk, stride=dil) if dil > 1 else pl.ds(qstart, blk)
        u_refs[pidx][idx, :] = _merge_heads_rows(u, lane, blk)
        m_refs[pidx][idx, :] = _merge_heads_rows(wide(m), lane, blk)
        l_refs[pidx][idx, :] = _merge_heads_rows(wide(den), lane, blk)

    for pidx, (_, dil) in enumerate(DIL_PATTERNS):
        n_blk = (seq // dil) // blk
        if dil == 1:
            block(pidx, dil, 0, 0, True)

            def body(g, carry, pidx=pidx, dil=dil):
                for jj in range(DIL_GEN_GROUP):
                    block(pidx, dil, 0, 1 + g * DIL_GEN_GROUP + jj, False)
                return carry
            lax.fori_loop(0, (n_blk - 1) // DIL_GEN_GROUP, body, 0)
        elif n_blk > 1:
            def body(c, carry, pidx=pidx, dil=dil, n_blk=n_blk):
                block(pidx, dil, c, 0, True)
                for n in range(1, n_blk):
                    block(pidx, dil, c, n, False)
                return carry
            lax.fori_loop(0, dil, body, 0)
        else:
            def body(g, carry, pidx=pidx, dil=dil):
                for jj in range(DIL_RES_GROUP):
                    block(pidx, dil, g * DIL_RES_GROUP + jj, 0, True)
                return carry
            lax.fori_loop(0, dil // DIL_RES_GROUP, body, 0)

    def comb(j, carry):
        sl = pl.ds(pl.multiple_of(j * COMB_ROWS, COMB_ROWS), COMB_ROWS)
        ms = [m_ref[sl, :] for m_ref in m_refs]
        m_all = functools.reduce(jnp.maximum, ms)
        es = [jnp.exp(mp - m_all) for mp in ms]
        num = sum(e * u_ref[sl, :] for e, u_ref in zip(es, u_refs))
        den = sum(e * l_ref[sl, :] for e, l_ref in zip(es, l_refs))
        o_ref[sl, :] = (num / den).astype(BF16)
        return carry
    lax.fori_loop(0, seq // COMB_ROWS, comb, 0)


def _t5_bucket_of(dist):
    max_exact = REL_BUCKETS // 2
    nf = jnp.maximum(dist, 1).astype(jnp.float32)
    large = max_exact + (jnp.log(nf / max_exact) / math.log(REL_MAX_DIST / max_exact)
                         * (REL_BUCKETS - max_exact)).astype(jnp.int32)
    large = jnp.minimum(large, REL_BUCKETS - 1)
    return jnp.where(dist < max_exact, dist, large)


def _dil_bias_tables(rel_bias):
    blk = DIL_BLOCK
    qi = jnp.arange(blk)[:, None]
    kj = jnp.arange(2 * blk)[None, :]
    sub = qi + blk - kj
    tables = []
    for window, dil in DIL_PATTERNS:
        in_band = (sub >= 0) & (sub <= window // dil)
        bucket = _t5_bucket_of(jnp.maximum(sub, 0) * dil)
        hit = bucket[None, :, :, None] == jnp.arange(REL_BUCKETS)
        bias = jnp.sum(jnp.where(hit, rel_bias.T.astype(F32)[:, None, None, :], 0.0), axis=-1)
        tables.append(jnp.where(in_band[None], bias, NEG))
    return jnp.stack(tables)


def _dil_attn(main3, dl4, dl16, bias_tables):
    b, s, _ = main3.shape
    n_pat = len(DIL_PATTERNS)
    operands, in_specs = [], []
    for role in range(3):
        operands.append(main3)
        in_specs.append(pl.BlockSpec((None, s, LANES),
                                     lambda bi, p, role=role: (bi, 0, DIL_GROUP0 + role * N_PAIRS + p)))
    for arr in (dl4, dl16):
        dil, rows = arr.shape[1], arr.shape[2]
        for role in range(3):
            operands.append(arr)
            in_specs.append(pl.BlockSpec((None, dil, rows, LANES),
                                         lambda bi, p, role=role: (bi, 0, 0, role * N_PAIRS + p)))
    operands.append(bias_tables)
    in_specs.append(pl.BlockSpec((n_pat, 2, DIL_BLOCK, 2 * DIL_BLOCK), lambda bi, p: (0, p, 0, 0)))
    return pl.pallas_call(
        functools.partial(_dil_kernel, seq=s),
        grid=(b, N_PAIRS),
        in_specs=in_specs,
        out_specs=pl.BlockSpec((None, s, LANES), lambda bi, p: (bi, 0, p)),
        out_shape=jax.ShapeDtypeStruct((b, s, GROUP), BF16),
        scratch_shapes=[pltpu.VMEM((s, LANES), F32)] * (3 * n_pat),
        compiler_params=_params(2),
        name="dil_attn",
    )(*operands)


def _layer_norm(v, g, b):
    mu = jnp.mean(v, axis=-1, keepdims=True)
    d = v - mu
    var = jnp.mean(d * d, axis=-1, keepdims=True)
    return d * lax.rsqrt(var + LN_EPS) * g + b


def _dense_kernel(x_ref, oa_ref, ob_ref, oc_ref, od_ref, wo_ref, g1_ref, b1_ref,
                  wg_ref, wu_ref, wd_ref, g2_ref, b2_ref, out_ref, h_ref, *, alpha):
    mix = (_dot(oa_ref[...], wo_ref[0:GROUP, :]) + _dot(ob_ref[...], wo_ref[GROUP:2 * GROUP, :])
           + _dot(oc_ref[...], wo_ref[2 * GROUP:3 * GROUP, :]) + _dot(od_ref[...], wo_ref[3 * GROUP:4 * GROUP, :]))
    x1 = _layer_norm(alpha * x_ref[...] + mix, g1_ref[...], b1_ref[...])
    xb = x1.astype(BF16)
    d_ff = wg_ref.shape[1]
    for c0 in range(0, d_ff, FFN_CHUNK):
        g = _dot(xb, wg_ref[:, c0:c0 + FFN_CHUNK])
        u = _dot(xb, wu_ref[:, c0:c0 + FFN_CHUNK])
        h_ref[:, c0:c0 + FFN_CHUNK] = (g * (1.0 / (1.0 + jnp.exp(-g))) * u).astype(BF16)
    y = _dot(h_ref[...], wd_ref[...])
    out_ref[...] = _layer_norm(alpha * x1 + y, g2_ref[...], b2_ref[...])


def _dense(x2d, outs, w_out, g1, b1, w_gate, w_up, w_down, g2, b2, alpha):
    m, d = x2d.shape
    tm = DENSE_TM
    d_ff = w_gate.shape[1]
    row = lambda i: (i, 0)
    vec = lambda a: a[None, :]
    o_spec = pl.BlockSpec((tm, GROUP), row)
    return pl.pallas_call(
        functools.partial(_dense_kernel, alpha=alpha),
        grid=(m // tm,),
        in_specs=[pl.BlockSpec((tm, d), row), o_spec, o_spec, o_spec, o_spec,
                  _resident((d, d)), _resident((1, d)), _resident((1, d)),
                  _resident((d, d_ff)), _resident((d, d_ff)), _resident((d_ff, d)),
                  _resident((1, d)), _resident((1, d))],
        out_specs=pl.BlockSpec((tm, d), row),
        out_shape=jax.ShapeDtypeStruct((m, d), F32),
        scratch_shapes=[pltpu.VMEM((tm, d_ff), BF16)],
        compiler_params=_params(1),
        name="dense",
    )(x2d, *outs, w_out.astype(BF16), vec(g1), vec(b1), w_gate.astype(BF16), w_up.astype(BF16),
      w_down.astype(BF16), vec(g2), vec(b2))


def kernel(x, w_in, f_bias, conv_w, w_out, rel_bias, ln1_g, ln1_b, w_gate, w_up, w_down, ln2_g, ln2_b):
    b, s, d = x.shape
    depth = w_in.shape[0]
    assert d == 4 * GROUP and w_in.shape[2] == 12 * GROUP + N_HEADS
    assert s % PROJ_TM == 0 and s % (2 * max(SB_BQ, FOX_BQ)) == 0 and s % COMB_ROWS == 0
    assert [dil for _, dil in DIL_PATTERNS][0] == 1
    for _, dil in DIL_PATTERNS:
        n_blk = (s // dil) // DIL_BLOCK
        assert n_blk * DIL_BLOCK * dil == s and PROJ_TM % (16 * dil) == 0
        assert (n_blk - 1) % DIL_GEN_GROUP == 0 if dil == 1 else (n_blk > 1 or dil % DIL_RES_GROUP == 0)
    alpha = (2 * depth) ** 0.25
    bias_tables = _dil_bias_tables(rel_bias)
    x2d = x.reshape(b * s, d)
    for layer in range(depth):
        main, dl4, dl16, aug, out_d = _in_proj(x2d, w_in[layer], f_bias[layer], conv_w[layer], b, s)
        main3 = main.reshape(b, s, main.shape[1])
        out_a = _sb_attn(main3)
        out_b = _dil_attn(main3, dl4, dl16, bias_tables)
        out_c = _fox_attn(main3, aug.reshape(b, s, aug.shape[1]))
        outs = [o.reshape(b * s, GROUP) for o in (out_a, out_b, out_c)] + [out_d]
        x2d = _dense(x2d, outs, w_out[layer], ln1_g[layer], ln1_b[layer], w_gate[layer], w_up[layer],
                     w_down[layer], ln2_g[layer], ln2_b[layer], alpha)
    return x2d.reshape(b, s, d)
```

```python
import functools
import math

import jax
import jax.numpy as jnp
import numpy as np
from jax import lax
from jax.experimental import pallas as pl
from jax.experimental.pallas import tpu as pltpu

HEAD_DIM = 64
N_HEADS = 4
GROUP = N_HEADS * HEAD_DIM
LANES = 128
N_PAIRS = GROUP // LANES
CONV_K = 3
DIL_PATTERNS = ((128, 1), (512, 4), (2048, 16))
DIL_BLOCK = 128
DIL_GROUP0 = 6
REL_BUCKETS = 32
REL_MAX_DIST = 2048
LN_EPS = 1e-5
SCALE = HEAD_DIM ** -0.5
NEG = -1e30
SIGN_BIT = np.uint32(0x80000000)
VMEM_LIMIT = 56 * 1024 * 1024

BF16 = jnp.bfloat16
F32 = jnp.float32

PROJ_TM = 512
DENSE_TM = 512
FFN_CHUNK = 256
SB_BQ = 256
SB_SUB = 128
FOX_BQ = 256
DIL_GEN_GROUP = 3
DIL_RES_GROUP = 4
COMB_ROWS = 256


def _dot(a, b):
    return jnp.dot(a, b, preferred_element_type=F32)


def _dot_nt(a, b):
    return lax.dot_general(a, b, (((1,), (1,)), ((), ())), preferred_element_type=F32)


def _resident(shape):
    nd = len(shape)
    return pl.BlockSpec(shape, lambda *_: (0,) * nd, pipeline_mode=pl.Buffered(1))


def _params(n_axes):
    return pltpu.CompilerParams(dimension_semantics=("arbitrary",) * n_axes,
                                vmem_limit_bytes=VMEM_LIMIT)


def _split_heads_rows(q, lane):
    zero = jnp.zeros_like(q)
    return jnp.concatenate([jnp.where(lane < HEAD_DIM, q, zero),
                            jnp.where(lane >= HEAD_DIM, q, zero)], axis=0)


def _merge_heads_rows(a, lane, rows):
    return jnp.where(lane < HEAD_DIM, a[:rows], a[rows:])


def _pair_lanes(p):
    return slice(p * LANES, (p + 1) * LANES)


def _group_spec(seq, g):
    return pl.BlockSpec((None, seq, GROUP), lambda bi: (bi, 0, g))


def _in_proj_kernel(x_ref, w_ref, fb_ref, cw_ref, sel_ref,
                    main_ref, dl4_ref, dl16_ref, aug_ref, od_ref, carry_ref, ubuf_ref, utail_ref, *stage_refs,
                    tiles_per_seq):
    tm = x_ref.shape[0]
    first = (pl.program_id(0) % tiles_per_seq) == 0
    xb = x_ref[...].astype(BF16)

    n_main = main_ref.shape[1]
    cv0 = n_main
    gate0 = n_main + 3 * GROUP
    dl_col0 = DIL_GROUP0 * LANES
    for c0 in range(0, n_main, GROUP):
        r = _dot_nt(xb, w_ref[c0:c0 + GROUP, :])
        main_ref[:, c0:c0 + GROUP] = r.astype(BF16)
        if dl_col0 <= c0 < dl_col0 + 3 * GROUP:
            j = (c0 - dl_col0) // LANES
            stage_refs[j][...] = r[:, :LANES]
            stage_refs[j + 1][...] = r[:, LANES:]

    for dil, ref in ((DIL_PATTERNS[1][1], dl4_ref), (DIL_PATTERNS[2][1], dl16_ref)):
        rows = tm // dil
        for c in range(dil):
            for j in range(len(stage_refs)):
                ref[c, :, j * LANES:(j + 1) * LANES] = (
                    stage_refs[j][pl.ds(c, rows, stride=dil), :].astype(BF16))

    g = _dot_nt(xb, w_ref[gate0:gate0 + LANES, :]) + fb_ref[...]
    logf = jnp.minimum(g, 0.0) - jnp.log(1.0 + jnp.exp(-jnp.abs(g)))
    r_i = lax.broadcasted_iota(jnp.int32, (tm, tm), 0)
    c_i = lax.broadcasted_iota(jnp.int32, (tm, tm), 1)
    tri = jnp.where(c_i <= r_i, 1.0, 0.0).astype(BF16)
    l_hi = logf.astype(BF16)
    l_r = logf - l_hi.astype(F32)
    l_mid = l_r.astype(BF16)
    l_lo = (l_r - l_mid.astype(F32)).astype(BF16)
    csum = _dot(tri, l_hi) + _dot(tri, l_mid) + _dot(tri, l_lo)
    prev = jnp.where(first, 0.0, carry_ref[...])
    csum = csum + prev
    carry_ref[...] = csum[tm - 1:tm, :]
    c_hi = csum.astype(BF16)
    c_r = csum - c_hi.astype(F32)
    c_mid = c_r.astype(BF16)
    c_lo = (c_r - c_mid.astype(F32)).astype(BF16)
    ones = jnp.ones((tm, LANES), BF16)
    aug_ref[...] = _dot(jnp.concatenate([c_hi, c_mid, c_lo, ones], axis=1), sel_ref[...]).astype(BF16)

    cvb = _dot_nt(xb, w_ref[cv0:cv0 + GROUP, :])
    cvc = _dot_nt(xb, w_ref[cv0 + GROUP:cv0 + 2 * GROUP, :])
    cvh = _dot_nt(xb, w_ref[cv0 + 2 * GROUP:cv0 + 3 * GROUP, :])
    u = cvc * cvh

    ubuf_ref[0:8, :] = jnp.where(first, 0.0, utail_ref[...])
    ubuf_ref[8:8 + tm, :] = u
    y = (cw_ref[0:1, :] * ubuf_ref[6:6 + tm, :] + cw_ref[1:2, :] * ubuf_ref[7:7 + tm, :]
         + cw_ref[2:3, :] * u)
    od_ref[...] = (cvb * y).astype(BF16)
    utail_ref[...] = u[tm - 8:tm, :]


def _aug_selection():
    sel = np.zeros((4 * LANES, 4 * LANES), np.float32)
    ones_row = 3 * LANES
    for h in range(N_HEADS):
        bq = (h // 2) * LANES + (h % 2) * HEAD_DIM
        bk = 2 * LANES + bq
        for part in range(3):
            sel[part * LANES + h, bq + part] = 1.0
            sel[ones_row, bq + 3 + part] = 1.0
            sel[ones_row, bk + part] = 1.0
            sel[part * LANES + h, bk + 3 + part] = -1.0
    return sel


def _in_proj(x2d, w_pad, f_bias, conv_w, batch, seq):
    m, d = x2d.shape
    tm = PROJ_TM
    tps = seq // tm
    n_main = 9 * GROUP
    d4, d16 = DIL_PATTERNS[1][1], DIL_PATTERNS[2][1]
    fb = jnp.pad(f_bias, (0, LANES - N_HEADS))[None, :]
    cw = jnp.pad(conv_w, ((0, 8 - CONV_K), (0, 0)))
    sel = jnp.asarray(_aug_selection(), BF16)
    row = lambda i: (i, 0)
    res = lambda i: (i // tps, 0, i % tps, 0)
    return pl.pallas_call(
        functools.partial(_in_proj_kernel, tiles_per_seq=tps),
        grid=(m // tm,),
        in_specs=[pl.BlockSpec((tm, d), row), _resident(w_pad.shape),
                  _resident(fb.shape), _resident(cw.shape), _resident(sel.shape)],
        out_specs=[pl.BlockSpec((tm, n_main), row),
                   pl.BlockSpec((None, d4, tm // d4, 3 * GROUP), res),
                   pl.BlockSpec((None, d16, tm // d16, 3 * GROUP), res),
                   pl.BlockSpec((tm, 4 * LANES), row),
                   pl.BlockSpec((tm, GROUP), row)],
        out_shape=[jax.ShapeDtypeStruct((m, n_main), BF16),
                   jax.ShapeDtypeStruct((batch, d4, seq // d4, 3 * GROUP), BF16),
                   jax.ShapeDtypeStruct((batch, d16, seq // d16, 3 * GROUP), BF16),
                   jax.ShapeDtypeStruct((m, 4 * LANES), BF16),
                   jax.ShapeDtypeStruct((m, GROUP), BF16)],
        scratch_shapes=[pltpu.VMEM((1, LANES), F32), pltpu.VMEM((tm + 8, GROUP), F32),
                        pltpu.VMEM((8, GROUP), F32)]
        + [pltpu.VMEM((tm, LANES), F32)] * (3 * N_PAIRS),
        compiler_params=_params(1),
        name="in_proj",
    )(x2d, w_pad, fb, cw, sel)


def _sb_kernel(q_ref, k_ref, v_ref, t_ref, o_ref, acc_ref, carry_ref, *, seq):
    bq, sub = SB_BQ, SB_SUB
    lane_q = lax.broadcasted_iota(jnp.int32, (bq, LANES), 1)

    def q_block(i, outer):
        qstart = pl.multiple_of(i * bq, bq)
        qss = []
        for p in range(N_PAIRS):
            qss.append(_split_heads_rows(q_ref[pl.ds(qstart, bq), _pair_lanes(p)], lane_q) * SCALE)
            acc_ref[p] = jnp.zeros((2 * bq, LANES), F32)
            carry_ref[p] = jnp.zeros((2 * bq, sub), F32)

        def group(p, start, g, masked):
            n = g * sub
            z = _dot_nt(qss[p], k_ref[pl.ds(start, n), _pair_lanes(p)])
            neg_abs = pltpu.bitcast(pltpu.bitcast(z, jnp.uint32) | SIGN_BIT, F32)
            sp = jnp.log(1.0 + jnp.exp(neg_abs))
            log_beta = jnp.minimum(z, 0.0) - sp
            log_rest = log_beta - z
            if masked:
                row = lax.broadcasted_iota(jnp.int32, (2 * bq, n), 0)
                col = lax.broadcasted_iota(jnp.int32, (2 * bq, n), 1)
                strict = (start + col) < (qstart + jnp.where(row >= bq, row - bq, row))
                log_rest = jnp.where(strict, log_rest, 0.0)
            hi = log_rest.astype(BF16)
            lo = (log_rest - hi.astype(F32)).astype(BF16)
            carry = carry_ref[p]
            parts = [None] * g
            for j in reversed(range(g)):
                sl = slice(j * sub, (j + 1) * sub)
                cs = _dot(jnp.concatenate([hi[:, sl], lo[:, sl]], axis=1), t_ref[...])
                a = jnp.exp(log_beta[:, sl] + (cs[:, :sub] + carry))
                carry = carry + cs[:, sub:]
                if masked:
                    a = jnp.where(strict[:, sl], a, 0.0)
                parts[j] = a.astype(BF16)
            carry_ref[p] = carry
            acc_ref[p] += _dot(jnp.concatenate(parts, axis=1), v_ref[pl.ds(start, n), _pair_lanes(p)])

        def groups(start, g, masked):
            for p in range(N_PAIRS):
                group(p, start, g, masked)

        groups(qstart, 2, True)

        @pl.when((i & 1) == 1)
        def _():
            groups(pl.multiple_of(qstart - 2 * sub, 2 * sub), 2, False)

        n4 = lax.shift_right_logical(i, 1)

        def body(t, c):
            groups(pl.multiple_of((n4 - 1 - t) * (4 * sub), 4 * sub), 4, False)
            return c

        lax.fori_loop(0, n4, body, 0)
        for p in range(N_PAIRS):
            o_ref[pl.ds(qstart, bq), _pair_lanes(p)] = _merge_heads_rows(acc_ref[p], lane_q, bq).astype(BF16)
        return outer

    lax.fori_loop(0, seq // bq, q_block, 0)


def _sb_tail_matrix():
    sub = SB_SUB
    j = np.arange(2 * sub)[:, None] % sub
    s = np.arange(2 * sub)[None, :]
    return np.where(s < sub, j > s, True).astype(np.float32)


def _sb_attn(main3):
    b, s, _ = main3.shape
    tmat = jnp.asarray(_sb_tail_matrix(), BF16)
    return pl.pallas_call(
        functools.partial(_sb_kernel, seq=s),
        grid=(b,),
        in_specs=[_group_spec(s, 0), _group_spec(s, 1), _group_spec(s, 2), _resident(tmat.shape)],
        out_specs=_group_spec(s, 0),
        out_shape=jax.ShapeDtypeStruct((b, s, GROUP), BF16),
        scratch_shapes=[pltpu.VMEM((N_PAIRS, 2 * SB_BQ, LANES), F32),
                        pltpu.VMEM((N_PAIRS, 2 * SB_BQ, SB_SUB), F32)],
        compiler_params=_params(1),
        name="sb_attn",
    )(main3, main3, main3, tmat)


def _fox_kernel(q_ref, aq_ref, k_ref, ak_ref, v_ref, o_ref, m_ref, l_ref, acc_ref, *, seq):
    bq = FOX_BQ
    lane_q = lax.broadcasted_iota(jnp.int32, (bq, LANES), 1)

    def wide(a):
        return jnp.broadcast_to(a, (2 * bq, LANES))

    def q_block(i, outer):
        qstart = pl.multiple_of(i * bq, bq)
        qss = [jnp.concatenate(
            [_split_heads_rows(q_ref[pl.ds(qstart, bq), _pair_lanes(p)], lane_q) * SCALE,
             _split_heads_rows(aq_ref[pl.ds(qstart, bq), _pair_lanes(p)], lane_q)], axis=1)
            for p in range(N_PAIRS)]

        def scores(p, start, n):
            rows = pl.ds(start, n)
            kk = jnp.concatenate([k_ref[rows, _pair_lanes(p)], ak_ref[rows, _pair_lanes(p)]], axis=1)
            return _dot_nt(qss[p], kk), v_ref[rows, _pair_lanes(p)]

        row = lax.broadcasted_iota(jnp.int32, (2 * bq, bq), 0)
        col = lax.broadcasted_iota(jnp.int32, (2 * bq, bq), 1)
        causal = col <= jnp.where(row >= bq, row - bq, row)
        for p in range(N_PAIRS):
            z, vs = scores(p, qstart, bq)
            z = jnp.where(causal, z, NEG)
            m0 = jnp.max(z, axis=1, keepdims=True)
            pr = jnp.exp(z - m0)
            m_ref[p] = wide(m0)
            l_ref[p] = wide(jnp.sum(pr, axis=1, keepdims=True))
            acc_ref[p] = _dot(pr.astype(BF16), vs)

        def step(start, n):
            for p in range(N_PAIRS):
                z, vs = scores(p, start, n)
                m_old = m_ref[p]
                m_new = jnp.maximum(m_old, jnp.max(z, axis=1, keepdims=True))
                alpha = jnp.exp(m_old - m_new)
                pr = jnp.exp(z - jnp.concatenate([m_new] * (n // LANES), axis=1))
                l_ref[p] = alpha * l_ref[p] + jnp.sum(pr, axis=1, keepdims=True)
                acc_ref[p] = alpha * acc_ref[p] + _dot(pr.astype(BF16), vs)
                m_ref[p] = m_new

        @pl.when((i & 1) == 1)
        def _():
            step(pl.multiple_of(qstart - bq, bq), bq)

        def body(t, c):
            step(pl.multiple_of(t * (2 * bq), 2 * bq), 2 * bq)
            return c

        lax.fori_loop(0, lax.shift_right_logical(i, 1), body, 0)
        for p in range(N_PAIRS):
            o_ref[pl.ds(qstart, bq), _pair_lanes(p)] = _merge_heads_rows(
                acc_ref[p] / l_ref[p], lane_q, bq).astype(BF16)
        return outer

    lax.fori_loop(0, seq // bq, q_block, 0)


def _fox_attn(main3, aug3):
    b, s, _ = main3.shape
    return pl.pallas_call(
        functools.partial(_fox_kernel, seq=s),
        grid=(b,),
        in_specs=[_group_spec(s, 6), _group_spec(s, 0), _group_spec(s, 7), _group_spec(s, 1), _group_spec(s, 8)],
        out_specs=_group_spec(s, 0),
        out_shape=jax.ShapeDtypeStruct((b, s, GROUP), BF16),
        scratch_shapes=[pltpu.VMEM((N_PAIRS, 2 * FOX_BQ, LANES), F32)] * 3,
        compiler_params=_params(1),
        name="fox_attn",
    )(main3, aug3, main3, aug3, main3)


def _dil_kernel(*refs, seq):
    n_pat = len(DIL_PATTERNS)
    qkv = refs[:3 * n_pat]
    bias_ref = refs[3 * n_pat]
    o_ref = refs[3 * n_pat + 1]
    scratch = refs[3 * n_pat + 2:]
    n_state = n_pat * N_PAIRS
    u_refs, m_refs, l_refs = scratch[0:n_state], scratch[n_state:2 * n_state], scratch[2 * n_state:3 * n_state]
    blk = DIL_BLOCK
    lane = lax.broadcasted_iota(jnp.int32, (blk, LANES), 1)

    def block(p, pidx, dil, c, n, first):
        q_ref, k_ref, v_ref = qkv[3 * pidx:3 * pidx + 3]
        lanes = _pair_lanes(p)

        def rd(ref, start, size):
            return ref[pl.ds(start, size), lanes] if dil == 1 else ref[c, pl.ds(start, size), lanes]

        qstart = n * blk if isinstance(n, int) else pl.multiple_of(n * blk, blk)
        qs = _split_heads_rows(rd(q_ref, qstart, blk), lane) * SCALE
        h0, h1 = 2 * p, 2 * p + 1
        if first:
            keys, vals = rd(k_ref, 0, blk), rd(v_ref, 0, blk)
            bias = jnp.concatenate([bias_ref[pidx, h0, :, blk:], bias_ref[pidx, h1, :, blk:]], axis=0)
        else:
            kstart = (n - 1) * blk if isinstance(n, int) else pl.multiple_of((n - 1) * blk, blk)
            keys, vals = rd(k_ref, kstart, 2 * blk), rd(v_ref, kstart, 2 * blk)
            bias = jnp.concatenate([bias_ref[pidx, h0], bias_ref[pidx, h1]], axis=0)
        z = _dot_nt(qs, keys) + bias
        m = jnp.max(z, axis=1, keepdims=True)
        pr = jnp.exp(z - m)
        den = jnp.sum(pr, axis=1, keepdims=True)
        u = _dot(pr.astype(BF16), vals)
        wide = lambda a: jnp.broadcast_to(a, (2 * blk, LANES))
        idx = pl.ds(n * (blk * dil) + c, blk, stride=dil) if dil > 1 else pl.ds(qstart, blk)
        st = pidx * N_PAIRS + p
        u_refs[st][idx, :] = _merge_heads_rows(u, lane, blk)
        m_refs[st][idx, :] = _merge_heads_rows(wide(m), lane, blk)
        l_refs[st][idx, :] = _merge_heads_rows(wide(den), lane, blk)

    def blocks(pidx, dil, c, n, first):
        for p in range(N_PAIRS):
            block(p, pidx, dil, c, n, first)

    for pidx, (_, dil) in enumerate(DIL_PATTERNS):
        n_blk = (seq // dil) // blk
        if dil == 1:
            blocks(pidx, dil, 0, 0, True)

            def body(g, carry, pidx=pidx, dil=dil):
                for jj in range(DIL_GEN_GROUP):
                    blocks(pidx, dil, 0, 1 + g * DIL_GEN_GROUP + jj, False)
                return carry
            lax.fori_loop(0, (n_blk - 1) // DIL_GEN_GROUP, body, 0)
        elif n_blk > 1:
            def body(c, carry, pidx=pidx, dil=dil, n_blk=n_blk):
                blocks(pidx, dil, c, 0, True)
                for n in range(1, n_blk):
                    blocks(pidx, dil, c, n, False)
                return carry
            lax.fori_loop(0, dil, body, 0)
        else:
            def body(g, carry, pidx=pidx, dil=dil):
                for jj in range(DIL_RES_GROUP):
                    blocks(pidx, dil, g * DIL_RES_GROUP + jj, 0, True)
                return carry
            lax.fori_loop(0, dil // DIL_RES_GROUP, body, 0)

    def comb(j, carry):
        sl = pl.ds(pl.multiple_of(j * COMB_ROWS, COMB_ROWS), COMB_ROWS)
        for p in range(N_PAIRS):
            sts = [pidx * N_PAIRS + p for pidx in range(n_pat)]
            ms = [m_refs[st][sl, :] for st in sts]
            m_all = functools.reduce(jnp.maximum, ms)
            es = [jnp.exp(mp - m_all) for mp in ms]
            num = sum(e * u_refs[st][sl, :] for e, st in zip(es, sts))
            den = sum(e * l_refs[st][sl, :] for e, st in zip(es, sts))
            o_ref[sl, _pair_lanes(p)] = (num / den).astype(BF16)
        return carry
    lax.fori_loop(0, seq // COMB_ROWS, comb, 0)


def _t5_bucket_of(dist):
    max_exact = REL_BUCKETS // 2
    nf = jnp.maximum(dist, 1).astype(jnp.float32)
    large = max_exact + (jnp.log(nf / max_exact) / math.log(REL_MAX_DIST / max_exact)
                         * (REL_BUCKETS - max_exact)).astype(jnp.int32)
    large = jnp.minimum(large, REL_BUCKETS - 1)
    return jnp.where(dist < max_exact, dist, large)


def _dil_bias_tables(rel_bias):
    blk = DIL_BLOCK
    qi = jnp.arange(blk)[:, None]
    kj = jnp.arange(2 * blk)[None, :]
    sub = qi + blk - kj
    tables = []
    for window, dil in DIL_PATTERNS:
        in_band = (sub >= 0) & (sub <= window // dil)
        bucket = _t5_bucket_of(jnp.maximum(sub, 0) * dil)
        hit = bucket[None, :, :, None] == jnp.arange(REL_BUCKETS)
        bias = jnp.sum(jnp.where(hit, rel_bias.T.astype(F32)[:, None, None, :], 0.0), axis=-1)
        tables.append(jnp.where(in_band[None], bias, NEG))
    return jnp.stack(tables)


def _dil_attn(main3, dl4, dl16, bias_tables):
    b, s, _ = main3.shape
    n_pat = len(DIL_PATTERNS)
    operands, in_specs = [], []
    for role in range(3):
        operands.append(main3)
        in_specs.append(_group_spec(s, DIL_GROUP0 // N_PAIRS + role))
    for arr in (dl4, dl16):
        dil, rows = arr.shape[1], arr.shape[2]
        for role in range(3):
            operands.append(arr)
            in_specs.append(pl.BlockSpec((None, dil, rows, GROUP), lambda bi, role=role: (bi, 0, 0, role)))
    operands.append(bias_tables)
    in_specs.append(_resident(bias_tables.shape))
    return pl.pallas_call(
        functools.partial(_dil_kernel, seq=s),
        grid=(b,),
        in_specs=in_specs,
        out_specs=_group_spec(s, 0),
        out_shape=jax.ShapeDtypeStruct((b, s, GROUP), BF16),
        scratch_shapes=[pltpu.VMEM((s, LANES), F32)] * (3 * n_pat * N_PAIRS),
        compiler_params=_params(1),
        name="dil_attn",
    )(*operands)


def _layer_norm(v, g, b):
    mu = jnp.mean(v, axis=-1, keepdims=True)
    d = v - mu
    var = jnp.mean(d * d, axis=-1, keepdims=True)
    return d * lax.rsqrt(var + LN_EPS) * g + b


def _dense_kernel(x_ref, oa_ref, ob_ref, oc_ref, od_ref, wo_ref, g1_ref, b1_ref,
                  wg_ref, wu_ref, wd_ref, g2_ref, b2_ref, out_ref, h_ref, *, alpha):
    mix = (_dot(oa_ref[...], wo_ref[0:GROUP, :]) + _dot(ob_ref[...], wo_ref[GROUP:2 * GROUP, :])
           + _dot(oc_ref[...], wo_ref[2 * GROUP:3 * GROUP, :]) + _dot(od_ref[...], wo_ref[3 * GROUP:4 * GROUP, :]))
    x1 = _layer_norm(alpha * x_ref[...] + mix, g1_ref[...], b1_ref[...])
    xb = x1.astype(BF16)
    d_ff = wg_ref.shape[1]
    for c0 in range(0, d_ff, FFN_CHUNK):
        g = _dot(xb, wg_ref[:, c0:c0 + FFN_CHUNK])
        u = _dot(xb, wu_ref[:, c0:c0 + FFN_CHUNK])
        h_ref[:, c0:c0 + FFN_CHUNK] = (g * (1.0 / (1.0 + jnp.exp(-g))) * u).astype(BF16)
    y = _dot(h_ref[...], wd_ref[...])
    out_ref[...] = _layer_norm(alpha * x1 + y, g2_ref[...], b2_ref[...])


def _dense(x2d, outs, w_out, g1, b1, w_gate, w_up, w_down, g2, b2, alpha):
    m, d = x2d.shape
    tm = DENSE_TM
    d_ff = w_gate.shape[1]
    row = lambda i: (i, 0)
    vec = lambda a: a[None, :]
    o_spec = pl.BlockSpec((tm, GROUP), row)
    return pl.pallas_call(
        functools.partial(_dense_kernel, alpha=alpha),
        grid=(m // tm,),
        in_specs=[pl.BlockSpec((tm, d), row), o_spec, o_spec, o_spec, o_spec,
                  _resident((d, d)), _resident((1, d)), _resident((1, d)),
                  _resident((d, d_ff)), _resident((d, d_ff)), _resident((d_ff, d)),
                  _resident((1, d)), _resident((1, d))],
        out_specs=pl.BlockSpec((tm, d), row),
        out_shape=jax.ShapeDtypeStruct((m, d), F32),
        scratch_shapes=[pltpu.VMEM((tm, d_ff), BF16)],
        compiler_params=_params(1),
        name="dense",
    )(x2d, *outs, w_out.astype(BF16), vec(g1), vec(b1), w_gate.astype(BF16), w_up.astype(BF16),
      w_down.astype(BF16), vec(g2), vec(b2))


def kernel(x, w_in, f_bias, conv_w, w_out, rel_bias, ln1_g, ln1_b, w_gate, w_up, w_down, ln2_g, ln2_b):
    b, s, d = x.shape
    depth = w_in.shape[0]
    assert d == 4 * GROUP and w_in.shape[2] == 12 * GROUP + N_HEADS
    assert s % PROJ_TM == 0 and s % (2 * max(SB_BQ, FOX_BQ)) == 0 and s % COMB_ROWS == 0
    assert [dil for _, dil in DIL_PATTERNS][0] == 1
    for _, dil in DIL_PATTERNS:
        n_blk = (s // dil) // DIL_BLOCK
        assert n_blk * DIL_BLOCK * dil == s and PROJ_TM % (16 * dil) == 0
        assert (n_blk - 1) % DIL_GEN_GROUP == 0 if dil == 1 else (n_blk > 1 or dil % DIL_RES_GROUP == 0)
    alpha = (2 * depth) ** 0.25
    bias_tables = _dil_bias_tables(rel_bias)
    w_pad = jnp.pad(jnp.swapaxes(w_in, 1, 2), ((0, 0), (0, LANES - N_HEADS), (0, 0))).astype(BF16)
    x2d = x.reshape(b * s, d)
    for layer in range(depth):
        main, dl4, dl16, aug, out_d = _in_proj(x2d, w_pad[layer], f_bias[layer], conv_w[layer], b, s)
        main3 = main.reshape(b, s, main.shape[1])
        out_a = _sb_attn(main3)
        out_b = _dil_attn(main3, dl4, dl16, bias_tables)
        out_c = _fox_attn(main3, aug.reshape(b, s, aug.shape[1]))
        outs = [o.reshape(b * s, GROUP) for o in (out_a, out_b, out_c)] + [out_d]
        x2d = _dense(x2d, outs, w_out[layer], ln1_g[layer], ln1_b[layer], w_gate[layer], w_up[layer],
                     w_down[layer], ln2_g[layer], ln2_b[layer], alpha)
    return x2d.reshape(b, s, d)
```

```python
import functools
import math

import jax
import jax.numpy as jnp
import numpy as np
from jax import lax
from jax.experimental import pallas as pl
from jax.experimental.pallas import tpu as pltpu

HEAD_DIM = 64
N_HEADS = 4
GROUP = N_HEADS * HEAD_DIM
LANES = 128
N_PAIRS = GROUP // LANES
CONV_K = 3
DIL_PATTERNS = ((128, 1), (512, 4), (2048, 16))
DIL_BLOCK = 128
DIL_GROUP0 = 6
REL_BUCKETS = 32
REL_MAX_DIST = 2048
LN_EPS = 1e-5
SCALE = HEAD_DIM ** -0.5
NEG = -1e30
SIGN_BIT = np.uint32(0x80000000)
VMEM_LIMIT = 56 * 1024 * 1024

BF16 = jnp.bfloat16
F32 = jnp.float32

PROJ_TM = 512
DENSE_TM = 512
FFN_CHUNK = 256
CAUSAL_BQ = 256
SB_SUB = 128
DIL_GEN_GROUP = 3
DIL_RES_GROUP = 4
COMB_ROWS = 256


def _dot(a, b):
    return jnp.dot(a, b, preferred_element_type=F32)


def _dot_nt(a, b):
    return lax.dot_general(a, b, (((1,), (1,)), ((), ())), preferred_element_type=F32)


def _resident(shape):
    nd = len(shape)
    return pl.BlockSpec(shape, lambda *_: (0,) * nd, pipeline_mode=pl.Buffered(1))


def _params(n_axes):
    return pltpu.CompilerParams(dimension_semantics=("arbitrary",) * n_axes,
                                vmem_limit_bytes=VMEM_LIMIT)


def _split_heads_rows(q, lane):
    zero = jnp.zeros_like(q)
    return jnp.concatenate([jnp.where(lane < HEAD_DIM, q, zero),
                            jnp.where(lane >= HEAD_DIM, q, zero)], axis=0)


def _merge_heads_rows(a, lane, rows):
    return jnp.where(lane < HEAD_DIM, a[:rows], a[rows:])


def _pair_lanes(p):
    return slice(p * LANES, (p + 1) * LANES)


def _group_spec(seq, g):
    return pl.BlockSpec((None, seq, GROUP), lambda bi: (bi, 0, g))


def _in_proj_kernel(x_ref, w_ref, fb_ref, cw_ref, sel_ref,
                    main_ref, dl4_ref, dl16_ref, aug_ref, od_ref, carry_ref, ubuf_ref, utail_ref, *stage_refs,
                    tiles_per_seq):
    tm = x_ref.shape[0]
    first = (pl.program_id(0) % tiles_per_seq) == 0
    xb = x_ref[...].astype(BF16)

    n_main = main_ref.shape[1]
    cv0 = n_main
    gate0 = n_main + 3 * GROUP
    dl_col0 = DIL_GROUP0 * LANES
    for c0 in range(0, n_main, GROUP):
        r = _dot_nt(xb, w_ref[c0:c0 + GROUP, :])
        main_ref[:, c0:c0 + GROUP] = r.astype(BF16)
        if dl_col0 <= c0 < dl_col0 + 3 * GROUP:
            j = (c0 - dl_col0) // LANES
            stage_refs[j][...] = r[:, :LANES]
            stage_refs[j + 1][...] = r[:, LANES:]

    for dil, ref in ((DIL_PATTERNS[1][1], dl4_ref), (DIL_PATTERNS[2][1], dl16_ref)):
        rows = tm // dil
        for c in range(dil):
            for j in range(len(stage_refs)):
                ref[c, :, j * LANES:(j + 1) * LANES] = (
                    stage_refs[j][pl.ds(c, rows, stride=dil), :].astype(BF16))

    g = _dot_nt(xb, w_ref[gate0:gate0 + LANES, :]) + fb_ref[...]
    logf = jnp.minimum(g, 0.0) - jnp.log(1.0 + jnp.exp(-jnp.abs(g)))
    r_i = lax.broadcasted_iota(jnp.int32, (tm, tm), 0)
    c_i = lax.broadcasted_iota(jnp.int32, (tm, tm), 1)
    tri = jnp.where(c_i <= r_i, 1.0, 0.0).astype(BF16)
    l_hi = logf.astype(BF16)
    l_lo = (logf - l_hi.astype(F32)).astype(BF16)
    csum = _dot(tri, l_hi) + _dot(tri, l_lo)
    prev = jnp.where(first, 0.0, carry_ref[...])
    csum = csum + prev
    carry_ref[...] = csum[tm - 1:tm, :]
    c_hi = csum.astype(BF16)
    c_r = csum - c_hi.astype(F32)
    c_mid = c_r.astype(BF16)
    c_lo = (c_r - c_mid.astype(F32)).astype(BF16)
    ones = jnp.ones((tm, LANES), BF16)
    aug_ref[...] = _dot(jnp.concatenate([c_hi, c_mid, c_lo, ones], axis=1), sel_ref[...]).astype(BF16)

    cvb = _dot_nt(xb, w_ref[cv0:cv0 + GROUP, :])
    cvc = _dot_nt(xb, w_ref[cv0 + GROUP:cv0 + 2 * GROUP, :])
    cvh = _dot_nt(xb, w_ref[cv0 + 2 * GROUP:cv0 + 3 * GROUP, :])
    u = cvc * cvh

    ubuf_ref[0:8, :] = jnp.where(first, 0.0, utail_ref[...])
    ubuf_ref[8:8 + tm, :] = u
    y = (cw_ref[0:1, :] * ubuf_ref[6:6 + tm, :] + cw_ref[1:2, :] * ubuf_ref[7:7 + tm, :]
         + cw_ref[2:3, :] * u)
    od_ref[...] = (cvb * y).astype(BF16)
    utail_ref[...] = u[tm - 8:tm, :]


def _aug_selection():
    sel = np.zeros((4 * LANES, 4 * LANES), np.float32)
    ones_row = 3 * LANES
    for h in range(N_HEADS):
        bq = (h // 2) * LANES + (h % 2) * HEAD_DIM
        bk = 2 * LANES + bq
        for part in range(3):
            sel[part * LANES + h, bq + part] = 1.0
            sel[ones_row, bq + 3 + part] = 1.0
            sel[ones_row, bk + part] = 1.0
            sel[part * LANES + h, bk + 3 + part] = -1.0
    return sel


def _in_proj(x2d, w_pad, f_bias, conv_w, batch, seq):
    m, d = x2d.shape
    tm = PROJ_TM
    tps = seq // tm
    n_main = 9 * GROUP
    d4, d16 = DIL_PATTERNS[1][1], DIL_PATTERNS[2][1]
    fb = jnp.pad(f_bias, (0, LANES - N_HEADS))[None, :]
    cw = jnp.pad(conv_w, ((0, 8 - CONV_K), (0, 0)))
    sel = jnp.asarray(_aug_selection(), BF16)
    row = lambda i: (i, 0)
    res = lambda i: (i // tps, 0, i % tps, 0)
    return pl.pallas_call(
        functools.partial(_in_proj_kernel, tiles_per_seq=tps),
        grid=(m // tm,),
        in_specs=[pl.BlockSpec((tm, d), row), _resident(w_pad.shape),
                  _resident(fb.shape), _resident(cw.shape), _resident(sel.shape)],
        out_specs=[pl.BlockSpec((tm, n_main), row),
                   pl.BlockSpec((None, d4, tm // d4, 3 * GROUP), res),
                   pl.BlockSpec((None, d16, tm // d16, 3 * GROUP), res),
                   pl.BlockSpec((tm, 4 * LANES), row),
                   pl.BlockSpec((tm, GROUP), row)],
        out_shape=[jax.ShapeDtypeStruct((m, n_main), BF16),
                   jax.ShapeDtypeStruct((batch, d4, seq // d4, 3 * GROUP), BF16),
                   jax.ShapeDtypeStruct((batch, d16, seq // d16, 3 * GROUP), BF16),
                   jax.ShapeDtypeStruct((m, 4 * LANES), BF16),
                   jax.ShapeDtypeStruct((m, GROUP), BF16)],
        scratch_shapes=[pltpu.VMEM((1, LANES), F32), pltpu.VMEM((tm + 8, GROUP), F32),
                        pltpu.VMEM((8, GROUP), F32)]
        + [pltpu.VMEM((tm, LANES), F32)] * (3 * N_PAIRS),
        compiler_params=_params(1),
        name="in_proj",
    )(x2d, w_pad, fb, cw, sel)


def _sb_ops(i, qstart, q_ref, k_ref, v_ref, t_ref, o_ref, acc_ref, carry_ref):
    bq, sub = CAUSAL_BQ, SB_SUB
    lane_q = lax.broadcasted_iota(jnp.int32, (bq, LANES), 1)
    qss = []
    for p in range(N_PAIRS):
        qss.append(_split_heads_rows(q_ref[pl.ds(qstart, bq), _pair_lanes(p)], lane_q) * SCALE)
        acc_ref[p] = jnp.zeros((2 * bq, LANES), F32)
        carry_ref[p] = jnp.zeros((2 * bq, sub), F32)

    def group(p, start, g, masked):
        n = g * sub
        z = _dot_nt(qss[p], k_ref[pl.ds(start, n), _pair_lanes(p)])
        neg_abs = pltpu.bitcast(pltpu.bitcast(z, jnp.uint32) | SIGN_BIT, F32)
        sp = jnp.log(1.0 + jnp.exp(neg_abs))
        log_beta = jnp.minimum(z, 0.0) - sp
        log_rest = log_beta - z
        if masked:
            row = lax.broadcasted_iota(jnp.int32, (2 * bq, n), 0)
            col = lax.broadcasted_iota(jnp.int32, (2 * bq, n), 1)
            strict = (start + col) < (qstart + jnp.where(row >= bq, row - bq, row))
            log_rest = jnp.where(strict, log_rest, 0.0)
        hi = log_rest.astype(BF16)
        lo = (log_rest - hi.astype(F32)).astype(BF16)
        carry = carry_ref[p]
        parts = [None] * g
        for j in reversed(range(g)):
            sl = slice(j * sub, (j + 1) * sub)
            cs = _dot(jnp.concatenate([hi[:, sl], lo[:, sl]], axis=1), t_ref[...])
            a = jnp.exp(log_beta[:, sl] + (cs[:, :sub] + carry))
            carry = carry + cs[:, sub:]
            if masked:
                a = jnp.where(strict[:, sl], a, 0.0)
            parts[j] = a.astype(BF16)
        carry_ref[p] = carry
        acc_ref[p] += _dot(jnp.concatenate(parts, axis=1), v_ref[pl.ds(start, n), _pair_lanes(p)])

    def groups(start, g, masked):
        for p in range(N_PAIRS):
            group(p, start, g, masked)

    def finish():
        for p in range(N_PAIRS):
            o_ref[pl.ds(qstart, bq), _pair_lanes(p)] = _merge_heads_rows(acc_ref[p], lane_q, bq).astype(BF16)

    return groups, finish


def _fox_ops(i, qstart, q_ref, aq_ref, k_ref, ak_ref, v_ref, o_ref, m_ref, l_ref, acc_ref):
    bq = CAUSAL_BQ
    lane_q = lax.broadcasted_iota(jnp.int32, (bq, LANES), 1)

    def wide(a):
        return jnp.broadcast_to(a, (2 * bq, LANES))

    qss = [jnp.concatenate(
        [_split_heads_rows(q_ref[pl.ds(qstart, bq), _pair_lanes(p)], lane_q) * SCALE,
         _split_heads_rows(aq_ref[pl.ds(qstart, bq), _pair_lanes(p)], lane_q)], axis=1)
        for p in range(N_PAIRS)]

    def scores(p, start, n):
        rows = pl.ds(start, n)
        kk = jnp.concatenate([k_ref[rows, _pair_lanes(p)], ak_ref[rows, _pair_lanes(p)]], axis=1)
        return _dot_nt(qss[p], kk), v_ref[rows, _pair_lanes(p)]

    def diag():
        row = lax.broadcasted_iota(jnp.int32, (2 * bq, bq), 0)
        col = lax.broadcasted_iota(jnp.int32, (2 * bq, bq), 1)
        causal = col <= jnp.where(row >= bq, row - bq, row)
        for p in range(N_PAIRS):
            z, vs = scores(p, qstart, bq)
            z = jnp.where(causal, z, NEG)
            m0 = jnp.max(z, axis=1, keepdims=True)
            pr = jnp.exp(z - m0)
            m_ref[p] = wide(m0)
            l_ref[p] = wide(jnp.sum(pr, axis=1, keepdims=True))
            acc_ref[p] = _dot(pr.astype(BF16), vs)

    def step(start, n):
        for p in range(N_PAIRS):
            z, vs = scores(p, start, n)
            m_old = m_ref[p]
            m_new = jnp.maximum(m_old, jnp.max(z, axis=1, keepdims=True))
            alpha = jnp.exp(m_old - m_new)
            pr = jnp.exp(z - jnp.concatenate([m_new] * (n // LANES), axis=1))
            l_ref[p] = alpha * l_ref[p] + jnp.sum(pr, axis=1, keepdims=True)
            acc_ref[p] = alpha * acc_ref[p] + _dot(pr.astype(BF16), vs)
            m_ref[p] = m_new

    def finish():
        for p in range(N_PAIRS):
            o_ref[pl.ds(qstart, bq), _pair_lanes(p)] = _merge_heads_rows(
                acc_ref[p] / l_ref[p], lane_q, bq).astype(BF16)

    return diag, step, finish


def _causal_kernel(sq_ref, sk_ref, sv_ref, t_ref, fq_ref, faq_ref, fk_ref, fak_ref, fv_ref,
                   so_ref, fo_ref, s_acc, s_carry, f_m, f_l, f_acc, *, seq):
    bq = CAUSAL_BQ

    def q_block(i, outer):
        qstart = pl.multiple_of(i * bq, bq)
        sb_groups, sb_finish = _sb_ops(i, qstart, sq_ref, sk_ref, sv_ref, t_ref, so_ref, s_acc, s_carry)
        fox_diag, fox_step, fox_finish = _fox_ops(i, qstart, fq_ref, faq_ref, fk_ref, fak_ref, fv_ref,
                                                  fo_ref, f_m, f_l, f_acc)
        sb_groups(qstart, bq // SB_SUB, True)
        fox_diag()

        @pl.when((i & 1) == 1)
        def _():
            prev = pl.multiple_of(qstart - bq, bq)
            sb_groups(prev, bq // SB_SUB, False)
            fox_step(prev, bq)

        n_wide = lax.shift_right_logical(i, 1)

        def body(t, c):
            sb_groups(pl.multiple_of((n_wide - 1 - t) * (2 * bq), 2 * bq), 2 * bq // SB_SUB, False)
            fox_step(pl.multiple_of(t * (2 * bq), 2 * bq), 2 * bq)
            return c

        lax.fori_loop(0, n_wide, body, 0)
        sb_finish()
        fox_finish()
        return outer

    lax.fori_loop(0, seq // bq, q_block, 0)


def _sb_tail_matrix():
    sub = SB_SUB
    j = np.arange(2 * sub)[:, None] % sub
    s = np.arange(2 * sub)[None, :]
    return np.where(s < sub, j > s, True).astype(np.float32)


def _causal_attn(main3, aug3):
    b, s, _ = main3.shape
    tmat = jnp.asarray(_sb_tail_matrix(), BF16)
    state = pltpu.VMEM((N_PAIRS, 2 * CAUSAL_BQ, LANES), F32)
    out = jax.ShapeDtypeStruct((b, s, GROUP), BF16)
    return pl.pallas_call(
        functools.partial(_causal_kernel, seq=s),
        grid=(b,),
        in_specs=[_group_spec(s, 0), _group_spec(s, 1), _group_spec(s, 2), _resident(tmat.shape),
                  _group_spec(s, 6), _group_spec(s, 0), _group_spec(s, 7), _group_spec(s, 1), _group_spec(s, 8)],
        out_specs=[_group_spec(s, 0), _group_spec(s, 0)],
        out_shape=[out, out],
        scratch_shapes=[state, pltpu.VMEM((N_PAIRS, 2 * CAUSAL_BQ, SB_SUB), F32), state, state, state],
        compiler_params=_params(1),
        name="causal_attn",
    )(main3, main3, main3, tmat, main3, aug3, main3, aug3, main3)


def _dil_kernel(*refs, seq):
    n_pat = len(DIL_PATTERNS)
    qkv = refs[:3 * n_pat]
    bias_ref = refs[3 * n_pat]
    o_ref = refs[3 * n_pat + 1]
    scratch = refs[3 * n_pat + 2:]
    n_state = n_pat * N_PAIRS
    u_refs, m_refs, l_refs = scratch[0:n_state], scratch[n_state:2 * n_state], scratch[2 * n_state:3 * n_state]
    blk = DIL_BLOCK
    lane = lax.broadcasted_iota(jnp.int32, (blk, LANES), 1)

    def block(p, pidx, dil, c, n, first):
        q_ref, k_ref, v_ref = qkv[3 * pidx:3 * pidx + 3]
        lanes = _pair_lanes(p)

        def rd(ref, start, size):
            return ref[pl.ds(start, size), lanes] if dil == 1 else ref[c, pl.ds(start, size), lanes]

        qstart = n * blk if isinstance(n, int) else pl.multiple_of(n * blk, blk)
        qs = _split_heads_rows(rd(q_ref, qstart, blk), lane) * SCALE
        h0, h1 = 2 * p, 2 * p + 1
        if first:
            keys, vals = rd(k_ref, 0, blk), rd(v_ref, 0, blk)
            bias = jnp.concatenate([bias_ref[pidx, h0, :, blk:], bias_ref[pidx, h1, :, blk:]], axis=0)
        else:
            kstart = (n - 1) * blk if isinstance(n, int) else pl.multiple_of((n - 1) * blk, blk)
            keys, vals = rd(k_ref, kstart, 2 * blk), rd(v_ref, kstart, 2 * blk)
            bias = jnp.concatenate([bias_ref[pidx, h0], bias_ref[pidx, h1]], axis=0)
        z = _dot_nt(qs, keys) + bias
        m = jnp.max(z, axis=1, keepdims=True)
        pr = jnp.exp(z - m)
        den = jnp.sum(pr, axis=1, keepdims=True)
        u = _dot(pr.astype(BF16), vals)
        wide = lambda a: jnp.broadcast_to(a, (2 * blk, LANES))
        idx = pl.ds(n * (blk * dil) + c, blk, stride=dil) if dil > 1 else pl.ds(qstart, blk)
        st = pidx * N_PAIRS + p
        u_refs[st][idx, :] = _merge_heads_rows(u, lane, blk)
        m_refs[st][idx, :] = _merge_heads_rows(wide(m), lane, blk)
        l_refs[st][idx, :] = _merge_heads_rows(wide(den), lane, blk)

    def blocks(pidx, dil, c, n, first):
        for p in range(N_PAIRS):
            block(p, pidx, dil, c, n, first)

    for pidx, (_, dil) in enumerate(DIL_PATTERNS):
        n_blk = (seq // dil) // blk
        if dil == 1:
            blocks(pidx, dil, 0, 0, True)

            def body(g, carry, pidx=pidx, dil=dil):
                for jj in range(DIL_GEN_GROUP):
                    blocks(pidx, dil, 0, 1 + g * DIL_GEN_GROUP + jj, False)
                return carry
            lax.fori_loop(0, (n_blk - 1) // DIL_GEN_GROUP, body, 0)
        elif n_blk > 1:
            def body(c, carry, pidx=pidx, dil=dil, n_blk=n_blk):
                blocks(pidx, dil, c, 0, True)
                for n in range(1, n_blk):
                    blocks(pidx, dil, c, n, False)
                return carry
            lax.fori_loop(0, dil, body, 0)
        else:
            def body(g, carry, pidx=pidx, dil=dil):
                for jj in range(DIL_RES_GROUP):
                    blocks(pidx, dil, g * DIL_RES_GROUP + jj, 0, True)
                return carry
            lax.fori_loop(0, dil // DIL_RES_GROUP, body, 0)

    def comb(j, carry):
        sl = pl.ds(pl.multiple_of(j * COMB_ROWS, COMB_ROWS), COMB_ROWS)
        for p in range(N_PAIRS):
            sts = [pidx * N_PAIRS + p for pidx in range(n_pat)]
            ms = [m_refs[st][sl, :] for st in sts]
            m_all = functools.reduce(jnp.maximum, ms)
            es = [jnp.exp(mp - m_all) for mp in ms]
            num = sum(e * u_refs[st][sl, :] for e, st in zip(es, sts))
            den = sum(e * l_refs[st][sl, :] for e, st in zip(es, sts))
            o_ref[sl, _pair_lanes(p)] = (num / den).astype(BF16)
        return carry
    lax.fori_loop(0, seq // COMB_ROWS, comb, 0)


def _t5_bucket_of(dist):
    max_exact = REL_BUCKETS // 2
    nf = jnp.maximum(dist, 1).astype(jnp.float32)
    large = max_exact + (jnp.log(nf / max_exact) / math.log(REL_MAX_DIST / max_exact)
                         * (REL_BUCKETS - max_exact)).astype(jnp.int32)
    large = jnp.minimum(large, REL_BUCKETS - 1)
    return jnp.where(dist < max_exact, dist, large)


def _dil_bias_tables(rel_bias):
    blk = DIL_BLOCK
    qi = jnp.arange(blk)[:, None]
    kj = jnp.arange(2 * blk)[None, :]
    sub = qi + blk - kj
    tables = []
    for window, dil in DIL_PATTERNS:
        in_band = (sub >= 0) & (sub <= window // dil)
        bucket = _t5_bucket_of(jnp.maximum(sub, 0) * dil)
        hit = bucket[None, :, :, None] == jnp.arange(REL_BUCKETS)
        bias = jnp.sum(jnp.where(hit, rel_bias.T.astype(F32)[:, None, None, :], 0.0), axis=-1)
        tables.append(jnp.where(in_band[None], bias, NEG))
    return jnp.stack(tables)


def _dil_attn(main3, dl4, dl16, bias_tables):
    b, s, _ = main3.shape
    n_pat = len(DIL_PATTERNS)
    operands, in_specs = [], []
    for role in range(3):
        operands.append(main3)
        in_specs.append(_group_spec(s, DIL_GROUP0 // N_PAIRS + role))
    for arr in (dl4, dl16):
        dil, rows = arr.shape[1], arr.shape[2]
        for role in range(3):
            operands.append(arr)
            in_specs.append(pl.BlockSpec((None, dil, rows, GROUP), lambda bi, role=role: (bi, 0, 0, role)))
    operands.append(bias_tables)
    in_specs.append(_resident(bias_tables.shape))
    return pl.pallas_call(
        functools.partial(_dil_kernel, seq=s),
        grid=(b,),
        in_specs=in_specs,
        out_specs=_group_spec(s, 0),
        out_shape=jax.ShapeDtypeStruct((b, s, GROUP), BF16),
        scratch_shapes=[pltpu.VMEM((s, LANES), F32)] * (3 * n_pat * N_PAIRS),
        compiler_params=_params(1),
        name="dil_attn",
    )(*operands)


def _layer_norm(v, g, b):
    mu = jnp.mean(v, axis=-1, keepdims=True)
    d = v - mu
    var = jnp.mean(d * d, axis=-1, keepdims=True)
    return d * lax.rsqrt(var + LN_EPS) * g + b


def _dense_kernel(x_ref, oa_ref, ob_ref, oc_ref, od_ref, wo_ref, g1_ref, b1_ref,
                  wg_ref, wu_ref, wd_ref, g2_ref, b2_ref, out_ref, h_ref, *, alpha):
    mix = (_dot(oa_ref[...], wo_ref[0:GROUP, :]) + _dot(ob_ref[...], wo_ref[GROUP:2 * GROUP, :])
           + _dot(oc_ref[...], wo_ref[2 * GROUP:3 * GROUP, :]) + _dot(od_ref[...], wo_ref[3 * GROUP:4 * GROUP, :]))
    x1 = _layer_norm(alpha * x_ref[...] + mix, g1_ref[...], b1_ref[...])
    xb = x1.astype(BF16)
    d_ff = wg_ref.shape[1]
    for c0 in range(0, d_ff, FFN_CHUNK):
        g = _dot(xb, wg_ref[:, c0:c0 + FFN_CHUNK])
        u = _dot(xb, wu_ref[:, c0:c0 + FFN_CHUNK])
        h_ref[:, c0:c0 + FFN_CHUNK] = (g * (1.0 / (1.0 + jnp.exp(-g))) * u).astype(BF16)
    y = _dot(h_ref[...], wd_ref[...])
    out_ref[...] = _layer_norm(alpha * x1 + y, g2_ref[...], b2_ref[...])


def _dense(x2d, outs, w_out, g1, b1, w_gate, w_up, w_down, g2, b2, alpha):
    m, d = x2d.shape
    tm = DENSE_TM
    d_ff = w_gate.shape[1]
    row = lambda i: (i, 0)
    vec = lambda a: a[None, :]
    o_spec = pl.BlockSpec((tm, GROUP), row)
    return pl.pallas_call(
        functools.partial(_dense_kernel, alpha=alpha),
        grid=(m // tm,),
        in_specs=[pl.BlockSpec((tm, d), row), o_spec, o_spec, o_spec, o_spec,
                  _resident((d, d)), _resident((1, d)), _resident((1, d)),
                  _resident((d, d_ff)), _resident((d, d_ff)), _resident((d_ff, d)),
                  _resident((1, d)), _resident((1, d))],
        out_specs=pl.BlockSpec((tm, d), row),
        out_shape=jax.ShapeDtypeStruct((m, d), F32),
        scratch_shapes=[pltpu.VMEM((tm, d_ff), BF16)],
        compiler_params=_params(1),
        name="dense",
    )(x2d, *outs, w_out.astype(BF16), vec(g1), vec(b1), w_gate.astype(BF16), w_up.astype(BF16),
      w_down.astype(BF16), vec(g2), vec(b2))


def kernel(x, w_in, f_bias, conv_w, w_out, rel_bias, ln1_g, ln1_b, w_gate, w_up, w_down, ln2_g, ln2_b):
    b, s, d = x.shape
    depth = w_in.shape[0]
    assert d == 4 * GROUP and w_in.shape[2] == 12 * GROUP + N_HEADS
    assert s % PROJ_TM == 0 and s % (2 * CAUSAL_BQ) == 0 and s % COMB_ROWS == 0
    assert [dil for _, dil in DIL_PATTERNS][0] == 1
    for _, dil in DIL_PATTERNS:
        n_blk = (s // dil) // DIL_BLOCK
        assert n_blk * DIL_BLOCK * dil == s and PROJ_TM % (16 * dil) == 0
        assert (n_blk - 1) % DIL_GEN_GROUP == 0 if dil == 1 else (n_blk > 1 or dil % DIL_RES_GROUP == 0)
    alpha = (2 * depth) ** 0.25
    bias_tables = _dil_bias_tables(rel_bias)
    w_pad = jnp.pad(jnp.swapaxes(w_in, 1, 2), ((0, 0), (0, LANES - N_HEADS), (0, 0))).astype(BF16)
    x2d = x.reshape(b * s, d)
    for layer in range(depth):
        main, dl4, dl16, aug, out_d = _in_proj(x2d, w_pad[layer], f_bias[layer], conv_w[layer], b, s)
        main3 = main.reshape(b, s, main.shape[1])
        out_a, out_c = _causal_attn(main3, aug.reshape(b, s, aug.shape[1]))
        out_b = _dil_attn(main3, dl4, dl16, bias_tables)
        outs = [o.reshape(b * s, GROUP) for o in (out_a, out_b, out_c)] + [out_d]
        x2d = _dense(x2d, outs, w_out[layer], ln1_g[layer], ln1_b[layer], w_gate[layer], w_up[layer],
                     w_down[layer], ln2_g[layer], ln2_b[layer], alpha)
    return x2d.reshape(b, s, d)
```

```python
import functools
import math

import jax
import jax.numpy as jnp
import numpy as np
from jax import lax
from jax.experimental import pallas as pl
from jax.experimental.pallas import tpu as pltpu

HEAD_DIM = 64
N_HEADS = 4
GROUP = N_HEADS * HEAD_DIM
LANES = 128
N_PAIRS = GROUP // LANES
CONV_K = 3
DIL_PATTERNS = ((128, 1), (512, 4), (2048, 16))
DIL_BLOCK = 128
DIL_GROUP0 = 6
REL_BUCKETS = 32
REL_MAX_DIST = 2048
LN_EPS = 1e-5
SCALE = HEAD_DIM ** -0.5
NEG = -1e30
SIGN_BIT = np.uint32(0x80000000)
VMEM_LIMIT = 56 * 1024 * 1024

BF16 = jnp.bfloat16
F32 = jnp.float32

PROJ_TM = 512
DENSE_TM = 512
FFN_CHUNK = 256
CAUSAL_BQ = 256
SB_SUB = 128
DIL_GEN_GROUP = 3
DIL_RES_GROUP = 4
COMB_ROWS = 256


def _dot(a, b):
    return jnp.dot(a, b, preferred_element_type=F32)


def _dot_nt(a, b):
    return lax.dot_general(a, b, (((1,), (1,)), ((), ())), preferred_element_type=F32)


def _resident(shape):
    nd = len(shape)
    return pl.BlockSpec(shape, lambda *_: (0,) * nd, pipeline_mode=pl.Buffered(1))


def _params(n_axes):
    return pltpu.CompilerParams(dimension_semantics=("arbitrary",) * n_axes,
                                vmem_limit_bytes=VMEM_LIMIT)


def _split_heads_rows(q, lane):
    zero = jnp.zeros_like(q)
    return jnp.concatenate([jnp.where(lane < HEAD_DIM, q, zero),
                            jnp.where(lane >= HEAD_DIM, q, zero)], axis=0)


def _merge_heads_rows(a, lane, rows):
    return jnp.where(lane < HEAD_DIM, a[:rows], a[rows:])


def _pair_lanes(p):
    return slice(p * LANES, (p + 1) * LANES)


def _group_spec(seq, g):
    return pl.BlockSpec((None, seq, GROUP), lambda bi: (bi, 0, g))


def _in_proj_kernel(x_ref, w_ref, fb_ref, cw_ref, sel_ref,
                    main_ref, dl4_ref, dl16_ref, aug_ref, od_ref, carry_ref, ubuf_ref, utail_ref, *stage_refs,
                    tiles_per_seq):
    tm = x_ref.shape[0]
    first = (pl.program_id(0) % tiles_per_seq) == 0
    xb = x_ref[...].astype(BF16)

    n_main = main_ref.shape[1]
    cv0 = n_main
    gate0 = n_main + 3 * GROUP
    dl_col0 = DIL_GROUP0 * LANES
    for c0 in range(0, n_main, GROUP):
        r = _dot_nt(xb, w_ref[c0:c0 + GROUP, :])
        main_ref[:, c0:c0 + GROUP] = r.astype(BF16)
        if dl_col0 <= c0 < dl_col0 + 3 * GROUP:
            j = (c0 - dl_col0) // LANES
            stage_refs[j][...] = r[:, :LANES]
            stage_refs[j + 1][...] = r[:, LANES:]

    for dil, ref in ((DIL_PATTERNS[1][1], dl4_ref), (DIL_PATTERNS[2][1], dl16_ref)):
        rows = tm // dil
        for c in range(dil):
            for j in range(len(stage_refs)):
                ref[c, :, j * LANES:(j + 1) * LANES] = (
                    stage_refs[j][pl.ds(c, rows, stride=dil), :].astype(BF16))

    g = _dot_nt(xb, w_ref[gate0:gate0 + LANES, :]) + fb_ref[...]
    logf = jnp.minimum(g, 0.0) - jnp.log(1.0 + jnp.exp(-jnp.abs(g)))
    r_i = lax.broadcasted_iota(jnp.int32, (tm, tm), 0)
    c_i = lax.broadcasted_iota(jnp.int32, (tm, tm), 1)
    tri = jnp.where(c_i <= r_i, 1.0, 0.0).astype(BF16)
    l_hi = logf.astype(BF16)
    l_lo = (logf - l_hi.astype(F32)).astype(BF16)
    csum = _dot(tri, l_hi) + _dot(tri, l_lo)
    prev = jnp.where(first, 0.0, carry_ref[...])
    csum = csum + prev
    carry_ref[...] = csum[tm - 1:tm, :]
    c_hi = csum.astype(BF16)
    c_r = csum - c_hi.astype(F32)
    c_mid = c_r.astype(BF16)
    c_lo = (c_r - c_mid.astype(F32)).astype(BF16)
    ones = jnp.ones((tm, LANES), BF16)
    aug_ref[...] = _dot(jnp.concatenate([c_hi, c_mid, c_lo, ones], axis=1), sel_ref[...]).astype(BF16)

    cvb = _dot_nt(xb, w_ref[cv0:cv0 + GROUP, :])
    cvc = _dot_nt(xb, w_ref[cv0 + GROUP:cv0 + 2 * GROUP, :])
    cvh = _dot_nt(xb, w_ref[cv0 + 2 * GROUP:cv0 + 3 * GROUP, :])
    u = cvc * cvh

    ubuf_ref[0:8, :] = jnp.where(first, 0.0, utail_ref[...])
    ubuf_ref[8:8 + tm, :] = u
    y = (cw_ref[0:1, :] * ubuf_ref[6:6 + tm, :] + cw_ref[1:2, :] * ubuf_ref[7:7 + tm, :]
         + cw_ref[2:3, :] * u)
    od_ref[...] = (cvb * y).astype(BF16)
    utail_ref[...] = u[tm - 8:tm, :]


def _aug_selection():
    sel = np.zeros((4 * LANES, 4 * LANES), np.float32)
    ones_row = 3 * LANES
    for h in range(N_HEADS):
        bq = (h // 2) * LANES + (h % 2) * HEAD_DIM
        bk = 2 * LANES + bq
        for part in range(3):
            sel[part * LANES + h, bq + part] = 1.0
            sel[ones_row, bq + 3 + part] = 1.0
            sel[ones_row, bk + part] = 1.0
            sel[part * LANES + h, bk + 3 + part] = -1.0
    return sel


def _in_proj(x2d, w_pad, f_bias, conv_w, batch, seq):
    m, d = x2d.shape
    tm = PROJ_TM
    tps = seq // tm
    n_main = 9 * GROUP
    d4, d16 = DIL_PATTERNS[1][1], DIL_PATTERNS[2][1]
    fb = jnp.pad(f_bias, (0, LANES - N_HEADS))[None, :]
    cw = jnp.pad(conv_w, ((0, 8 - CONV_K), (0, 0)))
    sel = jnp.asarray(_aug_selection(), BF16)
    row = lambda i: (i, 0)
    res = lambda i: (i // tps, 0, i % tps, 0)
    return pl.pallas_call(
        functools.partial(_in_proj_kernel, tiles_per_seq=tps),
        grid=(m // tm,),
        in_specs=[pl.BlockSpec((tm, d), row), _resident(w_pad.shape),
                  _resident(fb.shape), _resident(cw.shape), _resident(sel.shape)],
        out_specs=[pl.BlockSpec((tm, n_main), row),
                   pl.BlockSpec((None, d4, tm // d4, 3 * GROUP), res),
                   pl.BlockSpec((None, d16, tm // d16, 3 * GROUP), res),
                   pl.BlockSpec((tm, 4 * LANES), row),
                   pl.BlockSpec((tm, GROUP), row)],
        out_shape=[jax.ShapeDtypeStruct((m, n_main), BF16),
                   jax.ShapeDtypeStruct((batch, d4, seq // d4, 3 * GROUP), BF16),
                   jax.ShapeDtypeStruct((batch, d16, seq // d16, 3 * GROUP), BF16),
                   jax.ShapeDtypeStruct((m, 4 * LANES), BF16),
                   jax.ShapeDtypeStruct((m, GROUP), BF16)],
        scratch_shapes=[pltpu.VMEM((1, LANES), F32), pltpu.VMEM((tm + 8, GROUP), F32),
                        pltpu.VMEM((8, GROUP), F32)]
        + [pltpu.VMEM((tm, LANES), F32)] * (3 * N_PAIRS),
        compiler_params=_params(1),
        name="in_proj",
    )(x2d, w_pad, fb, cw, sel)


def _sb_ops(i, qstart, q_ref, k_ref, v_ref, t_ref, o_ref, acc_ref, carry_ref):
    bq, sub = CAUSAL_BQ, SB_SUB
    lane_q = lax.broadcasted_iota(jnp.int32, (bq, LANES), 1)
    qss = []
    for p in range(N_PAIRS):
        qss.append(_split_heads_rows(q_ref[pl.ds(qstart, bq), _pair_lanes(p)], lane_q) * SCALE)
        acc_ref[p] = jnp.zeros((2 * bq, LANES), F32)
        carry_ref[p] = jnp.zeros((2 * bq, sub), F32)

    def scores(p, start, g):
        return _dot_nt(qss[p], k_ref[pl.ds(start, g * sub), _pair_lanes(p)])

    def group(p, start, g, masked, z):
        n = g * sub
        neg_abs = pltpu.bitcast(pltpu.bitcast(z, jnp.uint32) | SIGN_BIT, F32)
        sp = jnp.log(1.0 + jnp.exp(neg_abs))
        log_beta = jnp.minimum(z, 0.0) - sp
        log_rest = log_beta - z
        if masked:
            row = lax.broadcasted_iota(jnp.int32, (2 * bq, n), 0)
            col = lax.broadcasted_iota(jnp.int32, (2 * bq, n), 1)
            strict = (start + col) < (qstart + jnp.where(row >= bq, row - bq, row))
            log_rest = jnp.where(strict, log_rest, 0.0)
        hi = log_rest.astype(BF16)
        lo = (log_rest - hi.astype(F32)).astype(BF16)
        carry = carry_ref[p]
        parts = [None] * g
        for j in reversed(range(g)):
            sl = slice(j * sub, (j + 1) * sub)
            cs = _dot(jnp.concatenate([hi[:, sl], lo[:, sl]], axis=1), t_ref[...])
            a = jnp.exp(log_beta[:, sl] + (cs[:, :sub] + carry))
            carry = carry + cs[:, sub:]
            if masked:
                a = jnp.where(strict[:, sl], a, 0.0)
            parts[j] = a.astype(BF16)
        carry_ref[p] = carry
        acc_ref[p] += _dot(jnp.concatenate(parts, axis=1), v_ref[pl.ds(start, n), _pair_lanes(p)])

    def groups(start, g, masked, zs=None):
        for p in range(N_PAIRS):
            group(p, start, g, masked, scores(p, start, g) if zs is None else zs[p])

    def all_scores(start, g):
        return [scores(p, start, g) for p in range(N_PAIRS)]

    def finish():
        for p in range(N_PAIRS):
            o_ref[pl.ds(qstart, bq), _pair_lanes(p)] = _merge_heads_rows(acc_ref[p], lane_q, bq).astype(BF16)

    return groups, all_scores, finish


def _fox_ops(i, qstart, q_ref, aq_ref, k_ref, ak_ref, v_ref, o_ref, m_ref, l_ref, acc_ref):
    bq = CAUSAL_BQ
    lane_q = lax.broadcasted_iota(jnp.int32, (bq, LANES), 1)

    def wide(a):
        return jnp.broadcast_to(a, (2 * bq, LANES))

    qss = [jnp.concatenate(
        [_split_heads_rows(q_ref[pl.ds(qstart, bq), _pair_lanes(p)], lane_q) * SCALE,
         _split_heads_rows(aq_ref[pl.ds(qstart, bq), _pair_lanes(p)], lane_q)], axis=1)
        for p in range(N_PAIRS)]

    def scores(p, start, n):
        rows = pl.ds(start, n)
        kk = jnp.concatenate([k_ref[rows, _pair_lanes(p)], ak_ref[rows, _pair_lanes(p)]], axis=1)
        return _dot_nt(qss[p], kk)

    def values(p, start, n):
        return v_ref[pl.ds(start, n), _pair_lanes(p)]

    def all_scores(start, n):
        return [scores(p, start, n) for p in range(N_PAIRS)]

    def diag(zs=None):
        row = lax.broadcasted_iota(jnp.int32, (2 * bq, bq), 0)
        col = lax.broadcasted_iota(jnp.int32, (2 * bq, bq), 1)
        causal = col <= jnp.where(row >= bq, row - bq, row)
        for p in range(N_PAIRS):
            z = scores(p, qstart, bq) if zs is None else zs[p]
            vs = values(p, qstart, bq)
            z = jnp.where(causal, z, NEG)
            m0 = jnp.max(z, axis=1, keepdims=True)
            pr = jnp.exp(z - m0)
            m_ref[p] = wide(m0)
            l_ref[p] = wide(jnp.sum(pr, axis=1, keepdims=True))
            acc_ref[p] = _dot(pr.astype(BF16), vs)

    def step(start, n, zs=None):
        for p in range(N_PAIRS):
            z = scores(p, start, n) if zs is None else zs[p]
            vs = values(p, start, n)
            m_old = m_ref[p]
            m_new = jnp.maximum(m_old, jnp.max(z, axis=1, keepdims=True))
            alpha = jnp.exp(m_old - m_new)
            pr = jnp.exp(z - jnp.concatenate([m_new] * (n // LANES), axis=1))
            l_ref[p] = alpha * l_ref[p] + jnp.sum(pr, axis=1, keepdims=True)
            acc_ref[p] = alpha * acc_ref[p] + _dot(pr.astype(BF16), vs)
            m_ref[p] = m_new

    def finish():
        for p in range(N_PAIRS):
            o_ref[pl.ds(qstart, bq), _pair_lanes(p)] = _merge_heads_rows(
                acc_ref[p] / l_ref[p], lane_q, bq).astype(BF16)

    return diag, step, all_scores, finish


def _causal_kernel(sq_ref, sk_ref, sv_ref, t_ref, fq_ref, faq_ref, fk_ref, fak_ref, fv_ref,
                   so_ref, fo_ref, s_acc, s_carry, f_m, f_l, f_acc, *, seq):
    bq = CAUSAL_BQ

    def q_block(i, outer):
        qstart = pl.multiple_of(i * bq, bq)
        sb_groups, sb_scores, sb_finish = _sb_ops(i, qstart, sq_ref, sk_ref, sv_ref, t_ref, so_ref,
                                                  s_acc, s_carry)
        fox_diag, fox_step, fox_scores, fox_finish = _fox_ops(i, qstart, fq_ref, faq_ref, fk_ref, fak_ref,
                                                              fv_ref, fo_ref, f_m, f_l, f_acc)
        def both(s_start, f_start, n, masked):
            wide = n > bq
            s_z = sb_scores(s_start, n // SB_SUB) if wide else None
            f_z = fox_scores(f_start, n) if wide else None
            sb_groups(s_start, n // SB_SUB, masked, s_z)
            if masked:
                fox_diag(f_z)
            else:
                fox_step(f_start, n, f_z)

        both(qstart, qstart, bq, True)

        @pl.when((i & 1) == 1)
        def _():
            prev = pl.multiple_of(qstart - bq, bq)
            both(prev, prev, bq, False)

        n_wide = lax.shift_right_logical(i, 1)

        def body(t, c):
            both(pl.multiple_of((n_wide - 1 - t) * (2 * bq), 2 * bq), pl.multiple_of(t * (2 * bq), 2 * bq),
                 2 * bq, False)
            return c

        lax.fori_loop(0, n_wide, body, 0)
        sb_finish()
        fox_finish()
        return outer

    lax.fori_loop(0, seq // bq, q_block, 0)


def _sb_tail_matrix():
    sub = SB_SUB
    j = np.arange(2 * sub)[:, None] % sub
    s = np.arange(2 * sub)[None, :]
    return np.where(s < sub, j > s, True).astype(np.float32)


def _causal_attn(main3, aug3):
    b, s, _ = main3.shape
    tmat = jnp.asarray(_sb_tail_matrix(), BF16)
    state = pltpu.VMEM((N_PAIRS, 2 * CAUSAL_BQ, LANES), F32)
    out = jax.ShapeDtypeStruct((b, s, GROUP), BF16)
    return pl.pallas_call(
        functools.partial(_causal_kernel, seq=s),
        grid=(b,),
        in_specs=[_group_spec(s, 0), _group_spec(s, 1), _group_spec(s, 2), _resident(tmat.shape),
                  _group_spec(s, 6), _group_spec(s, 0), _group_spec(s, 7), _group_spec(s, 1), _group_spec(s, 8)],
        out_specs=[_group_spec(s, 0), _group_spec(s, 0)],
        out_shape=[out, out],
        scratch_shapes=[state, pltpu.VMEM((N_PAIRS, 2 * CAUSAL_BQ, SB_SUB), F32), state, state, state],
        compiler_params=_params(1),
        name="causal_attn",
    )(main3, main3, main3, tmat, main3, aug3, main3, aug3, main3)


def _dil_kernel(*refs, seq):
    n_pat = len(DIL_PATTERNS)
    qkv = refs[:3 * n_pat]
    bias_ref = refs[3 * n_pat]
    o_ref = refs[3 * n_pat + 1]
    scratch = refs[3 * n_pat + 2:]
    n_state = n_pat * N_PAIRS
    u_refs, m_refs, l_refs = scratch[0:n_state], scratch[n_state:2 * n_state], scratch[2 * n_state:3 * n_state]
    blk = DIL_BLOCK
    lane = lax.broadcasted_iota(jnp.int32, (blk, LANES), 1)

    def block(p, pidx, dil, c, n, first):
        q_ref, k_ref, v_ref = qkv[3 * pidx:3 * pidx + 3]
        lanes = _pair_lanes(p)

        def rd(ref, start, size):
            return ref[pl.ds(start, size), lanes] if dil == 1 else ref[c, pl.ds(start, size), lanes]

        qstart = n * blk if isinstance(n, int) else pl.multiple_of(n * blk, blk)
        qs = _split_heads_rows(rd(q_ref, qstart, blk), lane) * SCALE
        h0, h1 = 2 * p, 2 * p + 1
        if first:
            keys, vals = rd(k_ref, 0, blk), rd(v_ref, 0, blk)
            bias = jnp.concatenate([bias_ref[pidx, h0, :, blk:], bias_ref[pidx, h1, :, blk:]], axis=0)
        else:
            kstart = (n - 1) * blk if isinstance(n, int) else pl.multiple_of((n - 1) * blk, blk)
            keys, vals = rd(k_ref, kstart, 2 * blk), rd(v_ref, kstart, 2 * blk)
            bias = jnp.concatenate([bias_ref[pidx, h0], bias_ref[pidx, h1]], axis=0)
        z = _dot_nt(qs, keys) + bias
        m = jnp.max(z, axis=1, keepdims=True)
        pr = jnp.exp(z - m)
        wide = lambda a: jnp.broadcast_to(a, (2 * blk, LANES))
        if (seq // dil) // blk == 1:
            den = wide(jnp.sum(pr, axis=1, keepdims=True))
            u = _dot(pr.astype(BF16), vals)
        else:
            ones = jnp.ones((vals.shape[0], LANES), BF16)
            u = _dot(pr.astype(BF16), jnp.concatenate([vals, ones], axis=1))
            u, den = u[:, :LANES], u[:, LANES:]
        idx = pl.ds(n * (blk * dil) + c, blk, stride=dil) if dil > 1 else pl.ds(qstart, blk)
        st = pidx * N_PAIRS + p
        u_refs[st][idx, :] = _merge_heads_rows(u, lane, blk)
        m_refs[st][idx, :] = _merge_heads_rows(wide(m), lane, blk)
        l_refs[st][idx, :] = _merge_heads_rows(den, lane, blk)

    def blocks(pidx, dil, c, n, first):
        for p in range(N_PAIRS):
            block(p, pidx, dil, c, n, first)

    for pidx, (_, dil) in enumerate(DIL_PATTERNS):
        n_blk = (seq // dil) // blk
        if dil == 1:
            blocks(pidx, dil, 0, 0, True)

            def body(g, carry, pidx=pidx, dil=dil):
                for jj in range(DIL_GEN_GROUP):
                    blocks(pidx, dil, 0, 1 + g * DIL_GEN_GROUP + jj, False)
                return carry
            lax.fori_loop(0, (n_blk - 1) // DIL_GEN_GROUP, body, 0)
        elif n_blk > 1:
            def body(c, carry, pidx=pidx, dil=dil, n_blk=n_blk):
                blocks(pidx, dil, c, 0, True)
                for n in range(1, n_blk):
                    blocks(pidx, dil, c, n, False)
                return carry
            lax.fori_loop(0, dil, body, 0)
        else:
            def body(g, carry, pidx=pidx, dil=dil):
                for jj in range(DIL_RES_GROUP):
                    blocks(pidx, dil, g * DIL_RES_GROUP + jj, 0, True)
                return carry
            lax.fori_loop(0, dil // DIL_RES_GROUP, body, 0)

    def comb(j, carry):
        sl = pl.ds(pl.multiple_of(j * COMB_ROWS, COMB_ROWS), COMB_ROWS)
        for p in range(N_PAIRS):
            sts = [pidx * N_PAIRS + p for pidx in range(n_pat)]
            ms = [m_refs[st][sl, :] for st in sts]
            m_all = functools.reduce(jnp.maximum, ms)
            es = [jnp.exp(mp - m_all) for mp in ms]
            num = sum(e * u_refs[st][sl, :] for e, st in zip(es, sts))
            den = sum(e * l_refs[st][sl, :] for e, st in zip(es, sts))
            o_ref[sl, _pair_lanes(p)] = (num / den).astype(BF16)
        return carry
    lax.fori_loop(0, seq // COMB_ROWS, comb, 0)


def _t5_bucket_of(dist):
    max_exact = REL_BUCKETS // 2
    nf = jnp.maximum(dist, 1).astype(jnp.float32)
    large = max_exact + (jnp.log(nf / max_exact) / math.log(REL_MAX_DIST / max_exact)
                         * (REL_BUCKETS - max_exact)).astype(jnp.int32)
    large = jnp.minimum(large, REL_BUCKETS - 1)
    return jnp.where(dist < max_exact, dist, large)


def _dil_bias_tables(rel_bias):
    blk = DIL_BLOCK
    qi = jnp.arange(blk)[:, None]
    kj = jnp.arange(2 * blk)[None, :]
    sub = qi + blk - kj
    tables = []
    for window, dil in DIL_PATTERNS:
        in_band = (sub >= 0) & (sub <= window // dil)
        bucket = _t5_bucket_of(jnp.maximum(sub, 0) * dil)
        hit = bucket[None, :, :, None] == jnp.arange(REL_BUCKETS)
        bias = jnp.sum(jnp.where(hit, rel_bias.T.astype(F32)[:, None, None, :], 0.0), axis=-1)
        tables.append(jnp.where(in_band[None], bias, NEG))
    return jnp.stack(tables)


def _dil_attn(main3, dl4, dl16, bias_tables):
    b, s, _ = main3.shape
    n_pat = len(DIL_PATTERNS)
    operands, in_specs = [], []
    for role in range(3):
        operands.append(main3)
        in_specs.append(_group_spec(s, DIL_GROUP0 // N_PAIRS + role))
    for arr in (dl4, dl16):
        dil, rows = arr.shape[1], arr.shape[2]
        for role in range(3):
            operands.append(arr)
            in_specs.append(pl.BlockSpec((None, dil, rows, GROUP), lambda bi, role=role: (bi, 0, 0, role)))
    operands.append(bias_tables)
    in_specs.append(_resident(bias_tables.shape))
    return pl.pallas_call(
        functools.partial(_dil_kernel, seq=s),
        grid=(b,),
        in_specs=in_specs,
        out_specs=_group_spec(s, 0),
        out_shape=jax.ShapeDtypeStruct((b, s, GROUP), BF16),
        scratch_shapes=[pltpu.VMEM((s, LANES), F32)] * (3 * n_pat * N_PAIRS),
        compiler_params=_params(1),
        name="dil_attn",
    )(*operands)


def _layer_norm(v, g, b):
    mu = jnp.mean(v, axis=-1, keepdims=True)
    d = v - mu
    var = jnp.mean(d * d, axis=-1, keepdims=True)
    return d * lax.rsqrt(var + LN_EPS) * g + b


def _dense_kernel(x_ref, oa_ref, ob_ref, oc_ref, od_ref, wo_ref, g1_ref, b1_ref,
                  wg_ref, wu_ref, wd_ref, g2_ref, b2_ref, out_ref, h_ref, *, alpha):
    mix = (_dot(oa_ref[...], wo_ref[0:GROUP, :]) + _dot(ob_ref[...], wo_ref[GROUP:2 * GROUP, :])
           + _dot(oc_ref[...], wo_ref[2 * GROUP:3 * GROUP, :]) + _dot(od_ref[...], wo_ref[3 * GROUP:4 * GROUP, :]))
    x1 = _layer_norm(alpha * x_ref[...] + mix, g1_ref[...], b1_ref[...])
    xb = x1.astype(BF16)
    d_ff = wg_ref.shape[1]
    for c0 in range(0, d_ff, FFN_CHUNK):
        g = _dot(xb, wg_ref[:, c0:c0 + FFN_CHUNK])
        u = _dot(xb, wu_ref[:, c0:c0 + FFN_CHUNK])
        h_ref[:, c0:c0 + FFN_CHUNK] = (g * (1.0 / (1.0 + jnp.exp(-g))) * u).astype(BF16)
    y = _dot(h_ref[...], wd_ref[...])
    out_ref[...] = _layer_norm(alpha * x1 + y, g2_ref[...], b2_ref[...])


def _dense(x2d, outs, w_out, g1, b1, w_gate, w_up, w_down, g2, b2, alpha):
    m, d = x2d.shape
    tm = DENSE_TM
    d_ff = w_gate.shape[1]
    row = lambda i: (i, 0)
    vec = lambda a: a[None, :]
    o_spec = pl.BlockSpec((tm, GROUP), row)
    return pl.pallas_call(
        functools.partial(_dense_kernel, alpha=alpha),
        grid=(m // tm,),
        in_specs=[pl.BlockSpec((tm, d), row), o_spec, o_spec, o_spec, o_spec,
                  _resident((d, d)), _resident((1, d)), _resident((1, d)),
                  _resident((d, d_ff)), _resident((d, d_ff)), _resident((d_ff, d)),
                  _resident((1, d)), _resident((1, d))],
        out_specs=pl.BlockSpec((tm, d), row),
        out_shape=jax.ShapeDtypeStruct((m, d), F32),
        scratch_shapes=[pltpu.VMEM((tm, d_ff), BF16)],
        compiler_params=_params(1),
        name="dense",
    )(x2d, *outs, w_out.astype(BF16), vec(g1), vec(b1), w_gate.astype(BF16), w_up.astype(BF16),
      w_down.astype(BF16), vec(g2), vec(b2))


def kernel(x, w_in, f_bias, conv_w, w_out, rel_bias, ln1_g, ln1_b, w_gate, w_up, w_down, ln2_g, ln2_b):
    b, s, d = x.shape
    depth = w_in.shape[0]
    assert d == 4 * GROUP and w_in.shape[2] == 12 * GROUP + N_HEADS
    assert s % PROJ_TM == 0 and s % (2 * CAUSAL_BQ) == 0 and s % COMB_ROWS == 0
    assert [dil for _, dil in DIL_PATTERNS][0] == 1
    for _, dil in DIL_PATTERNS:
        n_blk = (s // dil) // DIL_BLOCK
        assert n_blk * DIL_BLOCK * dil == s and PROJ_TM % (16 * dil) == 0
        assert (n_blk - 1) % DIL_GEN_GROUP == 0 if dil == 1 else (n_blk > 1 or dil % DIL_RES_GROUP == 0)
    alpha = (2 * depth) ** 0.25
    bias_tables = _dil_bias_tables(rel_bias)
    w_pad = jnp.pad(jnp.swapaxes(w_in, 1, 2), ((0, 0), (0, LANES - N_HEADS), (0, 0))).astype(BF16)
    x2d = x.reshape(b * s, d)
    for layer in range(depth):
        main, dl4, dl16, aug, out_d = _in_proj(x2d, w_pad[layer], f_bias[layer], conv_w[layer], b, s)
        main3 = main.reshape(b, s, main.shape[1])
        out_a, out_c = _causal_attn(main3, aug.reshape(b, s, aug.shape[1]))
        out_b = _dil_attn(main3, dl4, dl16, bias_tables)
        outs = [o.reshape(b * s, GROUP) for o in (out_a, out_b, out_c)] + [out_d]
        x2d = _dense(x2d, outs, w_out[layer], ln1_g[layer], ln1_b[layer], w_gate[layer], w_up[layer],
                     w_down[layer], ln2_g[layer], ln2_b[layer], alpha)
    return x2d.reshape(b, s, d)
```

```python
import functools
import math

import jax
import jax.numpy as jnp
import numpy as np
from jax import lax
from jax.experimental import pallas as pl
from jax.experimental.pallas import tpu as pltpu

HEAD_DIM = 64
N_HEADS = 4
GROUP = N_HEADS * HEAD_DIM
LANES = 128
N_PAIRS = GROUP // LANES
CONV_K = 3
DIL_PATTERNS = ((128, 1), (512, 4), (2048, 16))
DIL_BLOCK = 128
DIL_GROUP0 = 6
REL_BUCKETS = 32
REL_MAX_DIST = 2048
LN_EPS = 1e-5
SCALE = HEAD_DIM ** -0.5
NEG = -1e30
SIGN_BIT = np.uint32(0x80000000)
VMEM_LIMIT = 56 * 1024 * 1024

BF16 = jnp.bfloat16
F32 = jnp.float32

PROJ_TM = 512
DENSE_TM = 512
FFN_CHUNK = 256
CAUSAL_BQ = 256
SB_SUB = 128
SB_DEAD = 120.0
DIL_GEN_GROUP = 3
DIL_RES_GROUP = 4
COMB_ROWS = 256


def _dot(a, b):
    return jnp.dot(a, b, preferred_element_type=F32)


def _dot_nt(a, b):
    return lax.dot_general(a, b, (((1,), (1,)), ((), ())), preferred_element_type=F32)


def _resident(shape):
    nd = len(shape)
    return pl.BlockSpec(shape, lambda *_: (0,) * nd, pipeline_mode=pl.Buffered(1))


def _params(n_axes):
    return pltpu.CompilerParams(dimension_semantics=("arbitrary",) * n_axes,
                                vmem_limit_bytes=VMEM_LIMIT)


def _split_heads_rows(q, lane):
    zero = jnp.zeros_like(q)
    return jnp.concatenate([jnp.where(lane < HEAD_DIM, q, zero),
                            jnp.where(lane >= HEAD_DIM, q, zero)], axis=0)


def _merge_heads_rows(a, lane, rows):
    return jnp.where(lane < HEAD_DIM, a[:rows], a[rows:])


def _pair_lanes(p):
    return slice(p * LANES, (p + 1) * LANES)


def _group_spec(seq, g):
    return pl.BlockSpec((None, seq, GROUP), lambda bi: (bi, 0, g))


def _in_proj_kernel(x_ref, w_ref, fb_ref, cw_ref, sel_ref,
                    main_ref, dl4_ref, dl16_ref, aug_ref, od_ref, carry_ref, ubuf_ref, utail_ref, *stage_refs,
                    tiles_per_seq):
    tm = x_ref.shape[0]
    first = (pl.program_id(0) % tiles_per_seq) == 0
    xb = x_ref[...].astype(BF16)

    n_main = main_ref.shape[1]
    cv0 = n_main
    gate0 = n_main + 3 * GROUP
    dl_col0 = DIL_GROUP0 * LANES
    for c0 in range(0, n_main, GROUP):
        r = _dot_nt(xb, w_ref[c0:c0 + GROUP, :])
        main_ref[:, c0:c0 + GROUP] = r.astype(BF16)
        if dl_col0 <= c0 < dl_col0 + 3 * GROUP:
            j = (c0 - dl_col0) // LANES
            stage_refs[j][...] = r[:, :LANES]
            stage_refs[j + 1][...] = r[:, LANES:]

    for dil, ref in ((DIL_PATTERNS[1][1], dl4_ref), (DIL_PATTERNS[2][1], dl16_ref)):
        rows = tm // dil
        for c in range(dil):
            for j in range(len(stage_refs)):
                ref[c, :, j * LANES:(j + 1) * LANES] = (
                    stage_refs[j][pl.ds(c, rows, stride=dil), :].astype(BF16))

    g = _dot_nt(xb, w_ref[gate0:gate0 + LANES, :]) + fb_ref[...]
    logf = jnp.minimum(g, 0.0) - jnp.log(1.0 + jnp.exp(-jnp.abs(g)))
    r_i = lax.broadcasted_iota(jnp.int32, (tm, tm), 0)
    c_i = lax.broadcasted_iota(jnp.int32, (tm, tm), 1)
    tri = jnp.where(c_i <= r_i, 1.0, 0.0).astype(BF16)
    l_hi = logf.astype(BF16)
    l_lo = (logf - l_hi.astype(F32)).astype(BF16)
    csum = _dot(tri, l_hi) + _dot(tri, l_lo)
    prev = jnp.where(first, 0.0, carry_ref[...])
    csum = csum + prev
    carry_ref[...] = csum[tm - 1:tm, :]
    c_hi = csum.astype(BF16)
    c_r = csum - c_hi.astype(F32)
    c_mid = c_r.astype(BF16)
    c_lo = (c_r - c_mid.astype(F32)).astype(BF16)
    ones = jnp.ones((tm, LANES), BF16)
    aug_ref[...] = _dot(jnp.concatenate([c_hi, c_mid, c_lo, ones], axis=1), sel_ref[...]).astype(BF16)

    cvb = _dot_nt(xb, w_ref[cv0:cv0 + GROUP, :])
    cvc = _dot_nt(xb, w_ref[cv0 + GROUP:cv0 + 2 * GROUP, :])
    cvh = _dot_nt(xb, w_ref[cv0 + 2 * GROUP:cv0 + 3 * GROUP, :])
    u = cvc * cvh

    ubuf_ref[0:8, :] = jnp.where(first, 0.0, utail_ref[...])
    ubuf_ref[8:8 + tm, :] = u
    y = (cw_ref[0:1, :] * ubuf_ref[6:6 + tm, :] + cw_ref[1:2, :] * ubuf_ref[7:7 + tm, :]
         + cw_ref[2:3, :] * u)
    od_ref[...] = (cvb * y).astype(BF16)
    utail_ref[...] = u[tm - 8:tm, :]


def _aug_selection():
    sel = np.zeros((4 * LANES, 4 * LANES), np.float32)
    ones_row = 3 * LANES
    for h in range(N_HEADS):
        bq = (h // 2) * LANES + (h % 2) * HEAD_DIM
        bk = 2 * LANES + bq
        for part in range(3):
            sel[part * LANES + h, bq + part] = 1.0
            sel[ones_row, bq + 3 + part] = 1.0
            sel[ones_row, bk + part] = 1.0
            sel[part * LANES + h, bk + 3 + part] = -1.0
    return sel


def _in_proj(x2d, w_pad, f_bias, conv_w, batch, seq):
    m, d = x2d.shape
    tm = PROJ_TM
    tps = seq // tm
    n_main = 9 * GROUP
    d4, d16 = DIL_PATTERNS[1][1], DIL_PATTERNS[2][1]
    fb = jnp.pad(f_bias, (0, LANES - N_HEADS))[None, :]
    cw = jnp.pad(conv_w, ((0, 8 - CONV_K), (0, 0)))
    sel = jnp.asarray(_aug_selection(), BF16)
    row = lambda i: (i, 0)
    res = lambda i: (i // tps, 0, i % tps, 0)
    return pl.pallas_call(
        functools.partial(_in_proj_kernel, tiles_per_seq=tps),
        grid=(m // tm,),
        in_specs=[pl.BlockSpec((tm, d), row), _resident(w_pad.shape),
                  _resident(fb.shape), _resident(cw.shape), _resident(sel.shape)],
        out_specs=[pl.BlockSpec((tm, n_main), row),
                   pl.BlockSpec((None, d4, tm // d4, 3 * GROUP), res),
                   pl.BlockSpec((None, d16, tm // d16, 3 * GROUP), res),
                   pl.BlockSpec((tm, 4 * LANES), row),
                   pl.BlockSpec((tm, GROUP), row)],
        out_shape=[jax.ShapeDtypeStruct((m, n_main), BF16),
                   jax.ShapeDtypeStruct((batch, d4, seq // d4, 3 * GROUP), BF16),
                   jax.ShapeDtypeStruct((batch, d16, seq // d16, 3 * GROUP), BF16),
                   jax.ShapeDtypeStruct((m, 4 * LANES), BF16),
                   jax.ShapeDtypeStruct((m, GROUP), BF16)],
        scratch_shapes=[pltpu.VMEM((1, LANES), F32), pltpu.VMEM((tm + 8, GROUP), F32),
                        pltpu.VMEM((8, GROUP), F32)]
        + [pltpu.VMEM((tm, LANES), F32)] * (3 * N_PAIRS),
        compiler_params=_params(1),
        name="in_proj",
    )(x2d, w_pad, fb, cw, sel)


def _sb_ops(i, qstart, q_ref, k_ref, v_ref, t_ref, o_ref, acc_ref, carry_ref, live_ref):
    bq, sub = CAUSAL_BQ, SB_SUB
    lane_q = lax.broadcasted_iota(jnp.int32, (bq, LANES), 1)
    qss = []
    for p in range(N_PAIRS):
        qss.append(_split_heads_rows(q_ref[pl.ds(qstart, bq), _pair_lanes(p)], lane_q) * SCALE)
        acc_ref[p] = jnp.zeros((2 * bq, LANES), F32)
        carry_ref[p] = jnp.zeros((2 * bq, sub), F32)
    live_ref[0] = 1

    def scores(p, start, g):
        return _dot_nt(qss[p], k_ref[pl.ds(start, g * sub), _pair_lanes(p)])

    def group(p, start, g, masked, z):
        n = g * sub
        neg_abs = pltpu.bitcast(pltpu.bitcast(z, jnp.uint32) | SIGN_BIT, F32)
        sp = jnp.log(1.0 + jnp.exp(neg_abs))
        log_beta = jnp.minimum(z, 0.0) - sp
        log_rest = log_beta - z
        if masked:
            row = lax.broadcasted_iota(jnp.int32, (2 * bq, n), 0)
            col = lax.broadcasted_iota(jnp.int32, (2 * bq, n), 1)
            strict = (start + col) < (qstart + jnp.where(row >= bq, row - bq, row))
            log_rest = jnp.where(strict, log_rest, 0.0)
        hi = log_rest.astype(BF16)
        lo = (log_rest - hi.astype(F32)).astype(BF16)
        carry = carry_ref[p]
        parts = [None] * g
        for j in reversed(range(g)):
            sl = slice(j * sub, (j + 1) * sub)
            cs = _dot(jnp.concatenate([hi[:, sl], lo[:, sl]], axis=1), t_ref[...])
            a = jnp.exp(log_beta[:, sl] + (cs[:, :sub] + carry))
            carry = carry + cs[:, sub:]
            if masked:
                a = jnp.where(strict[:, sl], a, 0.0)
            parts[j] = a.astype(BF16)
        carry_ref[p] = carry
        acc_ref[p] += _dot(jnp.concatenate(parts, axis=1), v_ref[pl.ds(start, n), _pair_lanes(p)])

    def groups(start, g, masked, zs=None):
        for p in range(N_PAIRS):
            group(p, start, g, masked, scores(p, start, g) if zs is None else zs[p])
        if not masked:
            top = functools.reduce(jnp.maximum, [jnp.max(carry_ref[p]) for p in range(N_PAIRS)])
            live_ref[0] = jnp.where(top >= -SB_DEAD, 1, 0)

    def all_scores(start, g):
        return [scores(p, start, g) for p in range(N_PAIRS)]

    def is_live():
        return live_ref[0] == 1

    def finish():
        for p in range(N_PAIRS):
            o_ref[pl.ds(qstart, bq), _pair_lanes(p)] = _merge_heads_rows(acc_ref[p], lane_q, bq).astype(BF16)

    return groups, all_scores, is_live, finish


def _fox_ops(i, qstart, q_ref, aq_ref, k_ref, ak_ref, v_ref, o_ref, m_ref, l_ref, acc_ref):
    bq = CAUSAL_BQ
    lane_q = lax.broadcasted_iota(jnp.int32, (bq, LANES), 1)

    def wide(a):
        return jnp.broadcast_to(a, (2 * bq, LANES))

    qss = [jnp.concatenate(
        [_split_heads_rows(q_ref[pl.ds(qstart, bq), _pair_lanes(p)], lane_q) * SCALE,
         _split_heads_rows(aq_ref[pl.ds(qstart, bq), _pair_lanes(p)], lane_q)], axis=1)
        for p in range(N_PAIRS)]

    def scores(p, start, n):
        rows = pl.ds(start, n)
        kk = jnp.concatenate([k_ref[rows, _pair_lanes(p)], ak_ref[rows, _pair_lanes(p)]], axis=1)
        return _dot_nt(qss[p], kk)

    def values(p, start, n):
        return v_ref[pl.ds(start, n), _pair_lanes(p)]

    def all_scores(start, n):
        return [scores(p, start, n) for p in range(N_PAIRS)]

    def diag(zs=None):
        row = lax.broadcasted_iota(jnp.int32, (2 * bq, bq), 0)
        col = lax.broadcasted_iota(jnp.int32, (2 * bq, bq), 1)
        causal = col <= jnp.where(row >= bq, row - bq, row)
        for p in range(N_PAIRS):
            z = scores(p, qstart, bq) if zs is None else zs[p]
            vs = values(p, qstart, bq)
            z = jnp.where(causal, z, NEG)
            m0 = jnp.max(z, axis=1, keepdims=True)
            pr = jnp.exp(z - m0)
            m_ref[p] = wide(m0)
            l_ref[p] = wide(jnp.sum(pr, axis=1, keepdims=True))
            acc_ref[p] = _dot(pr.astype(BF16), vs)

    def step(start, n, zs=None):
        for p in range(N_PAIRS):
            z = scores(p, start, n) if zs is None else zs[p]
            vs = values(p, start, n)
            m_old = m_ref[p]
            m_new = jnp.maximum(m_old, jnp.max(z, axis=1, keepdims=True))
            alpha = jnp.exp(m_old - m_new)
            pr = jnp.exp(z - jnp.concatenate([m_new] * (n // LANES), axis=1))
            l_ref[p] = alpha * l_ref[p] + jnp.sum(pr, axis=1, keepdims=True)
            acc_ref[p] = alpha * acc_ref[p] + _dot(pr.astype(BF16), vs)
            m_ref[p] = m_new

    def finish():
        for p in range(N_PAIRS):
            o_ref[pl.ds(qstart, bq), _pair_lanes(p)] = _merge_heads_rows(
                acc_ref[p] / l_ref[p], lane_q, bq).astype(BF16)

    return diag, step, all_scores, finish


def _causal_kernel(sq_ref, sk_ref, sv_ref, t_ref, fq_ref, faq_ref, fk_ref, fak_ref, fv_ref,
                   so_ref, fo_ref, s_acc, s_carry, s_live, f_m, f_l, f_acc, *, seq):
    bq = CAUSAL_BQ

    def q_block(i, outer):
        qstart = pl.multiple_of(i * bq, bq)
        sb_groups, sb_scores, sb_live, sb_finish = _sb_ops(i, qstart, sq_ref, sk_ref, sv_ref, t_ref, so_ref,
                                                           s_acc, s_carry, s_live)
        fox_diag, fox_step, fox_scores, fox_finish = _fox_ops(i, qstart, fq_ref, faq_ref, fk_ref, fak_ref,
                                                              fv_ref, fo_ref, f_m, f_l, f_acc)
        sb_groups(qstart, bq // SB_SUB, True)
        fox_diag()

        @pl.when((i & 1) == 1)
        def _():
            prev = pl.multiple_of(qstart - bq, bq)
            sb_groups(prev, bq // SB_SUB, False)
            fox_step(prev, bq)

        n_wide = lax.shift_right_logical(i, 1)

        def body(t, c):
            start = pl.multiple_of((n_wide - 1 - t) * (2 * bq), 2 * bq)
            fox_step(start, 2 * bq)

            @pl.when(sb_live())
            def _():
                sb_groups(start, 2 * bq // SB_SUB, False, sb_scores(start, 2 * bq // SB_SUB))
            return c

        lax.fori_loop(0, n_wide, body, 0)
        sb_finish()
        fox_finish()
        return outer

    lax.fori_loop(0, seq // bq, q_block, 0)


def _sb_tail_matrix():
    sub = SB_SUB
    j = np.arange(2 * sub)[:, None] % sub
    s = np.arange(2 * sub)[None, :]
    return np.where(s < sub, j > s, True).astype(np.float32)


def _causal_attn(main3, aug3):
    b, s, _ = main3.shape
    tmat = jnp.asarray(_sb_tail_matrix(), BF16)
    state = pltpu.VMEM((N_PAIRS, 2 * CAUSAL_BQ, LANES), F32)
    out = jax.ShapeDtypeStruct((b, s, GROUP), BF16)
    return pl.pallas_call(
        functools.partial(_causal_kernel, seq=s),
        grid=(b,),
        in_specs=[_group_spec(s, 0), _group_spec(s, 1), _group_spec(s, 2), _resident(tmat.shape),
                  _group_spec(s, 6), _group_spec(s, 0), _group_spec(s, 7), _group_spec(s, 1), _group_spec(s, 8)],
        out_specs=[_group_spec(s, 0), _group_spec(s, 0)],
        out_shape=[out, out],
        scratch_shapes=[state, pltpu.VMEM((N_PAIRS, 2 * CAUSAL_BQ, SB_SUB), F32), pltpu.SMEM((1,), jnp.int32),
                        state, state, state],
        compiler_params=_params(1),
        name="causal_attn",
    )(main3, main3, main3, tmat, main3, aug3, main3, aug3, main3)


def _dil_kernel(*refs, seq):
    n_pat = len(DIL_PATTERNS)
    qkv = refs[:3 * n_pat]
    bias_ref = refs[3 * n_pat]
    o_ref = refs[3 * n_pat + 1]
    scratch = refs[3 * n_pat + 2:]
    n_state = n_pat * N_PAIRS
    u_refs, m_refs, l_refs = scratch[0:n_state], scratch[n_state:2 * n_state], scratch[2 * n_state:3 * n_state]
    blk = DIL_BLOCK
    lane = lax.broadcasted_iota(jnp.int32, (blk, LANES), 1)

    def block(p, pidx, dil, c, n, first):
        q_ref, k_ref, v_ref = qkv[3 * pidx:3 * pidx + 3]
        lanes = _pair_lanes(p)

        def rd(ref, start, size):
            return ref[pl.ds(start, size), lanes] if dil == 1 else ref[c, pl.ds(start, size), lanes]

        qstart = n * blk if isinstance(n, int) else pl.multiple_of(n * blk, blk)
        qs = _split_heads_rows(rd(q_ref, qstart, blk), lane) * SCALE
        h0, h1 = 2 * p, 2 * p + 1
        if first:
            keys, vals = rd(k_ref, 0, blk), rd(v_ref, 0, blk)
            bias = jnp.concatenate([bias_ref[pidx, h0, :, blk:], bias_ref[pidx, h1, :, blk:]], axis=0)
        else:
            kstart = (n - 1) * blk if isinstance(n, int) else pl.multiple_of((n - 1) * blk, blk)
            keys, vals = rd(k_ref, kstart, 2 * blk), rd(v_ref, kstart, 2 * blk)
            bias = jnp.concatenate([bias_ref[pidx, h0], bias_ref[pidx, h1]], axis=0)
        z = _dot_nt(qs, keys) + bias
        m = jnp.max(z, axis=1, keepdims=True)
        pr = jnp.exp(z - m)
        wide = lambda a: jnp.broadcast_to(a, (2 * blk, LANES))
        if (seq // dil) // blk == 1:
            den = wide(jnp.sum(pr, axis=1, keepdims=True))
            u = _dot(pr.astype(BF16), vals)
        else:
            ones = jnp.ones((vals.shape[0], LANES), BF16)
            u = _dot(pr.astype(BF16), jnp.concatenate([vals, ones], axis=1))
            u, den = u[:, :LANES], u[:, LANES:]
        idx = pl.ds(n * (blk * dil) + c, blk, stride=dil) if dil > 1 else pl.ds(qstart, blk)
        st = pidx * N_PAIRS + p
        u_refs[st][idx, :] = _merge_heads_rows(u, lane, blk)
        m_refs[st][idx, :] = _merge_heads_rows(wide(m), lane, blk)
        l_refs[st][idx, :] = _merge_heads_rows(den, lane, blk)

    def blocks(pidx, dil, c, n, first):
        for p in range(N_PAIRS):
            block(p, pidx, dil, c, n, first)

    for pidx, (_, dil) in enumerate(DIL_PATTERNS):
        n_blk = (seq // dil) // blk
        if dil == 1:
            blocks(pidx, dil, 0, 0, True)

            def body(g, carry, pidx=pidx, dil=dil):
                for jj in range(DIL_GEN_GROUP):
                    blocks(pidx, dil, 0, 1 + g * DIL_GEN_GROUP + jj, False)
                return carry
            lax.fori_loop(0, (n_blk - 1) // DIL_GEN_GROUP, body, 0)
        elif n_blk > 1:
            def body(c, carry, pidx=pidx, dil=dil, n_blk=n_blk):
                blocks(pidx, dil, c, 0, True)
                for n in range(1, n_blk):
                    blocks(pidx, dil, c, n, False)
                return carry
            lax.fori_loop(0, dil, body, 0)
        else:
            def body(g, carry, pidx=pidx, dil=dil):
                for jj in range(DIL_RES_GROUP):
                    blocks(pidx, dil, g * DIL_RES_GROUP + jj, 0, True)
                return carry
            lax.fori_loop(0, dil // DIL_RES_GROUP, body, 0)

    def comb(j, carry):
        sl = pl.ds(pl.multiple_of(j * COMB_ROWS, COMB_ROWS), COMB_ROWS)
        for p in range(N_PAIRS):
            sts = [pidx * N_PAIRS + p for pidx in range(n_pat)]
            ms = [m_refs[st][sl, :] for st in sts]
            m_all = functools.reduce(jnp.maximum, ms)
            es = [jnp.exp(mp - m_all) for mp in ms]
            num = sum(e * u_refs[st][sl, :] for e, st in zip(es, sts))
            den = sum(e * l_refs[st][sl, :] for e, st in zip(es, sts))
            o_ref[sl, _pair_lanes(p)] = (num / den).astype(BF16)
        return carry
    lax.fori_loop(0, seq // COMB_ROWS, comb, 0)


def _t5_bucket_of(dist):
    max_exact = REL_BUCKETS // 2
    nf = jnp.maximum(dist, 1).astype(jnp.float32)
    large = max_exact + (jnp.log(nf / max_exact) / math.log(REL_MAX_DIST / max_exact)
                         * (REL_BUCKETS - max_exact)).astype(jnp.int32)
    large = jnp.minimum(large, REL_BUCKETS - 1)
    return jnp.where(dist < max_exact, dist, large)


def _dil_bias_tables(rel_bias):
    blk = DIL_BLOCK
    qi = jnp.arange(blk)[:, None]
    kj = jnp.arange(2 * blk)[None, :]
    sub = qi + blk - kj
    tables = []
    for window, dil in DIL_PATTERNS:
        in_band = (sub >= 0) & (sub <= window // dil)
        bucket = _t5_bucket_of(jnp.maximum(sub, 0) * dil)
        hit = bucket[None, :, :, None] == jnp.arange(REL_BUCKETS)
        bias = jnp.sum(jnp.where(hit, rel_bias.T.astype(F32)[:, None, None, :], 0.0), axis=-1)
        tables.append(jnp.where(in_band[None], bias, NEG))
    return jnp.stack(tables)


def _dil_attn(main3, dl4, dl16, bias_tables):
    b, s, _ = main3.shape
    n_pat = len(DIL_PATTERNS)
    operands, in_specs = [], []
    for role in range(3):
        operands.append(main3)
        in_specs.append(_group_spec(s, DIL_GROUP0 // N_PAIRS + role))
    for arr in (dl4, dl16):
        dil, rows = arr.shape[1], arr.shape[2]
        for role in range(3):
            operands.append(arr)
            in_specs.append(pl.BlockSpec((None, dil, rows, GROUP), lambda bi, role=role: (bi, 0, 0, role)))
    operands.append(bias_tables)
    in_specs.append(_resident(bias_tables.shape))
    return pl.pallas_call(
        functools.partial(_dil_kernel, seq=s),
        grid=(b,),
        in_specs=in_specs,
        out_specs=_group_spec(s, 0),
        out_shape=jax.ShapeDtypeStruct((b, s, GROUP), BF16),
        scratch_shapes=[pltpu.VMEM((s, LANES), F32)] * (3 * n_pat * N_PAIRS),
        compiler_params=_params(1),
        name="dil_attn",
    )(*operands)


def _layer_norm(v, g, b):
    mu = jnp.mean(v, axis=-1, keepdims=True)
    d = v - mu
    var = jnp.mean(d * d, axis=-1, keepdims=True)
    return d * lax.rsqrt(var + LN_EPS) * g + b


def _dense_kernel(x_ref, oa_ref, ob_ref, oc_ref, od_ref, wo_ref, g1_ref, b1_ref,
                  wg_ref, wu_ref, wd_ref, g2_ref, b2_ref, out_ref, h_ref, *, alpha):
    mix = (_dot(oa_ref[...], wo_ref[0:GROUP, :]) + _dot(ob_ref[...], wo_ref[GROUP:2 * GROUP, :])
           + _dot(oc_ref[...], wo_ref[2 * GROUP:3 * GROUP, :]) + _dot(od_ref[...], wo_ref[3 * GROUP:4 * GROUP, :]))
    x1 = _layer_norm(alpha * x_ref[...] + mix, g1_ref[...], b1_ref[...])
    xb = x1.astype(BF16)
    d_ff = wg_ref.shape[1]
    for c0 in range(0, d_ff, FFN_CHUNK):
        g = _dot(xb, wg_ref[:, c0:c0 + FFN_CHUNK])
        u = _dot(xb, wu_ref[:, c0:c0 + FFN_CHUNK])
        h_ref[:, c0:c0 + FFN_CHUNK] = (g * (1.0 / (1.0 + jnp.exp(-g))) * u).astype(BF16)
    y = _dot(h_ref[...], wd_ref[...])
    out_ref[...] = _layer_norm(alpha * x1 + y, g2_ref[...], b2_ref[...])


def _dense(x2d, outs, w_out, g1, b1, w_gate, w_up, w_down, g2, b2, alpha):
    m, d = x2d.shape
    tm = DENSE_TM
    d_ff = w_gate.shape[1]
    row = lambda i: (i, 0)
    vec = lambda a: a[None, :]
    o_spec = pl.BlockSpec((tm, GROUP), row)
    return pl.pallas_call(
        functools.partial(_dense_kernel, alpha=alpha),
        grid=(m // tm,),
        in_specs=[pl.BlockSpec((tm, d), row), o_spec, o_spec, o_spec, o_spec,
                  _resident((d, d)), _resident((1, d)), _resident((1, d)),
                  _resident((d, d_ff)), _resident((d, d_ff)), _resident((d_ff, d)),
                  _resident((1, d)), _resident((1, d))],
        out_specs=pl.BlockSpec((tm, d), row),
        out_shape=jax.ShapeDtypeStruct((m, d), F32),
        scratch_shapes=[pltpu.VMEM((tm, d_ff), BF16)],
        compiler_params=_params(1),
        name="dense",
    )(x2d, *outs, w_out.astype(BF16), vec(g1), vec(b1), w_gate.astype(BF16), w_up.astype(BF16),
      w_down.astype(BF16), vec(g2), vec(b2))


def kernel(x, w_in, f_bias, conv_w, w_out, rel_bias, ln1_g, ln1_b, w_gate, w_up, w_down, ln2_g, ln2_b):
    b, s, d = x.shape
    depth = w_in.shape[0]
    assert d == 4 * GROUP and w_in.shape[2] == 12 * GROUP + N_HEADS
    assert s % PROJ_TM == 0 and s % (2 * CAUSAL_BQ) == 0 and s % COMB_ROWS == 0
    assert [dil for _, dil in DIL_PATTERNS][0] == 1
    for _, dil in DIL_PATTERNS:
        n_blk = (s // dil) // DIL_BLOCK
        assert n_blk * DIL_BLOCK * dil == s and PROJ_TM % (16 * dil) == 0
        assert (n_blk - 1) % DIL_GEN_GROUP == 0 if dil == 1 else (n_blk > 1 or dil % DIL_RES_GROUP == 0)
    alpha = (2 * depth) ** 0.25
    bias_tables = _dil_bias_tables(rel_bias)
    w_pad = jnp.pad(jnp.swapaxes(w_in, 1, 2), ((0, 0), (0, LANES - N_HEADS), (0, 0))).astype(BF16)
    x2d = x.reshape(b * s, d)
    for layer in range(depth):
        main, dl4, dl16, aug, out_d = _in_proj(x2d, w_pad[layer], f_bias[layer], conv_w[layer], b, s)
        main3 = main.reshape(b, s, main.shape[1])
        out_a, out_c = _causal_attn(main3, aug.reshape(b, s, aug.shape[1]))
        out_b = _dil_attn(main3, dl4, dl16, bias_tables)
        outs = [o.reshape(b * s, GROUP) for o in (out_a, out_b, out_c)] + [out_d]
        x2d = _dense(x2d, outs, w_out[layer], ln1_g[layer], ln1_b[layer], w_gate[layer], w_up[layer],
                     w_down[layer], ln2_g[layer], ln2_b[layer], alpha)
    return x2d.reshape(b, s, d)
```

```python
import functools
import math

import jax
import jax.numpy as jnp
import numpy as np
from jax import lax
from jax.experimental import pallas as pl
from jax.experimental.pallas import tpu as pltpu

HEAD_DIM = 64
N_HEADS = 4
GROUP = N_HEADS * HEAD_DIM
LANES = 128
N_PAIRS = GROUP // LANES
CONV_K = 3
DIL_PATTERNS = ((128, 1), (512, 4), (2048, 16))
DIL_BLOCK = 128
DIL_GROUP0 = 6
REL_BUCKETS = 32
REL_MAX_DIST = 2048
LN_EPS = 1e-5
SCALE = HEAD_DIM ** -0.5
NEG = -1e30
SIGN_BIT = np.uint32(0x80000000)
VMEM_LIMIT = 56 * 1024 * 1024

BF16 = jnp.bfloat16
F32 = jnp.float32

PROJ_TM = 512
DENSE_TM = 512
FFN_CHUNK = 256
CAUSAL_BQ = 256
SB_SUB = 128
SB_DEAD = 120.0
FOX_Q_COL = 6 * GROUP
FOX_K_COL = 7 * GROUP
FOX_DEAD = 120.0
FOX_NORM_SLACK = 1.02
DIL_GEN_GROUP = 3
DIL_RES_GROUP = 4
COMB_ROWS = 256


def _dot(a, b):
    return jnp.dot(a, b, preferred_element_type=F32)


def _dot_nt(a, b):
    return lax.dot_general(a, b, (((1,), (1,)), ((), ())), preferred_element_type=F32)


def _resident(shape):
    nd = len(shape)
    return pl.BlockSpec(shape, lambda *_: (0,) * nd, pipeline_mode=pl.Buffered(1))


def _params(n_axes):
    return pltpu.CompilerParams(dimension_semantics=("arbitrary",) * n_axes,
                                vmem_limit_bytes=VMEM_LIMIT)


def _split_heads_rows(q, lane):
    zero = jnp.zeros_like(q)
    return jnp.concatenate([jnp.where(lane < HEAD_DIM, q, zero),
                            jnp.where(lane >= HEAD_DIM, q, zero)], axis=0)


def _merge_heads_rows(a, lane, rows):
    return jnp.where(lane < HEAD_DIM, a[:rows], a[rows:])


def _pair_lanes(p):
    return slice(p * LANES, (p + 1) * LANES)


def _group_spec(seq, g):
    return pl.BlockSpec((None, seq, GROUP), lambda bi: (bi, 0, g))


def _in_proj_kernel(x_ref, w_ref, fb_ref, cw_ref, sel_ref,
                    main_ref, dl4_ref, dl16_ref, aug_ref, od_ref, side_ref,
                    carry_ref, ubuf_ref, utail_ref, *stage_refs,
                    tiles_per_seq):
    tm = x_ref.shape[0]
    first = (pl.program_id(0) % tiles_per_seq) == 0
    xb = x_ref[...].astype(BF16)

    n_main = main_ref.shape[1]
    cv0 = n_main
    gate0 = n_main + 3 * GROUP
    dl_col0 = DIL_GROUP0 * LANES
    row_sq = {}
    for c0 in range(0, n_main, GROUP):
        r = _dot_nt(xb, w_ref[c0:c0 + GROUP, :])
        main_ref[:, c0:c0 + GROUP] = r.astype(BF16)
        if c0 in (FOX_Q_COL, FOX_K_COL):
            row_sq[c0] = jnp.sum(r * r, axis=1, keepdims=True)
        if dl_col0 <= c0 < dl_col0 + 3 * GROUP:
            j = (c0 - dl_col0) // LANES
            stage_refs[j][...] = r[:, :LANES]
            stage_refs[j + 1][...] = r[:, LANES:]

    for dil, ref in ((DIL_PATTERNS[1][1], dl4_ref), (DIL_PATTERNS[2][1], dl16_ref)):
        rows = tm // dil
        for c in range(dil):
            for j in range(len(stage_refs)):
                ref[c, :, j * LANES:(j + 1) * LANES] = (
                    stage_refs[j][pl.ds(c, rows, stride=dil), :].astype(BF16))

    g = _dot_nt(xb, w_ref[gate0:gate0 + LANES, :]) + fb_ref[...]
    logf = jnp.minimum(g, 0.0) - jnp.log(1.0 + jnp.exp(-jnp.abs(g)))
    r_i = lax.broadcasted_iota(jnp.int32, (tm, tm), 0)
    c_i = lax.broadcasted_iota(jnp.int32, (tm, tm), 1)
    tri = jnp.where(c_i <= r_i, 1.0, 0.0).astype(BF16)
    l_hi = logf.astype(BF16)
    l_lo = (logf - l_hi.astype(F32)).astype(BF16)
    csum = _dot(tri, l_hi) + _dot(tri, l_lo)
    prev = jnp.where(first, 0.0, carry_ref[...])
    csum = csum + prev
    carry_ref[...] = csum[tm - 1:tm, :]
    for b in range(tm // CAUSAL_BQ):
        rows = slice(b * CAUSAL_BQ, (b + 1) * CAUSAL_BQ)
        for j, c0 in enumerate((FOX_Q_COL, FOX_K_COL)):
            side_ref[b, j:j + 1, :] = jnp.broadcast_to(jnp.max(row_sq[c0][rows], axis=0, keepdims=True), (1, LANES))
        side_ref[b, 2:3, :] = csum[b * CAUSAL_BQ:b * CAUSAL_BQ + 1, :]
        side_ref[b, 3:4, :] = csum[(b + 1) * CAUSAL_BQ - 1:(b + 1) * CAUSAL_BQ, :]
        side_ref[b, 4:8, :] = jnp.zeros((4, LANES), F32)
    c_hi = csum.astype(BF16)
    c_r = csum - c_hi.astype(F32)
    c_mid = c_r.astype(BF16)
    c_lo = (c_r - c_mid.astype(F32)).astype(BF16)
    ones = jnp.ones((tm, LANES), BF16)
    aug_ref[...] = _dot(jnp.concatenate([c_hi, c_mid, c_lo, ones], axis=1), sel_ref[...]).astype(BF16)

    cvb = _dot_nt(xb, w_ref[cv0:cv0 + GROUP, :])
    cvc = _dot_nt(xb, w_ref[cv0 + GROUP:cv0 + 2 * GROUP, :])
    cvh = _dot_nt(xb, w_ref[cv0 + 2 * GROUP:cv0 + 3 * GROUP, :])
    u = cvc * cvh

    ubuf_ref[0:8, :] = jnp.where(first, 0.0, utail_ref[...])
    ubuf_ref[8:8 + tm, :] = u
    y = (cw_ref[0:1, :] * ubuf_ref[6:6 + tm, :] + cw_ref[1:2, :] * ubuf_ref[7:7 + tm, :]
         + cw_ref[2:3, :] * u)
    od_ref[...] = (cvb * y).astype(BF16)
    utail_ref[...] = u[tm - 8:tm, :]


def _aug_selection():
    sel = np.zeros((4 * LANES, 4 * LANES), np.float32)
    ones_row = 3 * LANES
    for h in range(N_HEADS):
        bq = (h // 2) * LANES + (h % 2) * HEAD_DIM
        bk = 2 * LANES + bq
        for part in range(3):
            sel[part * LANES + h, bq + part] = 1.0
            sel[ones_row, bq + 3 + part] = 1.0
            sel[ones_row, bk + part] = 1.0
            sel[part * LANES + h, bk + 3 + part] = -1.0
    return sel


def _in_proj(x2d, w_pad, f_bias, conv_w, batch, seq):
    m, d = x2d.shape
    tm = PROJ_TM
    tps = seq // tm
    n_main = 9 * GROUP
    d4, d16 = DIL_PATTERNS[1][1], DIL_PATTERNS[2][1]
    fb = jnp.pad(f_bias, (0, LANES - N_HEADS))[None, :]
    cw = jnp.pad(conv_w, ((0, 8 - CONV_K), (0, 0)))
    sel = jnp.asarray(_aug_selection(), BF16)
    row = lambda i: (i, 0)
    res = lambda i: (i // tps, 0, i % tps, 0)
    return pl.pallas_call(
        functools.partial(_in_proj_kernel, tiles_per_seq=tps),
        grid=(m // tm,),
        in_specs=[pl.BlockSpec((tm, d), row), _resident(w_pad.shape),
                  _resident(fb.shape), _resident(cw.shape), _resident(sel.shape)],
        out_specs=[pl.BlockSpec((tm, n_main), row),
                   pl.BlockSpec((None, d4, tm // d4, 3 * GROUP), res),
                   pl.BlockSpec((None, d16, tm // d16, 3 * GROUP), res),
                   pl.BlockSpec((tm, 4 * LANES), row),
                   pl.BlockSpec((tm, GROUP), row),
                   pl.BlockSpec((tm // CAUSAL_BQ, 8, LANES), lambda i: (i, 0, 0))],
        out_shape=[jax.ShapeDtypeStruct((m, n_main), BF16),
                   jax.ShapeDtypeStruct((batch, d4, seq // d4, 3 * GROUP), BF16),
                   jax.ShapeDtypeStruct((batch, d16, seq // d16, 3 * GROUP), BF16),
                   jax.ShapeDtypeStruct((m, 4 * LANES), BF16),
                   jax.ShapeDtypeStruct((m, GROUP), BF16),
                   jax.ShapeDtypeStruct((m // CAUSAL_BQ, 8, LANES), F32)],
        scratch_shapes=[pltpu.VMEM((1, LANES), F32), pltpu.VMEM((tm + 8, GROUP), F32),
                        pltpu.VMEM((8, GROUP), F32)]
        + [pltpu.VMEM((tm, LANES), F32)] * (3 * N_PAIRS),
        compiler_params=_params(1),
        name="in_proj",
    )(x2d, w_pad, fb, cw, sel)


def _sb_ops(i, qstart, q_ref, k_ref, v_ref, t_ref, o_ref, acc_ref, carry_ref, live_ref):
    bq, sub = CAUSAL_BQ, SB_SUB
    lane_q = lax.broadcasted_iota(jnp.int32, (bq, LANES), 1)
    qss = []
    for p in range(N_PAIRS):
        qss.append(_split_heads_rows(q_ref[pl.ds(qstart, bq), _pair_lanes(p)], lane_q) * SCALE)
        acc_ref[p] = jnp.zeros((2 * bq, LANES), F32)
        carry_ref[p] = jnp.zeros((2 * bq, sub), F32)

    def scores(p, start, g):
        return _dot_nt(qss[p], k_ref[pl.ds(start, g * sub), _pair_lanes(p)])

    def group(p, start, g, masked, z):
        n = g * sub
        neg_abs = pltpu.bitcast(pltpu.bitcast(z, jnp.uint32) | SIGN_BIT, F32)
        sp = jnp.log(1.0 + jnp.exp(neg_abs))
        log_beta = jnp.minimum(z, 0.0) - sp
        log_rest = log_beta - z
        if masked:
            row = lax.broadcasted_iota(jnp.int32, (2 * bq, n), 0)
            col = lax.broadcasted_iota(jnp.int32, (2 * bq, n), 1)
            strict = (start + col) < (qstart + jnp.where(row >= bq, row - bq, row))
            log_rest = jnp.where(strict, log_rest, 0.0)
        hi = log_rest.astype(BF16)
        lo = (log_rest - hi.astype(F32)).astype(BF16)
        carry = carry_ref[p]
        parts = [None] * g
        for j in reversed(range(g)):
            sl = slice(j * sub, (j + 1) * sub)
            cs = _dot(jnp.concatenate([hi[:, sl], lo[:, sl]], axis=1), t_ref[...])
            a = jnp.exp(log_beta[:, sl] + (cs[:, :sub] + carry))
            carry = carry + cs[:, sub:]
            if masked:
                a = jnp.where(strict[:, sl], a, 0.0)
            parts[j] = a.astype(BF16)
        carry_ref[p] = carry
        acc_ref[p] += _dot(jnp.concatenate(parts, axis=1), v_ref[pl.ds(start, n), _pair_lanes(p)])

    def groups(start, g, masked, zs=None):
        for p in range(N_PAIRS):
            group(p, start, g, masked, scores(p, start, g) if zs is None else zs[p])
        if not masked:
            top = functools.reduce(jnp.maximum, [jnp.max(carry_ref[p]) for p in range(N_PAIRS)])
            live_ref[0] = jnp.where(top >= -SB_DEAD, 1, 0)

    def all_scores(start, g):
        return [scores(p, start, g) for p in range(N_PAIRS)]

    def finish():
        for p in range(N_PAIRS):
            o_ref[pl.ds(qstart, bq), _pair_lanes(p)] = _merge_heads_rows(acc_ref[p], lane_q, bq).astype(BF16)

    return groups, all_scores, finish


def _fox_ops(i, qstart, q_ref, aq_ref, k_ref, ak_ref, v_ref, side_ref, o_ref, m_ref, l_ref, acc_ref, live_ref):
    bq = CAUSAL_BQ
    lane_q = lax.broadcasted_iota(jnp.int32, (bq, LANES), 1)

    def wide(a):
        return jnp.broadcast_to(a, (2 * bq, LANES))

    qss = [jnp.concatenate(
        [_split_heads_rows(q_ref[pl.ds(qstart, bq), _pair_lanes(p)], lane_q) * SCALE,
         _split_heads_rows(aq_ref[pl.ds(qstart, bq), _pair_lanes(p)], lane_q)], axis=1)
        for p in range(N_PAIRS)]

    def scores(p, start, n):
        rows = pl.ds(start, n)
        kk = jnp.concatenate([k_ref[rows, _pair_lanes(p)], ak_ref[rows, _pair_lanes(p)]], axis=1)
        return _dot_nt(qss[p], kk)

    def values(p, start, n):
        return v_ref[pl.ds(start, n), _pair_lanes(p)]

    def diag(zs=None):
        row = lax.broadcasted_iota(jnp.int32, (2 * bq, bq), 0)
        col = lax.broadcasted_iota(jnp.int32, (2 * bq, bq), 1)
        causal = col <= jnp.where(row >= bq, row - bq, row)
        for p in range(N_PAIRS):
            z = scores(p, qstart, bq) if zs is None else zs[p]
            vs = values(p, qstart, bq)
            z = jnp.where(causal, z, NEG)
            m0 = jnp.max(z, axis=1, keepdims=True)
            pr = jnp.exp(z - m0)
            m_ref[p] = wide(m0)
            l_ref[p] = wide(jnp.sum(pr, axis=1, keepdims=True))
            acc_ref[p] = _dot(pr.astype(BF16), vs)

    def step(start, n, zs=None):
        for p in range(N_PAIRS):
            z = scores(p, start, n) if zs is None else zs[p]
            vs = values(p, start, n)
            m_old = m_ref[p]
            m_new = jnp.maximum(m_old, jnp.max(z, axis=1, keepdims=True))
            alpha = jnp.exp(m_old - m_new)
            pr = jnp.exp(z - jnp.concatenate([m_new] * (n // LANES), axis=1))
            l_ref[p] = alpha * l_ref[p] + jnp.sum(pr, axis=1, keepdims=True)
            acc_ref[p] = alpha * acc_ref[p] + _dot(pr.astype(BF16), vs)
            m_ref[p] = m_new

    def check(end_blk, k_norm_sq):
        q_side = side_ref[i]
        k_side = side_ref[end_blk]
        qk = (0.5 * SCALE * FOX_NORM_SLACK) * (q_side[0:1, :] + k_norm_sq)
        lane = lax.broadcasted_iota(jnp.int32, (1, LANES), 1)
        top = jnp.max(jnp.where(lane < N_HEADS, qk + (q_side[2:3, :] - k_side[3:4, :]), NEG))
        m_min = functools.reduce(jnp.minimum, [jnp.min(m_ref[p]) for p in range(N_PAIRS)])
        live_ref[0] = jnp.where(top - m_min >= -FOX_DEAD, 1, 0)

    def finish():
        for p in range(N_PAIRS):
            o_ref[pl.ds(qstart, bq), _pair_lanes(p)] = _merge_heads_rows(
                acc_ref[p] / l_ref[p], lane_q, bq).astype(BF16)

    return diag, step, check, finish


def _causal_kernel(sq_ref, sk_ref, sv_ref, t_ref, fq_ref, faq_ref, fk_ref, fak_ref, fv_ref, side_ref,
                   so_ref, fo_ref, s_acc, s_carry, s_live, f_m, f_l, f_acc, f_live, *, seq):
    bq = CAUSAL_BQ
    n_q = seq // bq
    k_norm_sq = functools.reduce(jnp.maximum, [side_ref[b, 1:2, :] for b in range(n_q)])

    def q_block(i, outer):
        qstart = pl.multiple_of(i * bq, bq)
        sb_groups, sb_scores, sb_finish = _sb_ops(i, qstart, sq_ref, sk_ref, sv_ref, t_ref, so_ref,
                                                  s_acc, s_carry, s_live)
        fox_diag, fox_step, fox_check, fox_finish = _fox_ops(i, qstart, fq_ref, faq_ref, fk_ref, fak_ref,
                                                             fv_ref, side_ref, fo_ref, f_m, f_l, f_acc, f_live)
        sb_groups(qstart, bq // SB_SUB, True)
        fox_diag()
        s_live[0] = 1
        f_live[0] = 1

        @pl.when((i & 1) == 1)
        def _():
            prev = pl.multiple_of(qstart - bq, bq)
            sb_groups(prev, bq // SB_SUB, False)
            fox_step(prev, bq)
            fox_check(jnp.maximum(i - 2, 0), k_norm_sq)

        n_wide = lax.shift_right_logical(i, 1)

        def cond(state):
            t, sb_on, fox_on = state
            return (t < n_wide) & ((sb_on == 1) | (fox_on == 1))

        def body(state):
            t = state[0]
            first_blk = 2 * (n_wide - 1 - t)
            start = pl.multiple_of(first_blk * bq, 2 * bq)

            @pl.when(f_live[0] == 1)
            def _():
                fox_step(start, 2 * bq)
                fox_check(jnp.maximum(first_blk - 1, 0), k_norm_sq)

            @pl.when(s_live[0] == 1)
            def _():
                sb_groups(start, 2 * bq // SB_SUB, False, sb_scores(start, 2 * bq // SB_SUB))
            return t + 1, s_live[0], f_live[0]

        lax.while_loop(cond, body, (jnp.int32(0), s_live[0], f_live[0]))
        sb_finish()
        fox_finish()
        return outer

    lax.fori_loop(0, n_q, q_block, 0)


def _sb_tail_matrix():
    sub = SB_SUB
    j = np.arange(2 * sub)[:, None] % sub
    s = np.arange(2 * sub)[None, :]
    return np.where(s < sub, j > s, True).astype(np.float32)


def _causal_attn(main3, aug3, side4):
    b, s, _ = main3.shape
    tmat = jnp.asarray(_sb_tail_matrix(), BF16)
    state = pltpu.VMEM((N_PAIRS, 2 * CAUSAL_BQ, LANES), F32)
    flag = pltpu.SMEM((1,), jnp.int32)
    out = jax.ShapeDtypeStruct((b, s, GROUP), BF16)
    return pl.pallas_call(
        functools.partial(_causal_kernel, seq=s),
        grid=(b,),
        in_specs=[_group_spec(s, 0), _group_spec(s, 1), _group_spec(s, 2), _resident(tmat.shape),
                  _group_spec(s, 6), _group_spec(s, 0), _group_spec(s, 7), _group_spec(s, 1), _group_spec(s, 8),
                  pl.BlockSpec((None,) + side4.shape[1:], lambda bi: (bi, 0, 0, 0))],
        out_specs=[_group_spec(s, 0), _group_spec(s, 0)],
        out_shape=[out, out],
        scratch_shapes=[state, pltpu.VMEM((N_PAIRS, 2 * CAUSAL_BQ, SB_SUB), F32), flag,
                        state, state, state, flag],
        compiler_params=_params(1),
        name="causal_attn",
    )(main3, main3, main3, tmat, main3, aug3, main3, aug3, main3, side4)


def _dil_kernel(*refs, seq):
    n_pat = len(DIL_PATTERNS)
    qkv = refs[:3 * n_pat]
    bias_ref = refs[3 * n_pat]
    o_ref = refs[3 * n_pat + 1]
    scratch = refs[3 * n_pat + 2:]
    n_state = n_pat * N_PAIRS
    u_refs, m_refs, l_refs = scratch[0:n_state], scratch[n_state:2 * n_state], scratch[2 * n_state:3 * n_state]
    blk = DIL_BLOCK
    lane = lax.broadcasted_iota(jnp.int32, (blk, LANES), 1)

    def block(p, pidx, dil, c, n, first):
        q_ref, k_ref, v_ref = qkv[3 * pidx:3 * pidx + 3]
        lanes = _pair_lanes(p)

        def rd(ref, start, size):
            return ref[pl.ds(start, size), lanes] if dil == 1 else ref[c, pl.ds(start, size), lanes]

        qstart = n * blk if isinstance(n, int) else pl.multiple_of(n * blk, blk)
        qs = _split_heads_rows(rd(q_ref, qstart, blk), lane) * SCALE
        h0, h1 = 2 * p, 2 * p + 1
        if first:
            keys, vals = rd(k_ref, 0, blk), rd(v_ref, 0, blk)
            bias = jnp.concatenate([bias_ref[pidx, h0, :, blk:], bias_ref[pidx, h1, :, blk:]], axis=0)
        else:
            kstart = (n - 1) * blk if isinstance(n, int) else pl.multiple_of((n - 1) * blk, blk)
            keys, vals = rd(k_ref, kstart, 2 * blk), rd(v_ref, kstart, 2 * blk)
            bias = jnp.concatenate([bias_ref[pidx, h0], bias_ref[pidx, h1]], axis=0)
        z = _dot_nt(qs, keys) + bias
        m = jnp.max(z, axis=1, keepdims=True)
        pr = jnp.exp(z - m)
        wide = lambda a: jnp.broadcast_to(a, (2 * blk, LANES))
        if (seq // dil) // blk == 1:
            den = wide(jnp.sum(pr, axis=1, keepdims=True))
            u = _dot(pr.astype(BF16), vals)
        else:
            ones = jnp.ones((vals.shape[0], LANES), BF16)
            u = _dot(pr.astype(BF16), jnp.concatenate([vals, ones], axis=1))
            u, den = u[:, :LANES], u[:, LANES:]
        idx = pl.ds(n * (blk * dil) + c, blk, stride=dil) if dil > 1 else pl.ds(qstart, blk)
        st = pidx * N_PAIRS + p
        u_refs[st][idx, :] = _merge_heads_rows(u, lane, blk)
        m_refs[st][idx, :] = _merge_heads_rows(wide(m), lane, blk)
        l_refs[st][idx, :] = _merge_heads_rows(den, lane, blk)

    def blocks(pidx, dil, c, n, first):
        for p in range(N_PAIRS):
            block(p, pidx, dil, c, n, first)

    for pidx, (_, dil) in enumerate(DIL_PATTERNS):
        n_blk = (seq // dil) // blk
        if dil == 1:
            blocks(pidx, dil, 0, 0, True)

            def body(g, carry, pidx=pidx, dil=dil):
                for jj in range(DIL_GEN_GROUP):
                    blocks(pidx, dil, 0, 1 + g * DIL_GEN_GROUP + jj, False)
                return carry
            lax.fori_loop(0, (n_blk - 1) // DIL_GEN_GROUP, body, 0)
        elif n_blk > 1:
            def body(c, carry, pidx=pidx, dil=dil, n_blk=n_blk):
                blocks(pidx, dil, c, 0, True)
                for n in range(1, n_blk):
                    blocks(pidx, dil, c, n, False)
                return carry
            lax.fori_loop(0, dil, body, 0)
        else:
            def body(g, carry, pidx=pidx, dil=dil):
                for jj in range(DIL_RES_GROUP):
                    blocks(pidx, dil, g * DIL_RES_GROUP + jj, 0, True)
                return carry
            lax.fori_loop(0, dil // DIL_RES_GROUP, body, 0)

    def comb(j, carry):
        sl = pl.ds(pl.multiple_of(j * COMB_ROWS, COMB_ROWS), COMB_ROWS)
        for p in range(N_PAIRS):
            sts = [pidx * N_PAIRS + p for pidx in range(n_pat)]
            ms = [m_refs[st][sl, :] for st in sts]
            m_all = functools.reduce(jnp.maximum, ms)
            es = [jnp.exp(mp - m_all) for mp in ms]
            num = sum(e * u_refs[st][sl, :] for e, st in zip(es, sts))
            den = sum(e * l_refs[st][sl, :] for e, st in zip(es, sts))
            o_ref[sl, _pair_lanes(p)] = (num / den).astype(BF16)
        return carry
    lax.fori_loop(0, seq // COMB_ROWS, comb, 0)


def _t5_bucket_of(dist):
    max_exact = REL_BUCKETS // 2
    nf = jnp.maximum(dist, 1).astype(jnp.float32)
    large = max_exact + (jnp.log(nf / max_exact) / math.log(REL_MAX_DIST / max_exact)
                         * (REL_BUCKETS - max_exact)).astype(jnp.int32)
    large = jnp.minimum(large, REL_BUCKETS - 1)
    return jnp.where(dist < max_exact, dist, large)


def _dil_bias_tables(rel_bias):
    blk = DIL_BLOCK
    qi = jnp.arange(blk)[:, None]
    kj = jnp.arange(2 * blk)[None, :]
    sub = qi + blk - kj
    tables = []
    for window, dil in DIL_PATTERNS:
        in_band = (sub >= 0) & (sub <= window // dil)
        bucket = _t5_bucket_of(jnp.maximum(sub, 0) * dil)
        hit = bucket[None, :, :, None] == jnp.arange(REL_BUCKETS)
        bias = jnp.sum(jnp.where(hit, rel_bias.T.astype(F32)[:, None, None, :], 0.0), axis=-1)
        tables.append(jnp.where(in_band[None], bias, NEG))
    return jnp.stack(tables)


def _dil_attn(main3, dl4, dl16, bias_tables):
    b, s, _ = main3.shape
    n_pat = len(DIL_PATTERNS)
    operands, in_specs = [], []
    for role in range(3):
        operands.append(main3)
        in_specs.append(_group_spec(s, DIL_GROUP0 // N_PAIRS + role))
    for arr in (dl4, dl16):
        dil, rows = arr.shape[1], arr.shape[2]
        for role in range(3):
            operands.append(arr)
            in_specs.append(pl.BlockSpec((None, dil, rows, GROUP), lambda bi, role=role: (bi, 0, 0, role)))
    operands.append(bias_tables)
    in_specs.append(_resident(bias_tables.shape))
    return pl.pallas_call(
        functools.partial(_dil_kernel, seq=s),
        grid=(b,),
        in_specs=in_specs,
        out_specs=_group_spec(s, 0),
        out_shape=jax.ShapeDtypeStruct((b, s, GROUP), BF16),
        scratch_shapes=[pltpu.VMEM((s, LANES), F32)] * (3 * n_pat * N_PAIRS),
        compiler_params=_params(1),
        name="dil_attn",
    )(*operands)


def _layer_norm(v, g, b):
    mu = jnp.mean(v, axis=-1, keepdims=True)
    d = v - mu
    var = jnp.mean(d * d, axis=-1, keepdims=True)
    return d * lax.rsqrt(var + LN_EPS) * g + b


def _dense_kernel(x_ref, oa_ref, ob_ref, oc_ref, od_ref, wo_ref, g1_ref, b1_ref,
                  wg_ref, wu_ref, wd_ref, g2_ref, b2_ref, out_ref, h_ref, *, alpha):
    mix = (_dot(oa_ref[...], wo_ref[0:GROUP, :]) + _dot(ob_ref[...], wo_ref[GROUP:2 * GROUP, :])
           + _dot(oc_ref[...], wo_ref[2 * GROUP:3 * GROUP, :]) + _dot(od_ref[...], wo_ref[3 * GROUP:4 * GROUP, :]))
    x1 = _layer_norm(alpha * x_ref[...] + mix, g1_ref[...], b1_ref[...])
    xb = x1.astype(BF16)
    d_ff = wg_ref.shape[1]
    for c0 in range(0, d_ff, FFN_CHUNK):
        g = _dot(xb, wg_ref[:, c0:c0 + FFN_CHUNK])
        u = _dot(xb, wu_ref[:, c0:c0 + FFN_CHUNK])
        h_ref[:, c0:c0 + FFN_CHUNK] = (g * (1.0 / (1.0 + jnp.exp(-g))) * u).astype(BF16)
    y = _dot(h_ref[...], wd_ref[...])
    out_ref[...] = _layer_norm(alpha * x1 + y, g2_ref[...], b2_ref[...])


def _dense(x2d, outs, w_out, g1, b1, w_gate, w_up, w_down, g2, b2, alpha):
    m, d = x2d.shape
    tm = DENSE_TM
    d_ff = w_gate.shape[1]
    row = lambda i: (i, 0)
    vec = lambda a: a[None, :]
    o_spec = pl.BlockSpec((tm, GROUP), row)
    return pl.pallas_call(
        functools.partial(_dense_kernel, alpha=alpha),
        grid=(m // tm,),
        in_specs=[pl.BlockSpec((tm, d), row), o_spec, o_spec, o_spec, o_spec,
                  _resident((d, d)), _resident((1, d)), _resident((1, d)),
                  _resident((d, d_ff)), _resident((d, d_ff)), _resident((d_ff, d)),
                  _resident((1, d)), _resident((1, d))],
        out_specs=pl.BlockSpec((tm, d), row),
        out_shape=jax.ShapeDtypeStruct((m, d), F32),
        scratch_shapes=[pltpu.VMEM((tm, d_ff), BF16)],
        compiler_params=_params(1),
        name="dense",
    )(x2d, *outs, w_out.astype(BF16), vec(g1), vec(b1), w_gate.astype(BF16), w_up.astype(BF16),
      w_down.astype(BF16), vec(g2), vec(b2))


def kernel(x, w_in, f_bias, conv_w, w_out, rel_bias, ln1_g, ln1_b, w_gate, w_up, w_down, ln2_g, ln2_b):
    b, s, d = x.shape
    depth = w_in.shape[0]
    assert d == 4 * GROUP and w_in.shape[2] == 12 * GROUP + N_HEADS
    assert s % PROJ_TM == 0 and s % (2 * CAUSAL_BQ) == 0 and s % COMB_ROWS == 0
    assert [dil for _, dil in DIL_PATTERNS][0] == 1
    for _, dil in DIL_PATTERNS:
        n_blk = (s // dil) // DIL_BLOCK
        assert n_blk * DIL_BLOCK * dil == s and PROJ_TM % (16 * dil) == 0
        assert (n_blk - 1) % DIL_GEN_GROUP == 0 if dil == 1 else (n_blk > 1 or dil % DIL_RES_GROUP == 0)
    alpha = (2 * depth) ** 0.25
    bias_tables = _dil_bias_tables(rel_bias)
    w_pad = jnp.pad(jnp.swapaxes(w_in, 1, 2), ((0, 0), (0, LANES - N_HEADS), (0, 0))).astype(BF16)
    x2d = x.reshape(b * s, d)
    for layer in range(depth):
        main, dl4, dl16, aug, out_d, side = _in_proj(x2d, w_pad[layer], f_bias[layer], conv_w[layer], b, s)
        main3 = main.reshape(b, s, main.shape[1])
        out_a, out_c = _causal_attn(main3, aug.reshape(b, s, aug.shape[1]),
                                    side.reshape((b, s // CAUSAL_BQ) + side.shape[1:]))
        out_b = _dil_attn(main3, dl4, dl16, bias_tables)
        outs = [o.reshape(b * s, GROUP) for o in (out_a, out_b, out_c)] + [out_d]
        x2d = _dense(x2d, outs, w_out[layer], ln1_g[layer], ln1_b[layer], w_gate[layer], w_up[layer],
                     w_down[layer], ln2_g[layer], ln2_b[layer], alpha)
    return x2d.reshape(b, s, d)
```

```python
import functools
import math

import jax
import jax.numpy as jnp
import numpy as np
from jax import lax
from jax.experimental import pallas as pl
from jax.experimental.pallas import tpu as pltpu

HEAD_DIM = 64
N_HEADS = 4
GROUP = N_HEADS * HEAD_DIM
LANES = 128
N_PAIRS = GROUP // LANES
CONV_K = 3
DIL_PATTERNS = ((128, 1), (512, 4), (2048, 16))
DIL_BLOCK = 128
DIL_GROUP0 = 6
REL_BUCKETS = 32
REL_MAX_DIST = 2048
LN_EPS = 1e-5
SCALE = HEAD_DIM ** -0.5
NEG = -1e30
SIGN_BIT = np.uint32(0x80000000)
VMEM_LIMIT = 56 * 1024 * 1024

BF16 = jnp.bfloat16
F32 = jnp.float32

PROJ_TM = 512
DENSE_TM = 512
FFN_CHUNK = 256
CAUSAL_BQ = 256
SB_SUB = 128
SB_DEAD = 120.0
FOX_Q_COL = 6 * GROUP
FOX_K_COL = 7 * GROUP
FOX_DEAD = 120.0
FOX_NORM_SLACK = 1.02
DIL_GEN_GROUP = 3
DIL_RES_GROUP = 4
COMB_ROWS = 256


def _dot(a, b):
    return jnp.dot(a, b, preferred_element_type=F32)


def _dot_nt(a, b):
    return lax.dot_general(a, b, (((1,), (1,)), ((), ())), preferred_element_type=F32)


def _resident(shape):
    nd = len(shape)
    return pl.BlockSpec(shape, lambda *_: (0,) * nd, pipeline_mode=pl.Buffered(1))


def _params(n_axes):
    return pltpu.CompilerParams(dimension_semantics=("arbitrary",) * n_axes,
                                vmem_limit_bytes=VMEM_LIMIT)


def _split_heads_rows(q, lane):
    zero = jnp.zeros_like(q)
    return jnp.concatenate([jnp.where(lane < HEAD_DIM, q, zero),
                            jnp.where(lane >= HEAD_DIM, q, zero)], axis=0)


def _merge_heads_rows(a, lane, rows):
    return jnp.where(lane < HEAD_DIM, a[:rows], a[rows:])


def _pair_lanes(p):
    return slice(p * LANES, (p + 1) * LANES)


def _group_spec(seq, g):
    return pl.BlockSpec((None, seq, GROUP), lambda bi: (bi, 0, g))


def _in_proj_kernel(x_ref, w_ref, fb_ref, cw_ref, sel_ref,
                    main_ref, dl4_ref, dl16_ref, aug_ref, od_ref, side_ref,
                    carry_ref, ubuf_ref, utail_ref, *stage_refs,
                    tiles_per_seq):
    tm = x_ref.shape[0]
    first = (pl.program_id(0) % tiles_per_seq) == 0
    xb = x_ref[...].astype(BF16)

    n_main = main_ref.shape[1]
    cv0 = n_main
    gate0 = n_main + 3 * GROUP
    cvb = _dot_nt(xb, w_ref[cv0:cv0 + GROUP, :])
    cvc = _dot_nt(xb, w_ref[cv0 + GROUP:cv0 + 2 * GROUP, :])
    cvh = _dot_nt(xb, w_ref[cv0 + 2 * GROUP:cv0 + 3 * GROUP, :])
    u = cvc * cvh

    ubuf_ref[0:8, :] = jnp.where(first, 0.0, utail_ref[...])
    ubuf_ref[8:8 + tm, :] = u
    y = (cw_ref[0:1, :] * ubuf_ref[6:6 + tm, :] + cw_ref[1:2, :] * ubuf_ref[7:7 + tm, :]
         + cw_ref[2:3, :] * u)
    od_ref[...] = (cvb * y).astype(BF16)
    utail_ref[...] = u[tm - 8:tm, :]

    dl_col0 = DIL_GROUP0 * LANES
    row_sq = {}
    for c0 in range(0, n_main, GROUP):
        r = _dot_nt(xb, w_ref[c0:c0 + GROUP, :])
        main_ref[:, c0:c0 + GROUP] = r.astype(BF16)
        if c0 in (FOX_Q_COL, FOX_K_COL):
            row_sq[c0] = jnp.sum(r * r, axis=1, keepdims=True)
        if dl_col0 <= c0 < dl_col0 + 3 * GROUP:
            j = (c0 - dl_col0) // LANES
            stage_refs[j][...] = r[:, :LANES]
            stage_refs[j + 1][...] = r[:, LANES:]

    for dil, ref in ((DIL_PATTERNS[1][1], dl4_ref), (DIL_PATTERNS[2][1], dl16_ref)):
        rows = tm // dil
        for c in range(dil):
            for j in range(len(stage_refs)):
                ref[c, :, j * LANES:(j + 1) * LANES] = (
                    stage_refs[j][pl.ds(c, rows, stride=dil), :].astype(BF16))

    g = _dot_nt(xb, w_ref[gate0:gate0 + LANES, :]) + fb_ref[...]
    logf = jnp.minimum(g, 0.0) - jnp.log(1.0 + jnp.exp(-jnp.abs(g)))
    r_i = lax.broadcasted_iota(jnp.int32, (tm, tm), 0)
    c_i = lax.broadcasted_iota(jnp.int32, (tm, tm), 1)
    tri = jnp.where(c_i <= r_i, 1.0, 0.0).astype(BF16)
    l_hi = logf.astype(BF16)
    l_lo = (logf - l_hi.astype(F32)).astype(BF16)
    csum = _dot(tri, l_hi) + _dot(tri, l_lo)
    prev = jnp.where(first, 0.0, carry_ref[...])
    csum = csum + prev
    carry_ref[...] = csum[tm - 1:tm, :]
    for b in range(tm // CAUSAL_BQ):
        rows = slice(b * CAUSAL_BQ, (b + 1) * CAUSAL_BQ)
        for j, c0 in enumerate((FOX_Q_COL, FOX_K_COL)):
            side_ref[b, j:j + 1, :] = jnp.broadcast_to(jnp.max(row_sq[c0][rows], axis=0, keepdims=True), (1, LANES))
        side_ref[b, 2:3, :] = csum[b * CAUSAL_BQ:b * CAUSAL_BQ + 1, :]
        side_ref[b, 3:4, :] = csum[(b + 1) * CAUSAL_BQ - 1:(b + 1) * CAUSAL_BQ, :]
        side_ref[b, 4:8, :] = jnp.zeros((4, LANES), F32)
    c_hi = csum.astype(BF16)
    c_r = csum - c_hi.astype(F32)
    c_mid = c_r.astype(BF16)
    c_lo = (c_r - c_mid.astype(F32)).astype(BF16)
    ones = jnp.ones((tm, LANES), BF16)
    aug_ref[...] = _dot(jnp.concatenate([c_hi, c_mid, c_lo, ones], axis=1), sel_ref[...]).astype(BF16)


def _aug_selection():
    sel = np.zeros((4 * LANES, 4 * LANES), np.float32)
    ones_row = 3 * LANES
    for h in range(N_HEADS):
        bq = (h // 2) * LANES + (h % 2) * HEAD_DIM
        bk = 2 * LANES + bq
        for part in range(3):
            sel[part * LANES + h, bq + part] = 1.0
            sel[ones_row, bq + 3 + part] = 1.0
            sel[ones_row, bk + part] = 1.0
            sel[part * LANES + h, bk + 3 + part] = -1.0
    return sel


def _in_proj(x2d, w_pad, f_bias, conv_w, batch, seq):
    m, d = x2d.shape
    tm = PROJ_TM
    tps = seq // tm
    n_main = 9 * GROUP
    d4, d16 = DIL_PATTERNS[1][1], DIL_PATTERNS[2][1]
    fb = jnp.pad(f_bias, (0, LANES - N_HEADS))[None, :]
    cw = jnp.pad(conv_w, ((0, 8 - CONV_K), (0, 0)))
    sel = jnp.asarray(_aug_selection(), BF16)
    row = lambda i: (i, 0)
    res = lambda i: (i // tps, 0, i % tps, 0)
    return pl.pallas_call(
        functools.partial(_in_proj_kernel, tiles_per_seq=tps),
        grid=(m // tm,),
        in_specs=[pl.BlockSpec((tm, d), row), _resident(w_pad.shape),
                  _resident(fb.shape), _resident(cw.shape), _resident(sel.shape)],
        out_specs=[pl.BlockSpec((tm, n_main), row),
                   pl.BlockSpec((None, d4, tm // d4, 3 * GROUP), res),
                   pl.BlockSpec((None, d16, tm // d16, 3 * GROUP), res),
                   pl.BlockSpec((tm, 4 * LANES), row),
                   pl.BlockSpec((tm, GROUP), row),
                   pl.BlockSpec((tm // CAUSAL_BQ, 8, LANES), lambda i: (i, 0, 0))],
        out_shape=[jax.ShapeDtypeStruct((m, n_main), BF16),
                   jax.ShapeDtypeStruct((batch, d4, seq // d4, 3 * GROUP), BF16),
                   jax.ShapeDtypeStruct((batch, d16, seq // d16, 3 * GROUP), BF16),
                   jax.ShapeDtypeStruct((m, 4 * LANES), BF16),
                   jax.ShapeDtypeStruct((m, GROUP), BF16),
                   jax.ShapeDtypeStruct((m // CAUSAL_BQ, 8, LANES), F32)],
        scratch_shapes=[pltpu.VMEM((1, LANES), F32), pltpu.VMEM((tm + 8, GROUP), F32),
                        pltpu.VMEM((8, GROUP), F32)]
        + [pltpu.VMEM((tm, LANES), F32)] * (3 * N_PAIRS),
        compiler_params=_params(1),
        name="in_proj",
    )(x2d, w_pad, fb, cw, sel)


def _sb_ops(i, qstart, q_ref, k_ref, v_ref, t_ref, o_ref, acc_ref, carry_ref, live_ref):
    bq, sub = CAUSAL_BQ, SB_SUB
    lane_q = lax.broadcasted_iota(jnp.int32, (bq, LANES), 1)
    qss = []
    for p in range(N_PAIRS):
        qss.append(_split_heads_rows(q_ref[pl.ds(qstart, bq), _pair_lanes(p)], lane_q) * SCALE)
        acc_ref[p] = jnp.zeros((2 * bq, LANES), F32)
        carry_ref[p] = jnp.zeros((2 * bq, sub), F32)

    def scores(p, start, g):
        return _dot_nt(qss[p], k_ref[pl.ds(start, g * sub), _pair_lanes(p)])

    def group(p, start, g, masked, z):
        n = g * sub
        neg_abs = pltpu.bitcast(pltpu.bitcast(z, jnp.uint32) | SIGN_BIT, F32)
        sp = jnp.log(1.0 + jnp.exp(neg_abs))
        log_beta = jnp.minimum(z, 0.0) - sp
        log_rest = log_beta - z
        if masked:
            row = lax.broadcasted_iota(jnp.int32, (2 * bq, n), 0)
            col = lax.broadcasted_iota(jnp.int32, (2 * bq, n), 1)
            strict = (start + col) < (qstart + jnp.where(row >= bq, row - bq, row))
            log_rest = jnp.where(strict, log_rest, 0.0)
        hi = log_rest.astype(BF16)
        lo = (log_rest - hi.astype(F32)).astype(BF16)
        carry = carry_ref[p]
        parts = [None] * g
        for j in reversed(range(g)):
            sl = slice(j * sub, (j + 1) * sub)
            cs = _dot(jnp.concatenate([hi[:, sl], lo[:, sl]], axis=1), t_ref[...])
            a = jnp.exp(log_beta[:, sl] + (cs[:, :sub] + carry))
            carry = carry + cs[:, sub:]
            if masked:
                a = jnp.where(strict[:, sl], a, 0.0)
            parts[j] = a.astype(BF16)
        carry_ref[p] = carry
        acc_ref[p] += _dot(jnp.concatenate(parts, axis=1), v_ref[pl.ds(start, n), _pair_lanes(p)])

    def groups(start, g, masked):
        for p in range(N_PAIRS):
            group(p, start, g, masked, scores(p, start, g))
        if not masked:
            top = functools.reduce(jnp.maximum, [jnp.max(carry_ref[p]) for p in range(N_PAIRS)])
            live_ref[0] = jnp.where(top >= -SB_DEAD, 1, 0)

    def finish():
        for p in range(N_PAIRS):
            o_ref[pl.ds(qstart, bq), _pair_lanes(p)] = _merge_heads_rows(acc_ref[p], lane_q, bq).astype(BF16)

    return groups, finish


def _fox_ops(i, qstart, q_ref, aq_ref, k_ref, ak_ref, v_ref, side_ref, o_ref, m_ref, l_ref, acc_ref, live_ref):
    bq = CAUSAL_BQ
    lane_q = lax.broadcasted_iota(jnp.int32, (bq, LANES), 1)

    def wide(a):
        return jnp.broadcast_to(a, (2 * bq, LANES))

    qss = [jnp.concatenate(
        [_split_heads_rows(q_ref[pl.ds(qstart, bq), _pair_lanes(p)], lane_q) * SCALE,
         _split_heads_rows(aq_ref[pl.ds(qstart, bq), _pair_lanes(p)], lane_q)], axis=1)
        for p in range(N_PAIRS)]

    def scores(p, start, n):
        rows = pl.ds(start, n)
        kk = jnp.concatenate([k_ref[rows, _pair_lanes(p)], ak_ref[rows, _pair_lanes(p)]], axis=1)
        return _dot_nt(qss[p], kk)

    def values(p, start, n):
        return v_ref[pl.ds(start, n), _pair_lanes(p)]

    def diag():
        row = lax.broadcasted_iota(jnp.int32, (2 * bq, bq), 0)
        col = lax.broadcasted_iota(jnp.int32, (2 * bq, bq), 1)
        causal = col <= jnp.where(row >= bq, row - bq, row)
        for p in range(N_PAIRS):
            z = scores(p, qstart, bq)
            vs = values(p, qstart, bq)
            z = jnp.where(causal, z, NEG)
            m0 = jnp.max(z, axis=1, keepdims=True)
            pr = jnp.exp(z - m0)
            m_ref[p] = wide(m0)
            l_ref[p] = wide(jnp.sum(pr, axis=1, keepdims=True))
            acc_ref[p] = _dot(pr.astype(BF16), vs)

    def step(start, n):
        for p in range(N_PAIRS):
            z = scores(p, start, n)
            vs = values(p, start, n)
            m_old = m_ref[p]
            m_new = jnp.maximum(m_old, jnp.max(z, axis=1, keepdims=True))
            alpha = jnp.exp(m_old - m_new)
            pr = jnp.exp(z - jnp.concatenate([m_new] * (n // LANES), axis=1))
            l_ref[p] = alpha * l_ref[p] + jnp.sum(pr, axis=1, keepdims=True)
            acc_ref[p] = alpha * acc_ref[p] + _dot(pr.astype(BF16), vs)
            m_ref[p] = m_new

    def check(end_blk, k_norm_sq):
        q_side = side_ref[i]
        k_side = side_ref[end_blk]
        qk = (0.5 * SCALE * FOX_NORM_SLACK) * (q_side[0:1, :] + k_norm_sq)
        lane = lax.broadcasted_iota(jnp.int32, (1, LANES), 1)
        top = jnp.max(jnp.where(lane < N_HEADS, qk + (q_side[2:3, :] - k_side[3:4, :]), NEG))
        m_min = functools.reduce(jnp.minimum, [jnp.min(m_ref[p]) for p in range(N_PAIRS)])
        live_ref[0] = jnp.where(top - m_min >= -FOX_DEAD, 1, 0)

    def finish():
        for p in range(N_PAIRS):
            o_ref[pl.ds(qstart, bq), _pair_lanes(p)] = _merge_heads_rows(
                acc_ref[p] / l_ref[p], lane_q, bq).astype(BF16)

    return diag, step, check, finish


def _causal_kernel(sq_ref, sk_ref, sv_ref, t_ref, fq_ref, faq_ref, fk_ref, fak_ref, fv_ref, side_ref,
                   so_ref, fo_ref, s_acc, s_carry, s_live, f_m, f_l, f_acc, f_live, *, seq):
    bq = CAUSAL_BQ
    n_q = seq // bq
    k_norm_sq = functools.reduce(jnp.maximum, [side_ref[b, 1:2, :] for b in range(n_q)])

    def q_block(i, outer):
        qstart = pl.multiple_of(i * bq, bq)
        sb_groups, sb_finish = _sb_ops(i, qstart, sq_ref, sk_ref, sv_ref, t_ref, so_ref,
                                                  s_acc, s_carry, s_live)
        fox_diag, fox_step, fox_check, fox_finish = _fox_ops(i, qstart, fq_ref, faq_ref, fk_ref, fak_ref,
                                                             fv_ref, side_ref, fo_ref, f_m, f_l, f_acc, f_live)
        sb_groups(qstart, bq // SB_SUB, True)
        fox_diag()
        s_live[0] = 1
        f_live[0] = 1

        @pl.when((i & 1) == 1)
        def _():
            prev = pl.multiple_of(qstart - bq, bq)
            sb_groups(prev, bq // SB_SUB, False)
            fox_step(prev, bq)
            fox_check(jnp.maximum(i - 2, 0), k_norm_sq)

        n_wide = lax.shift_right_logical(i, 1)

        def cond(state):
            t, sb_on, fox_on = state
            return (t < n_wide) & ((sb_on == 1) | (fox_on == 1))

        def body(state):
            t = state[0]
            first_blk = 2 * (n_wide - 1 - t)
            start = pl.multiple_of(first_blk * bq, 2 * bq)

            @pl.when(f_live[0] == 1)
            def _():
                fox_step(start, 2 * bq)
                fox_check(jnp.maximum(first_blk - 1, 0), k_norm_sq)

            for half in (1, 0):
                @pl.when(s_live[0] == 1)
                def _(half=half):
                    sb_groups(pl.multiple_of(start + half * bq, bq), bq // SB_SUB, False)
            return t + 1, s_live[0], f_live[0]

        lax.while_loop(cond, body, (jnp.int32(0), s_live[0], f_live[0]))
        sb_finish()
        fox_finish()
        return outer

    lax.fori_loop(0, n_q, q_block, 0)


def _sb_tail_matrix():
    sub = SB_SUB
    j = np.arange(2 * sub)[:, None] % sub
    s = np.arange(2 * sub)[None, :]
    return np.where(s < sub, j > s, True).astype(np.float32)


def _causal_attn(main3, aug3, side4):
    b, s, _ = main3.shape
    tmat = jnp.asarray(_sb_tail_matrix(), BF16)
    state = pltpu.VMEM((N_PAIRS, 2 * CAUSAL_BQ, LANES), F32)
    flag = pltpu.SMEM((1,), jnp.int32)
    out = jax.ShapeDtypeStruct((b, s, GROUP), BF16)
    return pl.pallas_call(
        functools.partial(_causal_kernel, seq=s),
        grid=(b,),
        in_specs=[_group_spec(s, 0), _group_spec(s, 1), _group_spec(s, 2), _resident(tmat.shape),
                  _group_spec(s, 6), _group_spec(s, 0), _group_spec(s, 7), _group_spec(s, 1), _group_spec(s, 8),
                  pl.BlockSpec((None,) + side4.shape[1:], lambda bi: (bi, 0, 0, 0))],
        out_specs=[_group_spec(s, 0), _group_spec(s, 0)],
        out_shape=[out, out],
        scratch_shapes=[state, pltpu.VMEM((N_PAIRS, 2 * CAUSAL_BQ, SB_SUB), F32), flag,
                        state, state, state, flag],
        compiler_params=_params(1),
        name="causal_attn",
    )(main3, main3, main3, tmat, main3, aug3, main3, aug3, main3, side4)


def _dil_kernel(*refs, seq):
    n_pat = len(DIL_PATTERNS)
    qkv = refs[:3 * n_pat]
    bias_ref = refs[3 * n_pat]
    o_ref = refs[3 * n_pat + 1]
    scratch = refs[3 * n_pat + 2:]
    n_state = n_pat * N_PAIRS
    u_refs, m_refs, l_refs = scratch[0:n_state], scratch[n_state:2 * n_state], scratch[2 * n_state:3 * n_state]
    blk = DIL_BLOCK
    lane = lax.broadcasted_iota(jnp.int32, (blk, LANES), 1)

    def block(p, pidx, dil, c, n, first):
        q_ref, k_ref, v_ref = qkv[3 * pidx:3 * pidx + 3]
        lanes = _pair_lanes(p)

        def rd(ref, start, size):
            return ref[pl.ds(start, size), lanes] if dil == 1 else ref[c, pl.ds(start, size), lanes]

        qstart = n * blk if isinstance(n, int) else pl.multiple_of(n * blk, blk)
        qs = _split_heads_rows(rd(q_ref, qstart, blk), lane) * SCALE
        h0, h1 = 2 * p, 2 * p + 1
        if first:
            keys, vals = rd(k_ref, 0, blk), rd(v_ref, 0, blk)
            bias = jnp.concatenate([bias_ref[pidx, h0, :, blk:], bias_ref[pidx, h1, :, blk:]], axis=0)
        else:
            kstart = (n - 1) * blk if isinstance(n, int) else pl.multiple_of((n - 1) * blk, blk)
            keys, vals = rd(k_ref, kstart, 2 * blk), rd(v_ref, kstart, 2 * blk)
            bias = jnp.concatenate([bias_ref[pidx, h0], bias_ref[pidx, h1]], axis=0)
        z = _dot_nt(qs, keys) + bias
        m = jnp.max(z, axis=1, keepdims=True)
        pr = jnp.exp(z - m)
        wide = lambda a: jnp.broadcast_to(a, (2 * blk, LANES))
        if (seq // dil) // blk == 1:
            den = wide(jnp.sum(pr, axis=1, keepdims=True))
            u = _dot(pr.astype(BF16), vals)
        else:
            ones = jnp.ones((vals.shape[0], LANES), BF16)
            u = _dot(pr.astype(BF16), jnp.concatenate([vals, ones], axis=1))
            u, den = u[:, :LANES], u[:, LANES:]
        idx = pl.ds(n * (blk * dil) + c, blk, stride=dil) if dil > 1 else pl.ds(qstart, blk)
        st = pidx * N_PAIRS + p
        u_refs[st][idx, :] = _merge_heads_rows(u, lane, blk)
        m_refs[st][idx, :] = _merge_heads_rows(wide(m), lane, blk)
        l_refs[st][idx, :] = _merge_heads_rows(den, lane, blk)

    def blocks(pidx, dil, c, n, first):
        for p in range(N_PAIRS):
            block(p, pidx, dil, c, n, first)

    for pidx, (_, dil) in enumerate(DIL_PATTERNS):
        n_blk = (seq // dil) // blk
        if dil == 1:
            blocks(pidx, dil, 0, 0, True)

            def body(g, carry, pidx=pidx, dil=dil):
                for jj in range(DIL_GEN_GROUP):
                    blocks(pidx, dil, 0, 1 + g * DIL_GEN_GROUP + jj, False)
                return carry
            lax.fori_loop(0, (n_blk - 1) // DIL_GEN_GROUP, body, 0)
        elif n_blk > 1:
            def body(c, carry, pidx=pidx, dil=dil, n_blk=n_blk):
                blocks(pidx, dil, c, 0, True)
                for n in range(1, n_blk):
                    blocks(pidx, dil, c, n, False)
                return carry
            lax.fori_loop(0, dil, body, 0)
        else:
            def body(g, carry, pidx=pidx, dil=dil):
                for jj in range(DIL_RES_GROUP):
                    blocks(pidx, dil, g * DIL_RES_GROUP + jj, 0, True)
                return carry
            lax.fori_loop(0, dil // DIL_RES_GROUP, body, 0)

    def comb(j, carry):
        sl = pl.ds(pl.multiple_of(j * COMB_ROWS, COMB_ROWS), COMB_ROWS)
        for p in range(N_PAIRS):
            sts = [pidx * N_PAIRS + p for pidx in range(n_pat)]
            ms = [m_refs[st][sl, :] for st in sts]
            m_all = functools.reduce(jnp.maximum, ms)
            es = [jnp.exp(mp - m_all) for mp in ms]
            num = sum(e * u_refs[st][sl, :] for e, st in zip(es, sts))
            den = sum(e * l_refs[st][sl, :] for e, st in zip(es, sts))
            o_ref[sl, _pair_lanes(p)] = (num / den).astype(BF16)
        return carry
    lax.fori_loop(0, seq // COMB_ROWS, comb, 0)


def _t5_bucket_of(dist):
    max_exact = REL_BUCKETS // 2
    nf = jnp.maximum(dist, 1).astype(jnp.float32)
    large = max_exact + (jnp.log(nf / max_exact) / math.log(REL_MAX_DIST / max_exact)
                         * (REL_BUCKETS - max_exact)).astype(jnp.int32)
    large = jnp.minimum(large, REL_BUCKETS - 1)
    return jnp.where(dist < max_exact, dist, large)


def _dil_bias_tables(rel_bias):
    blk = DIL_BLOCK
    qi = jnp.arange(blk)[:, None]
    kj = jnp.arange(2 * blk)[None, :]
    sub = qi + blk - kj
    tables = []
    for window, dil in DIL_PATTERNS:
        in_band = (sub >= 0) & (sub <= window // dil)
        bucket = _t5_bucket_of(jnp.maximum(sub, 0) * dil)
        hit = bucket[None, :, :, None] == jnp.arange(REL_BUCKETS)
        bias = jnp.sum(jnp.where(hit, rel_bias.T.astype(F32)[:, None, None, :], 0.0), axis=-1)
        tables.append(jnp.where(in_band[None], bias, NEG))
    return jnp.stack(tables)


def _dil_attn(main3, dl4, dl16, bias_tables):
    b, s, _ = main3.shape
    n_pat = len(DIL_PATTERNS)
    operands, in_specs = [], []
    for role in range(3):
        operands.append(main3)
        in_specs.append(_group_spec(s, DIL_GROUP0 // N_PAIRS + role))
    for arr in (dl4, dl16):
        dil, rows = arr.shape[1], arr.shape[2]
        for role in range(3):
            operands.append(arr)
            in_specs.append(pl.BlockSpec((None, dil, rows, GROUP), lambda bi, role=role: (bi, 0, 0, role)))
    operands.append(bias_tables)
    in_specs.append(_resident(bias_tables.shape))
    return pl.pallas_call(
        functools.partial(_dil_kernel, seq=s),
        grid=(b,),
        in_specs=in_specs,
        out_specs=_group_spec(s, 0),
        out_shape=jax.ShapeDtypeStruct((b, s, GROUP), BF16),
        scratch_shapes=[pltpu.VMEM((s, LANES), F32)] * (3 * n_pat * N_PAIRS),
        compiler_params=_params(1),
        name="dil_attn",
    )(*operands)


def _layer_norm(v, g, b):
    mu = jnp.mean(v, axis=-1, keepdims=True)
    d = v - mu
    var = jnp.mean(d * d, axis=-1, keepdims=True)
    return d * lax.rsqrt(var + LN_EPS) * g + b


def _dense_kernel(x_ref, oa_ref, ob_ref, oc_ref, od_ref, wo_ref, g1_ref, b1_ref,
                  wg_ref, wu_ref, wd_ref, g2_ref, b2_ref, out_ref, h_ref, *, alpha):
    mix = (_dot(oa_ref[...], wo_ref[0:GROUP, :]) + _dot(ob_ref[...], wo_ref[GROUP:2 * GROUP, :])
           + _dot(oc_ref[...], wo_ref[2 * GROUP:3 * GROUP, :]) + _dot(od_ref[...], wo_ref[3 * GROUP:4 * GROUP, :]))
    x1 = _layer_norm(alpha * x_ref[...] + mix, g1_ref[...], b1_ref[...])
    xb = x1.astype(BF16)
    d_ff = wg_ref.shape[1]
    for c0 in range(0, d_ff, FFN_CHUNK):
        g = _dot(xb, wg_ref[:, c0:c0 + FFN_CHUNK])
        u = _dot(xb, wu_ref[:, c0:c0 + FFN_CHUNK])
        h_ref[:, c0:c0 + FFN_CHUNK] = (g * (1.0 / (1.0 + jnp.exp(-g))) * u).astype(BF16)
    y = _dot(h_ref[...], wd_ref[...])
    out_ref[...] = _layer_norm(alpha * x1 + y, g2_ref[...], b2_ref[...])


def _dense(x2d, outs, w_out, g1, b1, w_gate, w_up, w_down, g2, b2, alpha):
    m, d = x2d.shape
    tm = DENSE_TM
    d_ff = w_gate.shape[1]
    row = lambda i: (i, 0)
    vec = lambda a: a[None, :]
    o_spec = pl.BlockSpec((tm, GROUP), row)
    return pl.pallas_call(
        functools.partial(_dense_kernel, alpha=alpha),
        grid=(m // tm,),
        in_specs=[pl.BlockSpec((tm, d), row), o_spec, o_spec, o_spec, o_spec,
                  _resident((d, d)), _resident((1, d)), _resident((1, d)),
                  _resident((d, d_ff)), _resident((d, d_ff)), _resident((d_ff, d)),
                  _resident((1, d)), _resident((1, d))],
        out_specs=pl.BlockSpec((tm, d), row),
        out_shape=jax.ShapeDtypeStruct((m, d), F32),
        scratch_shapes=[pltpu.VMEM((tm, d_ff), BF16)],
        compiler_params=_params(1),
        name="dense",
    )(x2d, *outs, w_out.astype(BF16), vec(g1), vec(b1), w_gate.astype(BF16), w_up.astype(BF16),
      w_down.astype(BF16), vec(g2), vec(b2))


def kernel(x, w_in, f_bias, conv_w, w_out, rel_bias, ln1_g, ln1_b, w_gate, w_up, w_down, ln2_g, ln2_b):
    b, s, d = x.shape
    depth = w_in.shape[0]
    assert d == 4 * GROUP and w_in.shape[2] == 12 * GROUP + N_HEADS
    assert s % PROJ_TM == 0 and s % (2 * CAUSAL_BQ) == 0 and s % COMB_ROWS == 0
    assert [dil for _, dil in DIL_PATTERNS][0] == 1
    for _, dil in DIL_PATTERNS:
        n_blk = (s // dil) // DIL_BLOCK
        assert n_blk * DIL_BLOCK * dil == s and PROJ_TM % (16 * dil) == 0
        assert (n_blk - 1) % DIL_GEN_GROUP == 0 if dil == 1 else (n_blk > 1 or dil % DIL_RES_GROUP == 0)
    alpha = (2 * depth) ** 0.25
    bias_tables = _dil_bias_tables(rel_bias)
    w_pad = jnp.pad(jnp.swapaxes(w_in, 1, 2), ((0, 0), (0, LANES - N_HEADS), (0, 0))).astype(BF16)
    x2d = x.reshape(b * s, d)
    for layer in range(depth):
        main, dl4, dl16, aug, out_d, side = _in_proj(x2d, w_pad[layer], f_bias[layer], conv_w[layer], b, s)
        main3 = main.reshape(b, s, main.shape[1])
        out_a, out_c = _causal_attn(main3, aug.reshape(b, s, aug.shape[1]),
                                    side.reshape((b, s // CAUSAL_BQ) + side.shape[1:]))
        out_b = _dil_attn(main3, dl4, dl16, bias_tables)
        outs = [o.reshape(b * s, GROUP) for o in (out_a, out_b, out_c)] + [out_d]
        x2d = _dense(x2d, outs, w_out[layer], ln1_g[layer], ln1_b[layer], w_gate[layer], w_up[layer],
                     w_down[layer], ln2_g[layer], ln2_b[layer], alpha)
    return x2d.reshape(b, s, d)
```

```python
import functools
import math

import jax
import jax.numpy as jnp
import numpy as np
from jax import lax
from jax.experimental import pallas as pl
from jax.experimental.pallas import tpu as pltpu

HEAD_DIM = 64
N_HEADS = 4
GROUP = N_HEADS * HEAD_DIM
LANES = 128
N_PAIRS = GROUP // LANES
CONV_K = 3
DIL_PATTERNS = ((128, 1), (512, 4), (2048, 16))
DIL_BLOCK = 128
DIL_GROUP0 = 6
REL_BUCKETS = 32
REL_MAX_DIST = 2048
LN_EPS = 1e-5
SCALE = HEAD_DIM ** -0.5
NEG = -1e30
SIGN_BIT = np.uint32(0x80000000)
VMEM_LIMIT = 56 * 1024 * 1024

BF16 = jnp.bfloat16
F32 = jnp.float32

PROJ_TM = 512
DENSE_TM = 512
FFN_CHUNK = 256
CAUSAL_BQ = 256
SB_SUB = 128
SB_DEAD = 120.0
FOX_Q_COL = 6 * GROUP
FOX_K_COL = 7 * GROUP
FOX_DEAD = 120.0
FOX_NORM_SLACK = 1.02
DIL_GEN_GROUP = 15
DIL_MID_GROUP = 4
DIL_RES_GROUP = 8
COMB_ROWS = 256


def _dot(a, b):
    return jnp.dot(a, b, preferred_element_type=F32)


def _dot_nt(a, b):
    return lax.dot_general(a, b, (((1,), (1,)), ((), ())), preferred_element_type=F32)


def _resident(shape):
    nd = len(shape)
    return pl.BlockSpec(shape, lambda *_: (0,) * nd, pipeline_mode=pl.Buffered(1))


def _params(n_axes):
    return pltpu.CompilerParams(dimension_semantics=("arbitrary",) * n_axes,
                                vmem_limit_bytes=VMEM_LIMIT)


def _split_heads_rows(q, lane):
    zero = jnp.zeros_like(q)
    return jnp.concatenate([jnp.where(lane < HEAD_DIM, q, zero),
                            jnp.where(lane >= HEAD_DIM, q, zero)], axis=0)


def _merge_heads_rows(a, lane, rows):
    return jnp.where(lane < HEAD_DIM, a[:rows], a[rows:])


def _pair_lanes(p):
    return slice(p * LANES, (p + 1) * LANES)


def _group_spec(seq, g):
    return pl.BlockSpec((None, seq, GROUP), lambda bi: (bi, 0, g))


def _in_proj_kernel(x_ref, w_ref, fb_ref, cw_ref, sel_ref,
                    main_ref, dl4_ref, dl16_ref, aug_ref, od_ref, side_ref,
                    carry_ref, ubuf_ref, utail_ref, *stage_refs,
                    tiles_per_seq):
    tm = x_ref.shape[0]
    first = (pl.program_id(0) % tiles_per_seq) == 0
    xb = x_ref[...].astype(BF16)

    n_main = main_ref.shape[1]
    cv0 = n_main
    gate0 = n_main + 3 * GROUP
    cvb = _dot_nt(xb, w_ref[cv0:cv0 + GROUP, :])
    cvc = _dot_nt(xb, w_ref[cv0 + GROUP:cv0 + 2 * GROUP, :])
    cvh = _dot_nt(xb, w_ref[cv0 + 2 * GROUP:cv0 + 3 * GROUP, :])
    u = cvc * cvh

    ubuf_ref[0:8, :] = jnp.where(first, 0.0, utail_ref[...])
    ubuf_ref[8:8 + tm, :] = u
    y = (cw_ref[0:1, :] * ubuf_ref[6:6 + tm, :] + cw_ref[1:2, :] * ubuf_ref[7:7 + tm, :]
         + cw_ref[2:3, :] * u)
    od_ref[...] = (cvb * y).astype(BF16)
    utail_ref[...] = u[tm - 8:tm, :]

    dl_col0 = DIL_GROUP0 * LANES
    row_sq = {}
    for c0 in range(0, n_main, GROUP):
        r = _dot_nt(xb, w_ref[c0:c0 + GROUP, :])
        main_ref[:, c0:c0 + GROUP] = r.astype(BF16)
        if c0 in (FOX_Q_COL, FOX_K_COL):
            row_sq[c0] = jnp.sum(r * r, axis=1, keepdims=True)
        if dl_col0 <= c0 < dl_col0 + 3 * GROUP:
            j = (c0 - dl_col0) // LANES
            stage_refs[j][...] = r[:, :LANES]
            stage_refs[j + 1][...] = r[:, LANES:]

    for dil, ref in ((DIL_PATTERNS[1][1], dl4_ref), (DIL_PATTERNS[2][1], dl16_ref)):
        rows = tm // dil
        for c in range(dil):
            for j in range(len(stage_refs)):
                ref[c, :, j * LANES:(j + 1) * LANES] = (
                    stage_refs[j][pl.ds(c, rows, stride=dil), :].astype(BF16))

    g = _dot_nt(xb, w_ref[gate0:gate0 + LANES, :]) + fb_ref[...]
    logf = jnp.minimum(g, 0.0) - jnp.log(1.0 + jnp.exp(-jnp.abs(g)))
    r_i = lax.broadcasted_iota(jnp.int32, (tm, tm), 0)
    c_i = lax.broadcasted_iota(jnp.int32, (tm, tm), 1)
    tri = jnp.where(c_i <= r_i, 1.0, 0.0).astype(BF16)
    l_hi = logf.astype(BF16)
    l_lo = (logf - l_hi.astype(F32)).astype(BF16)
    both = _dot(tri, jnp.concatenate([l_hi, l_lo], axis=1))
    csum = both[:, :LANES] + both[:, LANES:]
    prev = jnp.where(first, 0.0, carry_ref[...])
    csum = csum + prev
    carry_ref[...] = csum[tm - 1:tm, :]
    for b in range(tm // CAUSAL_BQ):
        rows = slice(b * CAUSAL_BQ, (b + 1) * CAUSAL_BQ)
        for j, c0 in enumerate((FOX_Q_COL, FOX_K_COL)):
            side_ref[b, j:j + 1, :] = jnp.broadcast_to(jnp.max(row_sq[c0][rows], axis=0, keepdims=True), (1, LANES))
        side_ref[b, 2:3, :] = csum[b * CAUSAL_BQ:b * CAUSAL_BQ + 1, :]
        side_ref[b, 3:4, :] = csum[(b + 1) * CAUSAL_BQ - 1:(b + 1) * CAUSAL_BQ, :]
        side_ref[b, 4:8, :] = jnp.zeros((4, LANES), F32)
    c_hi = csum.astype(BF16)
    c_r = csum - c_hi.astype(F32)
    c_mid = c_r.astype(BF16)
    c_lo = (c_r - c_mid.astype(F32)).astype(BF16)
    ones = jnp.ones((tm, LANES), BF16)
    aug_ref[...] = _dot(jnp.concatenate([c_hi, c_mid, c_lo, ones], axis=1), sel_ref[...]).astype(BF16)


def _aug_selection():
    sel = np.zeros((4 * LANES, 4 * LANES), np.float32)
    ones_row = 3 * LANES
    for h in range(N_HEADS):
        bq = (h // 2) * LANES + (h % 2) * HEAD_DIM
        bk = 2 * LANES + bq
        for part in range(3):
            sel[part * LANES + h, bq + part] = 1.0
            sel[ones_row, bq + 3 + part] = 1.0
            sel[ones_row, bk + part] = 1.0
            sel[part * LANES + h, bk + 3 + part] = -1.0
    return sel


def _in_proj(x2d, w_pad, f_bias, conv_w, batch, seq):
    m, d = x2d.shape
    tm = PROJ_TM
    tps = seq // tm
    n_main = 9 * GROUP
    d4, d16 = DIL_PATTERNS[1][1], DIL_PATTERNS[2][1]
    fb = jnp.pad(f_bias, (0, LANES - N_HEADS))[None, :]
    cw = jnp.pad(conv_w, ((0, 8 - CONV_K), (0, 0)))
    sel = jnp.asarray(_aug_selection(), BF16)
    row = lambda i: (i, 0)
    res = lambda i: (i // tps, 0, i % tps, 0)
    return pl.pallas_call(
        functools.partial(_in_proj_kernel, tiles_per_seq=tps),
        grid=(m // tm,),
        in_specs=[pl.BlockSpec((tm, d), row), _resident(w_pad.shape),
                  _resident(fb.shape), _resident(cw.shape), _resident(sel.shape)],
        out_specs=[pl.BlockSpec((tm, n_main), row),
                   pl.BlockSpec((None, d4, tm // d4, 3 * GROUP), res),
                   pl.BlockSpec((None, d16, tm // d16, 3 * GROUP), res),
                   pl.BlockSpec((tm, 4 * LANES), row),
                   pl.BlockSpec((tm, GROUP), row),
                   pl.BlockSpec((tm // CAUSAL_BQ, 8, LANES), lambda i: (i, 0, 0))],
        out_shape=[jax.ShapeDtypeStruct((m, n_main), BF16),
                   jax.ShapeDtypeStruct((batch, d4, seq // d4, 3 * GROUP), BF16),
                   jax.ShapeDtypeStruct((batch, d16, seq // d16, 3 * GROUP), BF16),
                   jax.ShapeDtypeStruct((m, 4 * LANES), BF16),
                   jax.ShapeDtypeStruct((m, GROUP), BF16),
                   jax.ShapeDtypeStruct((m // CAUSAL_BQ, 8, LANES), F32)],
        scratch_shapes=[pltpu.VMEM((1, LANES), F32), pltpu.VMEM((tm + 8, GROUP), F32),
                        pltpu.VMEM((8, GROUP), F32)]
        + [pltpu.VMEM((tm, LANES), F32)] * (3 * N_PAIRS),
        compiler_params=_params(1),
        name="in_proj",
    )(x2d, w_pad, fb, cw, sel)


def _sb_ops(i, qstart, q_ref, k_ref, v_ref, t_ref, o_ref, acc_ref, carry_ref, live_ref):
    bq, sub = CAUSAL_BQ, SB_SUB
    lane_q = lax.broadcasted_iota(jnp.int32, (bq, LANES), 1)
    qss = []
    for p in range(N_PAIRS):
        qss.append(_split_heads_rows(q_ref[pl.ds(qstart, bq), _pair_lanes(p)], lane_q) * SCALE)
        acc_ref[p] = jnp.zeros((2 * bq, LANES), F32)
        carry_ref[p] = jnp.zeros((2 * bq, sub), F32)

    def scores(p, start, g):
        return _dot_nt(qss[p], k_ref[pl.ds(start, g * sub), _pair_lanes(p)])

    def group(p, start, g, masked, z):
        n = g * sub
        neg_abs = pltpu.bitcast(pltpu.bitcast(z, jnp.uint32) | SIGN_BIT, F32)
        sp = jnp.log(1.0 + jnp.exp(neg_abs))
        log_beta = jnp.minimum(z, 0.0) - sp
        log_rest = log_beta - z
        if masked:
            row = lax.broadcasted_iota(jnp.int32, (2 * bq, n), 0)
            col = lax.broadcasted_iota(jnp.int32, (2 * bq, n), 1)
            strict = (start + col) < (qstart + jnp.where(row >= bq, row - bq, row))
            log_rest = jnp.where(strict, log_rest, 0.0)
        hi = log_rest.astype(BF16)
        lo = (log_rest - hi.astype(F32)).astype(BF16)
        carry = carry_ref[p]
        parts = [None] * g
        for j in reversed(range(g)):
            sl = slice(j * sub, (j + 1) * sub)
            cs = _dot(jnp.concatenate([hi[:, sl], lo[:, sl]], axis=1), t_ref[...])
            a = jnp.exp(log_beta[:, sl] + (cs[:, :sub] + carry))
            carry = carry + cs[:, sub:]
            if masked:
                a = jnp.where(strict[:, sl], a, 0.0)
            parts[j] = a.astype(BF16)
        carry_ref[p] = carry
        acc_ref[p] += _dot(jnp.concatenate(parts, axis=1), v_ref[pl.ds(start, n), _pair_lanes(p)])

    def groups(start, g, masked):
        for p in range(N_PAIRS):
            group(p, start, g, masked, scores(p, start, g))
        if not masked:
            top = functools.reduce(jnp.maximum, [jnp.max(carry_ref[p]) for p in range(N_PAIRS)])
            live_ref[0] = jnp.where(top >= -SB_DEAD, 1, 0)

    def finish():
        for p in range(N_PAIRS):
            o_ref[pl.ds(qstart, bq), _pair_lanes(p)] = _merge_heads_rows(acc_ref[p], lane_q, bq).astype(BF16)

    return groups, finish


def _fox_ops(i, qstart, q_ref, aq_ref, k_ref, ak_ref, v_ref, side_ref, o_ref, m_ref, l_ref, acc_ref, live_ref):
    bq = CAUSAL_BQ
    lane_q = lax.broadcasted_iota(jnp.int32, (bq, LANES), 1)

    def wide(a):
        return jnp.broadcast_to(a, (2 * bq, LANES))

    qss = [jnp.concatenate(
        [_split_heads_rows(q_ref[pl.ds(qstart, bq), _pair_lanes(p)], lane_q) * SCALE,
         _split_heads_rows(aq_ref[pl.ds(qstart, bq), _pair_lanes(p)], lane_q)], axis=1)
        for p in range(N_PAIRS)]

    def scores(p, start, n):
        rows = pl.ds(start, n)
        kk = jnp.concatenate([k_ref[rows, _pair_lanes(p)], ak_ref[rows, _pair_lanes(p)]], axis=1)
        return _dot_nt(qss[p], kk)

    def values(p, start, n):
        return v_ref[pl.ds(start, n), _pair_lanes(p)]

    def diag():
        row = lax.broadcasted_iota(jnp.int32, (2 * bq, bq), 0)
        col = lax.broadcasted_iota(jnp.int32, (2 * bq, bq), 1)
        causal = col <= jnp.where(row >= bq, row - bq, row)
        for p in range(N_PAIRS):
            z = scores(p, qstart, bq)
            vs = values(p, qstart, bq)
            z = jnp.where(causal, z, NEG)
            m0 = jnp.max(z, axis=1, keepdims=True)
            pr = jnp.exp(z - m0)
            m_ref[p] = wide(m0)
            l_ref[p] = wide(jnp.sum(pr, axis=1, keepdims=True))
            acc_ref[p] = _dot(pr.astype(BF16), vs)

    def step(start, n):
        for p in range(N_PAIRS):
            z = scores(p, start, n)
            vs = values(p, start, n)
            m_old = m_ref[p]
            m_new = jnp.maximum(m_old, jnp.max(z, axis=1, keepdims=True))
            alpha = jnp.exp(m_old - m_new)
            pr = jnp.exp(z - jnp.concatenate([m_new] * (n // LANES), axis=1))
            l_ref[p] = alpha * l_ref[p] + jnp.sum(pr, axis=1, keepdims=True)
            acc_ref[p] = alpha * acc_ref[p] + _dot(pr.astype(BF16), vs)
            m_ref[p] = m_new

    def check(end_blk, k_norm_sq):
        q_side = side_ref[i]
        k_side = side_ref[end_blk]
        qk = (0.5 * SCALE * FOX_NORM_SLACK) * (q_side[0:1, :] + k_norm_sq)
        lane = lax.broadcasted_iota(jnp.int32, (1, LANES), 1)
        top = jnp.max(jnp.where(lane < N_HEADS, qk + (q_side[2:3, :] - k_side[3:4, :]), NEG))
        m_min = functools.reduce(jnp.minimum, [jnp.min(m_ref[p]) for p in range(N_PAIRS)])
        live_ref[0] = jnp.where(top - m_min >= -FOX_DEAD, 1, 0)

    def finish():
        for p in range(N_PAIRS):
            o_ref[pl.ds(qstart, bq), _pair_lanes(p)] = _merge_heads_rows(
                acc_ref[p] / l_ref[p], lane_q, bq).astype(BF16)

    return diag, step, check, finish


def _causal_kernel(sq_ref, sk_ref, sv_ref, t_ref, fq_ref, faq_ref, fk_ref, fak_ref, fv_ref, side_ref,
                   so_ref, fo_ref, s_acc, s_carry, s_live, f_m, f_l, f_acc, f_live, *, seq):
    bq = CAUSAL_BQ
    n_q = seq // bq
    k_norm_sq = functools.reduce(jnp.maximum, [side_ref[b, 1:2, :] for b in range(n_q)])

    def q_block(i, outer):
        qstart = pl.multiple_of(i * bq, bq)
        sb_groups, sb_finish = _sb_ops(i, qstart, sq_ref, sk_ref, sv_ref, t_ref, so_ref,
                                                  s_acc, s_carry, s_live)
        fox_diag, fox_step, fox_check, fox_finish = _fox_ops(i, qstart, fq_ref, faq_ref, fk_ref, fak_ref,
                                                             fv_ref, side_ref, fo_ref, f_m, f_l, f_acc, f_live)
        sb_groups(qstart, bq // SB_SUB, True)
        fox_diag()
        s_live[0] = 1
        f_live[0] = 1

        @pl.when((i & 1) == 1)
        def _():
            prev = pl.multiple_of(qstart - bq, bq)
            sb_groups(prev, bq // SB_SUB, False)
            fox_step(prev, bq)
            fox_check(jnp.maximum(i - 2, 0), k_norm_sq)

        n_wide = lax.shift_right_logical(i, 1)

        def cond(state):
            t, sb_on, fox_on = state
            return (t < n_wide) & ((sb_on == 1) | (fox_on == 1))

        def body(state):
            t = state[0]
            first_blk = 2 * (n_wide - 1 - t)
            start = pl.multiple_of(first_blk * bq, 2 * bq)

            @pl.when(f_live[0] == 1)
            def _():
                fox_step(start, 2 * bq)
                fox_check(jnp.maximum(first_blk - 1, 0), k_norm_sq)

            for half in (1, 0):
                @pl.when(s_live[0] == 1)
                def _(half=half):
                    sb_groups(pl.multiple_of(start + half * bq, bq), bq // SB_SUB, False)
            return t + 1, s_live[0], f_live[0]

        lax.while_loop(cond, body, (jnp.int32(0), s_live[0], f_live[0]))
        sb_finish()
        fox_finish()
        return outer

    lax.fori_loop(0, n_q, q_block, 0)


def _sb_tail_matrix():
    sub = SB_SUB
    j = np.arange(2 * sub)[:, None] % sub
    s = np.arange(2 * sub)[None, :]
    return np.where(s < sub, j > s, True).astype(np.float32)


def _causal_attn(main3, aug3, side4):
    b, s, _ = main3.shape
    tmat = jnp.asarray(_sb_tail_matrix(), BF16)
    state = pltpu.VMEM((N_PAIRS, 2 * CAUSAL_BQ, LANES), F32)
    flag = pltpu.SMEM((1,), jnp.int32)
    out = jax.ShapeDtypeStruct((b, s, GROUP), BF16)
    return pl.pallas_call(
        functools.partial(_causal_kernel, seq=s),
        grid=(b,),
        in_specs=[_group_spec(s, 0), _group_spec(s, 1), _group_spec(s, 2), _resident(tmat.shape),
                  _group_spec(s, 6), _group_spec(s, 0), _group_spec(s, 7), _group_spec(s, 1), _group_spec(s, 8),
                  pl.BlockSpec((None,) + side4.shape[1:], lambda bi: (bi, 0, 0, 0))],
        out_specs=[_group_spec(s, 0), _group_spec(s, 0)],
        out_shape=[out, out],
        scratch_shapes=[state, pltpu.VMEM((N_PAIRS, 2 * CAUSAL_BQ, SB_SUB), F32), flag,
                        state, state, state, flag],
        compiler_params=_params(1),
        name="causal_attn",
    )(main3, main3, main3, tmat, main3, aug3, main3, aug3, main3, side4)


def _dil_kernel(*refs, seq):
    n_pat = len(DIL_PATTERNS)
    qkv = refs[:3 * n_pat]
    bias_ref = refs[3 * n_pat]
    o_ref = refs[3 * n_pat + 1]
    scratch = refs[3 * n_pat + 2:]
    n_state = n_pat * N_PAIRS
    u_refs, m_refs, l_refs = scratch[0:n_state], scratch[n_state:2 * n_state], scratch[2 * n_state:3 * n_state]
    blk = DIL_BLOCK
    lane = lax.broadcasted_iota(jnp.int32, (blk, LANES), 1)

    def block(p, pidx, dil, c, n, first):
        q_ref, k_ref, v_ref = qkv[3 * pidx:3 * pidx + 3]
        lanes = _pair_lanes(p)

        def rd(ref, start, size):
            return ref[pl.ds(start, size), lanes] if dil == 1 else ref[c, pl.ds(start, size), lanes]

        qstart = n * blk if isinstance(n, int) else pl.multiple_of(n * blk, blk)
        qs = _split_heads_rows(rd(q_ref, qstart, blk), lane) * SCALE
        h0, h1 = 2 * p, 2 * p + 1
        if first:
            keys, vals = rd(k_ref, 0, blk), rd(v_ref, 0, blk)
            bias = jnp.concatenate([bias_ref[pidx, h0, :, blk:], bias_ref[pidx, h1, :, blk:]], axis=0)
        else:
            kstart = (n - 1) * blk if isinstance(n, int) else pl.multiple_of((n - 1) * blk, blk)
            keys, vals = rd(k_ref, kstart, 2 * blk), rd(v_ref, kstart, 2 * blk)
            bias = jnp.concatenate([bias_ref[pidx, h0], bias_ref[pidx, h1]], axis=0)
        z = _dot_nt(qs, keys) + bias
        m = jnp.max(z, axis=1, keepdims=True)
        pr = jnp.exp(z - m)
        wide = lambda a: jnp.broadcast_to(a, (2 * blk, LANES))
        if (seq // dil) // blk == 1:
            den = wide(jnp.sum(pr, axis=1, keepdims=True))
            u = _dot(pr.astype(BF16), vals)
        else:
            ones = jnp.ones((vals.shape[0], LANES), BF16)
            u = _dot(pr.astype(BF16), jnp.concatenate([vals, ones], axis=1))
            u, den = u[:, :LANES], u[:, LANES:]
        idx = pl.ds(n * (blk * dil) + c, blk, stride=dil) if dil > 1 else pl.ds(qstart, blk)
        st = pidx * N_PAIRS + p
        u_refs[st][idx, :] = _merge_heads_rows(u, lane, blk)
        m_refs[st][idx, :] = _merge_heads_rows(wide(m), lane, blk)
        l_refs[st][idx, :] = _merge_heads_rows(den, lane, blk)

    def blocks(pidx, dil, c, n, first):
        for p in range(N_PAIRS):
            block(p, pidx, dil, c, n, first)

    for pidx, (_, dil) in enumerate(DIL_PATTERNS):
        n_blk = (seq // dil) // blk
        if dil == 1:
            blocks(pidx, dil, 0, 0, True)

            def body(g, carry, pidx=pidx, dil=dil):
                for jj in range(DIL_GEN_GROUP):
                    blocks(pidx, dil, 0, 1 + g * DIL_GEN_GROUP + jj, False)
                return carry
            lax.fori_loop(0, (n_blk - 1) // DIL_GEN_GROUP, body, 0)
        elif n_blk > 1:
            def body(g, carry, pidx=pidx, dil=dil, n_blk=n_blk):
                for jj in range(DIL_MID_GROUP):
                    c = g * DIL_MID_GROUP + jj
                    blocks(pidx, dil, c, 0, True)
                    for n in range(1, n_blk):
                        blocks(pidx, dil, c, n, False)
                return carry
            lax.fori_loop(0, dil // DIL_MID_GROUP, body, 0)
        else:
            def body(g, carry, pidx=pidx, dil=dil):
                for jj in range(DIL_RES_GROUP):
                    blocks(pidx, dil, g * DIL_RES_GROUP + jj, 0, True)
                return carry
            lax.fori_loop(0, dil // DIL_RES_GROUP, body, 0)

    def comb(j, carry):
        sl = pl.ds(pl.multiple_of(j * COMB_ROWS, COMB_ROWS), COMB_ROWS)
        for p in range(N_PAIRS):
            sts = [pidx * N_PAIRS + p for pidx in range(n_pat)]
            ms = [m_refs[st][sl, :] for st in sts]
            m_all = functools.reduce(jnp.maximum, ms)
            es = [jnp.exp(mp - m_all) for mp in ms]
            num = sum(e * u_refs[st][sl, :] for e, st in zip(es, sts))
            den = sum(e * l_refs[st][sl, :] for e, st in zip(es, sts))
            o_ref[sl, _pair_lanes(p)] = (num / den).astype(BF16)
        return carry
    lax.fori_loop(0, seq // COMB_ROWS, comb, 0)


def _t5_bucket_of(dist):
    max_exact = REL_BUCKETS // 2
    nf = jnp.maximum(dist, 1).astype(jnp.float32)
    large = max_exact + (jnp.log(nf / max_exact) / math.log(REL_MAX_DIST / max_exact)
                         * (REL_BUCKETS - max_exact)).astype(jnp.int32)
    large = jnp.minimum(large, REL_BUCKETS - 1)
    return jnp.where(dist < max_exact, dist, large)


def _dil_bias_tables(rel_bias):
    blk = DIL_BLOCK
    qi = jnp.arange(blk)[:, None]
    kj = jnp.arange(2 * blk)[None, :]
    sub = qi + blk - kj
    tables = []
    for window, dil in DIL_PATTERNS:
        in_band = (sub >= 0) & (sub <= window // dil)
        bucket = _t5_bucket_of(jnp.maximum(sub, 0) * dil)
        hit = bucket[None, :, :, None] == jnp.arange(REL_BUCKETS)
        bias = jnp.sum(jnp.where(hit, rel_bias.T.astype(F32)[:, None, None, :], 0.0), axis=-1)
        tables.append(jnp.where(in_band[None], bias, NEG))
    return jnp.stack(tables)


def _dil_attn(main3, dl4, dl16, bias_tables):
    b, s, _ = main3.shape
    n_pat = len(DIL_PATTERNS)
    operands, in_specs = [], []
    for role in range(3):
        operands.append(main3)
        in_specs.append(_group_spec(s, DIL_GROUP0 // N_PAIRS + role))
    for arr in (dl4, dl16):
        dil, rows = arr.shape[1], arr.shape[2]
        for role in range(3):
            operands.append(arr)
            in_specs.append(pl.BlockSpec((None, dil, rows, GROUP), lambda bi, role=role: (bi, 0, 0, role)))
    operands.append(bias_tables)
    in_specs.append(_resident(bias_tables.shape))
    return pl.pallas_call(
        functools.partial(_dil_kernel, seq=s),
        grid=(b,),
        in_specs=in_specs,
        out_specs=_group_spec(s, 0),
        out_shape=jax.ShapeDtypeStruct((b, s, GROUP), BF16),
        scratch_shapes=[pltpu.VMEM((s, LANES), F32)] * (3 * n_pat * N_PAIRS),
        compiler_params=_params(1),
        name="dil_attn",
    )(*operands)


def _layer_norm(v, g, b):
    mu = jnp.mean(v, axis=-1, keepdims=True)
    d = v - mu
    var = jnp.mean(d * d, axis=-1, keepdims=True)
    return d * lax.rsqrt(var + LN_EPS) * g + b


def _dense_kernel(x_ref, oa_ref, ob_ref, oc_ref, od_ref, wo_ref, g1_ref, b1_ref,
                  wg_ref, wu_ref, wd_ref, g2_ref, b2_ref, out_ref, h_ref, *, alpha):
    mix = (_dot(oa_ref[...], wo_ref[0:GROUP, :]) + _dot(ob_ref[...], wo_ref[GROUP:2 * GROUP, :])
           + _dot(oc_ref[...], wo_ref[2 * GROUP:3 * GROUP, :]) + _dot(od_ref[...], wo_ref[3 * GROUP:4 * GROUP, :]))
    x1 = _layer_norm(alpha * x_ref[...] + mix, g1_ref[...], b1_ref[...])
    xb = x1.astype(BF16)
    d_ff = wg_ref.shape[1]
    for c0 in range(0, d_ff, FFN_CHUNK):
        g = _dot(xb, wg_ref[:, c0:c0 + FFN_CHUNK])
        u = _dot(xb, wu_ref[:, c0:c0 + FFN_CHUNK])
        h_ref[:, c0:c0 + FFN_CHUNK] = (g * (1.0 / (1.0 + jnp.exp(-g))) * u).astype(BF16)
    y = _dot(h_ref[...], wd_ref[...])
    out_ref[...] = _layer_norm(alpha * x1 + y, g2_ref[...], b2_ref[...])


def _dense(x2d, outs, w_out, g1, b1, w_gate, w_up, w_down, g2, b2, alpha):
    m, d = x2d.shape
    tm = DENSE_TM
    d_ff = w_gate.shape[1]
    row = lambda i: (i, 0)
    vec = lambda a: a[None, :]
    o_spec = pl.BlockSpec((tm, GROUP), row)
    return pl.pallas_call(
        functools.partial(_dense_kernel, alpha=alpha),
        grid=(m // tm,),
        in_specs=[pl.BlockSpec((tm, d), row), o_spec, o_spec, o_spec, o_spec,
                  _resident((d, d)), _resident((1, d)), _resident((1, d)),
                  _resident((d, d_ff)), _resident((d, d_ff)), _resident((d_ff, d)),
                  _resident((1, d)), _resident((1, d))],
        out_specs=pl.BlockSpec((tm, d), row),
        out_shape=jax.ShapeDtypeStruct((m, d), F32),
        scratch_shapes=[pltpu.VMEM((tm, d_ff), BF16)],
        compiler_params=_params(1),
        name="dense",
    )(x2d, *outs, w_out.astype(BF16), vec(g1), vec(b1), w_gate.astype(BF16), w_up.astype(BF16),
      w_down.astype(BF16), vec(g2), vec(b2))


def kernel(x, w_in, f_bias, conv_w, w_out, rel_bias, ln1_g, ln1_b, w_gate, w_up, w_down, ln2_g, ln2_b):
    b, s, d = x.shape
    depth = w_in.shape[0]
    assert d == 4 * GROUP and w_in.shape[2] == 12 * GROUP + N_HEADS
    assert s % PROJ_TM == 0 and s % (2 * CAUSAL_BQ) == 0 and s % COMB_ROWS == 0
    assert [dil for _, dil in DIL_PATTERNS][0] == 1
    for _, dil in DIL_PATTERNS:
        n_blk = (s // dil) // DIL_BLOCK
        assert n_blk * DIL_BLOCK * dil == s and PROJ_TM % (16 * dil) == 0
        assert ((n_blk - 1) % DIL_GEN_GROUP == 0 if dil == 1 else
                dil % (DIL_MID_GROUP if n_blk > 1 else DIL_RES_GROUP) == 0)
    alpha = (2 * depth) ** 0.25
    bias_tables = _dil_bias_tables(rel_bias)
    w_pad = jnp.pad(jnp.swapaxes(w_in, 1, 2), ((0, 0), (0, LANES - N_HEADS), (0, 0))).astype(BF16)
    x2d = x.reshape(b * s, d)
    for layer in range(depth):
        main, dl4, dl16, aug, out_d, side = _in_proj(x2d, w_pad[layer], f_bias[layer], conv_w[layer], b, s)
        main3 = main.reshape(b, s, main.shape[1])
        out_a, out_c = _causal_attn(main3, aug.reshape(b, s, aug.shape[1]),
                                    side.reshape((b, s // CAUSAL_BQ) + side.shape[1:]))
        out_b = _dil_attn(main3, dl4, dl16, bias_tables)
        outs = [o.reshape(b * s, GROUP) for o in (out_a, out_b, out_c)] + [out_d]
        x2d = _dense(x2d, outs, w_out[layer], ln1_g[layer], ln1_b[layer], w_gate[layer], w_up[layer],
                     w_down[layer], ln2_g[layer], ln2_b[layer], alpha)
    return x2d.reshape(b, s, d)
```

```python
import functools
import math

import jax
import jax.numpy as jnp
import numpy as np
from jax import lax
from jax.experimental import pallas as pl
from jax.experimental.pallas import tpu as pltpu

HEAD_DIM = 64
N_HEADS = 4
GROUP = N_HEADS * HEAD_DIM
LANES = 128
N_PAIRS = GROUP // LANES
CONV_K = 3
DIL_PATTERNS = ((128, 1), (512, 4), (2048, 16))
DIL_BLOCK = 128
DIL_GROUP0 = 6
REL_BUCKETS = 32
REL_MAX_DIST = 2048
LN_EPS = 1e-5
SCALE = HEAD_DIM ** -0.5
NEG = -1e30
SIGN_BIT = np.uint32(0x80000000)
VMEM_LIMIT = 56 * 1024 * 1024

BF16 = jnp.bfloat16
F32 = jnp.float32

PROJ_TM = 512
DENSE_TM = 512
FFN_CHUNK = 256
CAUSAL_BQ = 256
SB_SUB = 128
SB_DEAD = 120.0
FOX_Q_COL = 6 * GROUP
FOX_K_COL = 7 * GROUP
FOX_DEAD = 120.0
FOX_NORM_SLACK = 1.02
DIL_GEN_GROUP = 15
DIL_MID_GROUP = 4
DIL_RES_GROUP = 8
COMB_ROWS = 256


def _dot(a, b):
    return jnp.dot(a, b, preferred_element_type=F32)


def _dot_nt(a, b):
    return lax.dot_general(a, b, (((1,), (1,)), ((), ())), preferred_element_type=F32)


def _resident(shape):
    nd = len(shape)
    return pl.BlockSpec(shape, lambda *_: (0,) * nd, pipeline_mode=pl.Buffered(1))


def _params(n_axes):
    return pltpu.CompilerParams(dimension_semantics=("arbitrary",) * n_axes,
                                vmem_limit_bytes=VMEM_LIMIT)


def _split_heads_rows(q, lane):
    zero = jnp.zeros_like(q)
    return jnp.concatenate([jnp.where(lane < HEAD_DIM, q, zero),
                            jnp.where(lane >= HEAD_DIM, q, zero)], axis=0)


def _merge_heads_rows(a, lane, rows):
    return jnp.where(lane < HEAD_DIM, a[:rows], a[rows:])


def _pair_lanes(p):
    return slice(p * LANES, (p + 1) * LANES)


def _group_spec(seq, g):
    return pl.BlockSpec((None, seq, GROUP), lambda bi: (bi, 0, g))


def _in_proj_kernel(x_ref, w_ref, fb_ref, cw_ref, sel_ref,
                    main_ref, dl4_ref, dl16_ref, aug_ref, od_ref, side_ref,
                    carry_ref, ubuf_ref, utail_ref, *stage_refs,
                    tiles_per_seq):
    tm = x_ref.shape[0]
    first = (pl.program_id(0) % tiles_per_seq) == 0
    xb = x_ref[...].astype(BF16)

    n_main = main_ref.shape[1]
    cv0 = n_main
    gate0 = n_main + 3 * GROUP
    cvb = _dot_nt(xb, w_ref[cv0:cv0 + GROUP, :])
    cvc = _dot_nt(xb, w_ref[cv0 + GROUP:cv0 + 2 * GROUP, :])
    cvh = _dot_nt(xb, w_ref[cv0 + 2 * GROUP:cv0 + 3 * GROUP, :])
    u = cvc * cvh

    ubuf_ref[0:8, :] = jnp.where(first, 0.0, utail_ref[...])
    ubuf_ref[8:8 + tm, :] = u
    y = (cw_ref[0:1, :] * ubuf_ref[6:6 + tm, :] + cw_ref[1:2, :] * ubuf_ref[7:7 + tm, :]
         + cw_ref[2:3, :] * u)
    od_ref[...] = (cvb * y).astype(BF16)
    utail_ref[...] = u[tm - 8:tm, :]

    dl_col0 = DIL_GROUP0 * LANES
    row_sq = {}
    for c0 in range(0, n_main, GROUP):
        r = _dot_nt(xb, w_ref[c0:c0 + GROUP, :])
        main_ref[:, c0:c0 + GROUP] = r.astype(BF16)
        if c0 in (FOX_Q_COL, FOX_K_COL):
            row_sq[c0] = jnp.sum(r * r, axis=1, keepdims=True)
        if dl_col0 <= c0 < dl_col0 + 3 * GROUP:
            j = (c0 - dl_col0) // LANES
            stage_refs[j][...] = r[:, :LANES]
            stage_refs[j + 1][...] = r[:, LANES:]

    for dil, ref in ((DIL_PATTERNS[1][1], dl4_ref), (DIL_PATTERNS[2][1], dl16_ref)):
        rows = tm // dil
        for c in range(dil):
            for j in range(len(stage_refs)):
                ref[c, :, j * LANES:(j + 1) * LANES] = (
                    stage_refs[j][pl.ds(c, rows, stride=dil), :].astype(BF16))

    g = _dot_nt(xb, w_ref[gate0:gate0 + LANES, :]) + fb_ref[...]
    logf = jnp.minimum(g, 0.0) - jnp.log(1.0 + jnp.exp(-jnp.abs(g)))
    r_i = lax.broadcasted_iota(jnp.int32, (tm, tm), 0)
    c_i = lax.broadcasted_iota(jnp.int32, (tm, tm), 1)
    tri = jnp.where(c_i <= r_i, 1.0, 0.0).astype(BF16)
    l_hi = logf.astype(BF16)
    l_lo = (logf - l_hi.astype(F32)).astype(BF16)
    both = _dot(tri, jnp.concatenate([l_hi, l_lo], axis=1))
    csum = both[:, :LANES] + both[:, LANES:]
    prev = jnp.where(first, 0.0, carry_ref[...])
    csum = csum + prev
    carry_ref[...] = csum[tm - 1:tm, :]
    for b in range(tm // CAUSAL_BQ):
        rows = slice(b * CAUSAL_BQ, (b + 1) * CAUSAL_BQ)
        for j, c0 in enumerate((FOX_Q_COL, FOX_K_COL)):
            side_ref[b, j:j + 1, :] = jnp.broadcast_to(jnp.max(row_sq[c0][rows], axis=0, keepdims=True), (1, LANES))
        side_ref[b, 2:3, :] = csum[b * CAUSAL_BQ:b * CAUSAL_BQ + 1, :]
        side_ref[b, 3:4, :] = csum[(b + 1) * CAUSAL_BQ - 1:(b + 1) * CAUSAL_BQ, :]
        side_ref[b, 4:8, :] = jnp.zeros((4, LANES), F32)
    c_hi = csum.astype(BF16)
    c_r = csum - c_hi.astype(F32)
    c_mid = c_r.astype(BF16)
    c_lo = (c_r - c_mid.astype(F32)).astype(BF16)
    ones = jnp.ones((tm, LANES), BF16)
    aug_ref[...] = _dot(jnp.concatenate([c_hi, c_mid, c_lo, ones], axis=1), sel_ref[...]).astype(BF16)


def _aug_selection():
    sel = np.zeros((4 * LANES, 4 * LANES), np.float32)
    ones_row = 3 * LANES
    for h in range(N_HEADS):
        bq = (h // 2) * LANES + (h % 2) * HEAD_DIM
        bk = 2 * LANES + bq
        for part in range(3):
            sel[part * LANES + h, bq + part] = 1.0
            sel[ones_row, bq + 3 + part] = 1.0
            sel[ones_row, bk + part] = 1.0
            sel[part * LANES + h, bk + 3 + part] = -1.0
    return sel


def _in_proj(x2d, w_pad, f_bias, conv_w, batch, seq):
    m, d = x2d.shape
    tm = PROJ_TM
    tps = seq // tm
    n_main = 9 * GROUP
    d4, d16 = DIL_PATTERNS[1][1], DIL_PATTERNS[2][1]
    fb = jnp.pad(f_bias, (0, LANES - N_HEADS))[None, :]
    cw = jnp.pad(conv_w, ((0, 8 - CONV_K), (0, 0)))
    sel = jnp.asarray(_aug_selection(), BF16)
    row = lambda i: (i, 0)
    res = lambda i: (i // tps, 0, i % tps, 0)
    return pl.pallas_call(
        functools.partial(_in_proj_kernel, tiles_per_seq=tps),
        grid=(m // tm,),
        in_specs=[pl.BlockSpec((tm, d), row), _resident(w_pad.shape),
                  _resident(fb.shape), _resident(cw.shape), _resident(sel.shape)],
        out_specs=[pl.BlockSpec((tm, n_main), row),
                   pl.BlockSpec((None, d4, tm // d4, 3 * GROUP), res),
                   pl.BlockSpec((None, d16, tm // d16, 3 * GROUP), res),
                   pl.BlockSpec((tm, 4 * LANES), row),
                   pl.BlockSpec((tm, GROUP), row),
                   pl.BlockSpec((tm // CAUSAL_BQ, 8, LANES), lambda i: (i, 0, 0))],
        out_shape=[jax.ShapeDtypeStruct((m, n_main), BF16),
                   jax.ShapeDtypeStruct((batch, d4, seq // d4, 3 * GROUP), BF16),
                   jax.ShapeDtypeStruct((batch, d16, seq // d16, 3 * GROUP), BF16),
                   jax.ShapeDtypeStruct((m, 4 * LANES), BF16),
                   jax.ShapeDtypeStruct((m, GROUP), BF16),
                   jax.ShapeDtypeStruct((m // CAUSAL_BQ, 8, LANES), F32)],
        scratch_shapes=[pltpu.VMEM((1, LANES), F32), pltpu.VMEM((tm + 8, GROUP), F32),
                        pltpu.VMEM((8, GROUP), F32)]
        + [pltpu.VMEM((tm, LANES), F32)] * (3 * N_PAIRS),
        compiler_params=_params(1),
        name="in_proj",
    )(x2d, w_pad, fb, cw, sel)


def _sb_ops(i, qstart, q_ref, k_ref, v_ref, t_ref, o_ref, acc_ref, carry_ref):
    bq, sub = CAUSAL_BQ, SB_SUB
    lane_q = lax.broadcasted_iota(jnp.int32, (bq, LANES), 1)
    qss = []
    for p in range(N_PAIRS):
        qss.append(_split_heads_rows(q_ref[pl.ds(qstart, bq), _pair_lanes(p)], lane_q) * SCALE)
        acc_ref[p] = jnp.zeros((2 * bq, LANES), F32)
        carry_ref[p] = jnp.zeros((2 * bq, sub), F32)

    def scores(p, start, g):
        return _dot_nt(qss[p], k_ref[pl.ds(start, g * sub), _pair_lanes(p)])

    def group(p, start, g, masked, z):
        n = g * sub
        neg_abs = pltpu.bitcast(pltpu.bitcast(z, jnp.uint32) | SIGN_BIT, F32)
        sp = jnp.log(1.0 + jnp.exp(neg_abs))
        log_beta = jnp.minimum(z, 0.0) - sp
        log_rest = log_beta - z
        if masked:
            row = lax.broadcasted_iota(jnp.int32, (2 * bq, n), 0)
            col = lax.broadcasted_iota(jnp.int32, (2 * bq, n), 1)
            strict = (start + col) < (qstart + jnp.where(row >= bq, row - bq, row))
            log_rest = jnp.where(strict, log_rest, 0.0)
        hi = log_rest.astype(BF16)
        lo = (log_rest - hi.astype(F32)).astype(BF16)
        carry = carry_ref[p]
        parts = [None] * g
        for j in reversed(range(g)):
            sl = slice(j * sub, (j + 1) * sub)
            cs = _dot(jnp.concatenate([hi[:, sl], lo[:, sl]], axis=1), t_ref[...])
            a = jnp.exp(log_beta[:, sl] + (cs[:, :sub] + carry))
            carry = carry + cs[:, sub:]
            if masked:
                a = jnp.where(strict[:, sl], a, 0.0)
            parts[j] = a.astype(BF16)
        carry_ref[p] = carry
        acc_ref[p] += _dot(jnp.concatenate(parts, axis=1), v_ref[pl.ds(start, n), _pair_lanes(p)])

    def groups(start, g, masked):
        for p in range(N_PAIRS):
            group(p, start, g, masked, scores(p, start, g))

    def top():
        return functools.reduce(jnp.maximum, [jnp.max(carry_ref[p]) for p in range(N_PAIRS)])

    def finish():
        for p in range(N_PAIRS):
            o_ref[pl.ds(qstart, bq), _pair_lanes(p)] = _merge_heads_rows(acc_ref[p], lane_q, bq).astype(BF16)

    return groups, top, finish


def _fox_ops(i, qstart, q_ref, aq_ref, k_ref, ak_ref, v_ref, side_ref, o_ref, m_ref, l_ref, acc_ref):
    bq = CAUSAL_BQ
    lane_q = lax.broadcasted_iota(jnp.int32, (bq, LANES), 1)

    def wide(a):
        return jnp.broadcast_to(a, (2 * bq, LANES))

    qss = [jnp.concatenate(
        [_split_heads_rows(q_ref[pl.ds(qstart, bq), _pair_lanes(p)], lane_q) * SCALE,
         _split_heads_rows(aq_ref[pl.ds(qstart, bq), _pair_lanes(p)], lane_q)], axis=1)
        for p in range(N_PAIRS)]

    def scores(p, start, n):
        rows = pl.ds(start, n)
        kk = jnp.concatenate([k_ref[rows, _pair_lanes(p)], ak_ref[rows, _pair_lanes(p)]], axis=1)
        return _dot_nt(qss[p], kk)

    def values(p, start, n):
        return v_ref[pl.ds(start, n), _pair_lanes(p)]

    def diag():
        row = lax.broadcasted_iota(jnp.int32, (2 * bq, bq), 0)
        col = lax.broadcasted_iota(jnp.int32, (2 * bq, bq), 1)
        causal = col <= jnp.where(row >= bq, row - bq, row)
        for p in range(N_PAIRS):
            z = scores(p, qstart, bq)
            vs = values(p, qstart, bq)
            z = jnp.where(causal, z, NEG)
            m0 = jnp.max(z, axis=1, keepdims=True)
            pr = jnp.exp(z - m0)
            m_ref[p] = wide(m0)
            l_ref[p] = wide(jnp.sum(pr, axis=1, keepdims=True))
            acc_ref[p] = _dot(pr.astype(BF16), vs)

    def step(start, n):
        for p in range(N_PAIRS):
            z = scores(p, start, n)
            vs = values(p, start, n)
            m_old = m_ref[p]
            m_new = jnp.maximum(m_old, jnp.max(z, axis=1, keepdims=True))
            alpha = jnp.exp(m_old - m_new)
            pr = jnp.exp(z - jnp.concatenate([m_new] * (n // LANES), axis=1))
            l_ref[p] = alpha * l_ref[p] + jnp.sum(pr, axis=1, keepdims=True)
            acc_ref[p] = alpha * acc_ref[p] + _dot(pr.astype(BF16), vs)
            m_ref[p] = m_new

    def margin(end_blk, k_norm_sq):
        q_side = side_ref[i]
        k_side = side_ref[end_blk]
        qk = (0.5 * SCALE * FOX_NORM_SLACK) * (q_side[0:1, :] + k_norm_sq)
        lane = lax.broadcasted_iota(jnp.int32, (1, LANES), 1)
        top = jnp.max(jnp.where(lane < N_HEADS, qk + (q_side[2:3, :] - k_side[3:4, :]), NEG))
        m_min = functools.reduce(jnp.minimum, [jnp.min(m_ref[p]) for p in range(N_PAIRS)])
        return top - m_min

    def finish():
        for p in range(N_PAIRS):
            o_ref[pl.ds(qstart, bq), _pair_lanes(p)] = _merge_heads_rows(
                acc_ref[p] / l_ref[p], lane_q, bq).astype(BF16)

    return diag, step, margin, finish


def _causal_kernel(sq_ref, sk_ref, sv_ref, t_ref, fq_ref, faq_ref, fk_ref, fak_ref, fv_ref, side_ref,
                   so_ref, fo_ref, s_acc, s_carry, s_live, f_m, f_l, f_acc, f_live, *, seq):
    bq = CAUSAL_BQ
    n_q = seq // bq
    k_norm_sq = functools.reduce(jnp.maximum, [side_ref[b, 1:2, :] for b in range(n_q)])

    def q_pair(j, outer):
        blocks = []
        for slot in range(2):
            i = 2 * j + slot
            qstart = pl.multiple_of(i * bq, bq)
            sb = _sb_ops(i, qstart, sq_ref, sk_ref, sv_ref, t_ref, so_ref, s_acc.at[slot], s_carry.at[slot])
            fox = _fox_ops(i, qstart, fq_ref, faq_ref, fk_ref, fak_ref, fv_ref, side_ref, fo_ref,
                           f_m.at[slot], f_l.at[slot], f_acc.at[slot])
            blocks.append((qstart, sb, fox))
        (q_even, sb_even, fox_even), (q_odd, sb_odd, fox_odd) = blocks

        for qstart, (sb_groups, _, _), _ in blocks:
            sb_groups(qstart, bq // SB_SUB, True)
        for _, _, (fox_diag, _, _, _) in blocks:
            fox_diag()
        sb_odd[0](q_even, bq // SB_SUB, False)
        fox_odd[1](q_even, bq)
        s_live[0] = 1
        s_live[1] = jnp.where(sb_odd[1]() >= -SB_DEAD, 1, 0)
        f_live[0] = 1

        def cond(state):
            t, sb_on, fox_on = state
            return (t < j) & ((sb_on == 1) | (fox_on == 1))

        def body(state):
            t = state[0]
            first_blk = 2 * (j - 1 - t)
            start = pl.multiple_of(first_blk * bq, 2 * bq)

            @pl.when(f_live[0] == 1)
            def _():
                for _, _, (_, fox_step, _, _) in blocks:
                    fox_step(start, 2 * bq)
                nxt = jnp.maximum(first_blk - 1, 0)
                worst = jnp.maximum(fox_even[2](nxt, k_norm_sq), fox_odd[2](nxt, k_norm_sq))
                f_live[0] = jnp.where(worst >= -FOX_DEAD, 1, 0)

            for half in (1, 0):
                for slot, (_, (sb_groups, sb_top, _), _) in enumerate(blocks):
                    @pl.when(s_live[slot] == 1)
                    def _(half=half, slot=slot, sb_groups=sb_groups, sb_top=sb_top):
                        sb_groups(pl.multiple_of(start + half * bq, bq), bq // SB_SUB, False)
                        s_live[slot] = jnp.where(sb_top() >= -SB_DEAD, 1, 0)
            return t + 1, s_live[0] | s_live[1], f_live[0]

        lax.while_loop(cond, body, (jnp.int32(0), s_live[0] | s_live[1], f_live[0]))
        for _, (_, _, sb_finish), (_, _, _, fox_finish) in blocks:
            sb_finish()
            fox_finish()
        return outer

    lax.fori_loop(0, n_q // 2, q_pair, 0)


def _sb_tail_matrix():
    sub = SB_SUB
    j = np.arange(2 * sub)[:, None] % sub
    s = np.arange(2 * sub)[None, :]
    return np.where(s < sub, j > s, True).astype(np.float32)


def _causal_attn(main3, aug3, side4):
    b, s, _ = main3.shape
    tmat = jnp.asarray(_sb_tail_matrix(), BF16)
    state = pltpu.VMEM((2, N_PAIRS, 2 * CAUSAL_BQ, LANES), F32)
    flag = pltpu.SMEM((2,), jnp.int32)
    out = jax.ShapeDtypeStruct((b, s, GROUP), BF16)
    return pl.pallas_call(
        functools.partial(_causal_kernel, seq=s),
        grid=(b,),
        in_specs=[_group_spec(s, 0), _group_spec(s, 1), _group_spec(s, 2), _resident(tmat.shape),
                  _group_spec(s, 6), _group_spec(s, 0), _group_spec(s, 7), _group_spec(s, 1), _group_spec(s, 8),
                  pl.BlockSpec((None,) + side4.shape[1:], lambda bi: (bi, 0, 0, 0))],
        out_specs=[_group_spec(s, 0), _group_spec(s, 0)],
        out_shape=[out, out],
        scratch_shapes=[state, pltpu.VMEM((2, N_PAIRS, 2 * CAUSAL_BQ, SB_SUB), F32), flag,
                        state, state, state, flag],
        compiler_params=_params(1),
        name="causal_attn",
    )(main3, main3, main3, tmat, main3, aug3, main3, aug3, main3, side4)


def _dil_kernel(*refs, seq):
    n_pat = len(DIL_PATTERNS)
    qkv = refs[:3 * n_pat]
    bias_ref = refs[3 * n_pat]
    o_ref = refs[3 * n_pat + 1]
    scratch = refs[3 * n_pat + 2:]
    n_state = n_pat * N_PAIRS
    u_refs, m_refs, l_refs = scratch[0:n_state], scratch[n_state:2 * n_state], scratch[2 * n_state:3 * n_state]
    blk = DIL_BLOCK
    lane = lax.broadcasted_iota(jnp.int32, (blk, LANES), 1)

    def block(p, pidx, dil, c, n, first):
        q_ref, k_ref, v_ref = qkv[3 * pidx:3 * pidx + 3]
        lanes = _pair_lanes(p)

        def rd(ref, start, size):
            return ref[pl.ds(start, size), lanes] if dil == 1 else ref[c, pl.ds(start, size), lanes]

        qstart = n * blk if isinstance(n, int) else pl.multiple_of(n * blk, blk)
        qs = _split_heads_rows(rd(q_ref, qstart, blk), lane) * SCALE
        h0, h1 = 2 * p, 2 * p + 1
        if first:
            keys, vals = rd(k_ref, 0, blk), rd(v_ref, 0, blk)
            bias = jnp.concatenate([bias_ref[pidx, h0, :, blk:], bias_ref[pidx, h1, :, blk:]], axis=0)
        else:
            kstart = (n - 1) * blk if isinstance(n, int) else pl.multiple_of((n - 1) * blk, blk)
            keys, vals = rd(k_ref, kstart, 2 * blk), rd(v_ref, kstart, 2 * blk)
            bias = jnp.concatenate([bias_ref[pidx, h0], bias_ref[pidx, h1]], axis=0)
        z = _dot_nt(qs, keys) + bias
        m = jnp.max(z, axis=1, keepdims=True)
        pr = jnp.exp(z - m)
        wide = lambda a: jnp.broadcast_to(a, (2 * blk, LANES))
        if (seq // dil) // blk == 1:
            den = wide(jnp.sum(pr, axis=1, keepdims=True))
            u = _dot(pr.astype(BF16), vals)
        else:
            ones = jnp.ones((vals.shape[0], LANES), BF16)
            u = _dot(pr.astype(BF16), jnp.concatenate([vals, ones], axis=1))
            u, den = u[:, :LANES], u[:, LANES:]
        idx = pl.ds(n * (blk * dil) + c, blk, stride=dil) if dil > 1 else pl.ds(qstart, blk)
        st = pidx * N_PAIRS + p
        u_refs[st][idx, :] = _merge_heads_rows(u, lane, blk)
        m_refs[st][idx, :] = _merge_heads_rows(wide(m), lane, blk)
        l_refs[st][idx, :] = _merge_heads_rows(den, lane, blk)

    def blocks(pidx, dil, c, n, first):
        for p in range(N_PAIRS):
            block(p, pidx, dil, c, n, first)

    for pidx, (_, dil) in enumerate(DIL_PATTERNS):
        n_blk = (seq // dil) // blk
        if dil == 1:
            blocks(pidx, dil, 0, 0, True)

            def body(g, carry, pidx=pidx, dil=dil):
                for jj in range(DIL_GEN_GROUP):
                    blocks(pidx, dil, 0, 1 + g * DIL_GEN_GROUP + jj, False)
                return carry
            lax.fori_loop(0, (n_blk - 1) // DIL_GEN_GROUP, body, 0)
        elif n_blk > 1:
            def body(g, carry, pidx=pidx, dil=dil, n_blk=n_blk):
                for jj in range(DIL_MID_GROUP):
                    c = g * DIL_MID_GROUP + jj
                    blocks(pidx, dil, c, 0, True)
                    for n in range(1, n_blk):
                        blocks(pidx, dil, c, n, False)
                return carry
            lax.fori_loop(0, dil // DIL_MID_GROUP, body, 0)
        else:
            def body(g, carry, pidx=pidx, dil=dil):
                for jj in range(DIL_RES_GROUP):
                    blocks(pidx, dil, g * DIL_RES_GROUP + jj, 0, True)
                return carry
            lax.fori_loop(0, dil // DIL_RES_GROUP, body, 0)

    def comb(j, carry):
        sl = pl.ds(pl.multiple_of(j * COMB_ROWS, COMB_ROWS), COMB_ROWS)
        for p in range(N_PAIRS):
            sts = [pidx * N_PAIRS + p for pidx in range(n_pat)]
            ms = [m_refs[st][sl, :] for st in sts]
            m_all = functools.reduce(jnp.maximum, ms)
            es = [jnp.exp(mp - m_all) for mp in ms]
            num = sum(e * u_refs[st][sl, :] for e, st in zip(es, sts))
            den = sum(e * l_refs[st][sl, :] for e, st in zip(es, sts))
            o_ref[sl, _pair_lanes(p)] = (num / den).astype(BF16)
        return carry
    lax.fori_loop(0, seq // COMB_ROWS, comb, 0)


def _t5_bucket_of(dist):
    max_exact = REL_BUCKETS // 2
    nf = jnp.maximum(dist, 1).astype(jnp.float32)
    large = max_exact + (jnp.log(nf / max_exact) / math.log(REL_MAX_DIST / max_exact)
                         * (REL_BUCKETS - max_exact)).astype(jnp.int32)
    large = jnp.minimum(large, REL_BUCKETS - 1)
    return jnp.where(dist < max_exact, dist, large)


def _dil_bias_tables(rel_bias):
    blk = DIL_BLOCK
    qi = jnp.arange(blk)[:, None]
    kj = jnp.arange(2 * blk)[None, :]
    sub = qi + blk - kj
    tables = []
    for window, dil in DIL_PATTERNS:
        in_band = (sub >= 0) & (sub <= window // dil)
        bucket = _t5_bucket_of(jnp.maximum(sub, 0) * dil)
        hit = bucket[None, :, :, None] == jnp.arange(REL_BUCKETS)
        bias = jnp.sum(jnp.where(hit, rel_bias.T.astype(F32)[:, None, None, :], 0.0), axis=-1)
        tables.append(jnp.where(in_band[None], bias, NEG))
    return jnp.stack(tables)


def _dil_attn(main3, dl4, dl16, bias_tables):
    b, s, _ = main3.shape
    n_pat = len(DIL_PATTERNS)
    operands, in_specs = [], []
    for role in range(3):
        operands.append(main3)
        in_specs.append(_group_spec(s, DIL_GROUP0 // N_PAIRS + role))
    for arr in (dl4, dl16):
        dil, rows = arr.shape[1], arr.shape[2]
        for role in range(3):
            operands.append(arr)
            in_specs.append(pl.BlockSpec((None, dil, rows, GROUP), lambda bi, role=role: (bi, 0, 0, role)))
    operands.append(bias_tables)
    in_specs.append(_resident(bias_tables.shape))
    return pl.pallas_call(
        functools.partial(_dil_kernel, seq=s),
        grid=(b,),
        in_specs=in_specs,
        out_specs=_group_spec(s, 0),
        out_shape=jax.ShapeDtypeStruct((b, s, GROUP), BF16),
        scratch_shapes=[pltpu.VMEM((s, LANES), F32)] * (3 * n_pat * N_PAIRS),
        compiler_params=_params(1),
        name="dil_attn",
    )(*operands)


def _layer_norm(v, g, b):
    mu = jnp.mean(v, axis=-1, keepdims=True)
    d = v - mu
    var = jnp.mean(d * d, axis=-1, keepdims=True)
    return d * lax.rsqrt(var + LN_EPS) * g + b


def _dense_kernel(x_ref, oa_ref, ob_ref, oc_ref, od_ref, wo_ref, g1_ref, b1_ref,
                  wg_ref, wu_ref, wd_ref, g2_ref, b2_ref, out_ref, h_ref, *, alpha):
    mix = (_dot(oa_ref[...], wo_ref[0:GROUP, :]) + _dot(ob_ref[...], wo_ref[GROUP:2 * GROUP, :])
           + _dot(oc_ref[...], wo_ref[2 * GROUP:3 * GROUP, :]) + _dot(od_ref[...], wo_ref[3 * GROUP:4 * GROUP, :]))
    x1 = _layer_norm(alpha * x_ref[...] + mix, g1_ref[...], b1_ref[...])
    xb = x1.astype(BF16)
    d_ff = wg_ref.shape[1]
    for c0 in range(0, d_ff, FFN_CHUNK):
        g = _dot(xb, wg_ref[:, c0:c0 + FFN_CHUNK])
        u = _dot(xb, wu_ref[:, c0:c0 + FFN_CHUNK])
        h_ref[:, c0:c0 + FFN_CHUNK] = (g * (1.0 / (1.0 + jnp.exp(-g))) * u).astype(BF16)
    y = _dot(h_ref[...], wd_ref[...])
    out_ref[...] = _layer_norm(alpha * x1 + y, g2_ref[...], b2_ref[...])


def _dense(x2d, outs, w_out, g1, b1, w_gate, w_up, w_down, g2, b2, alpha):
    m, d = x2d.shape
    tm = DENSE_TM
    d_ff = w_gate.shape[1]
    row = lambda i: (i, 0)
    vec = lambda a: a[None, :]
    o_spec = pl.BlockSpec((tm, GROUP), row)
    return pl.pallas_call(
        functools.partial(_dense_kernel, alpha=alpha),
        grid=(m // tm,),
        in_specs=[pl.BlockSpec((tm, d), row), o_spec, o_spec, o_spec, o_spec,
                  _resident((d, d)), _resident((1, d)), _resident((1, d)),
                  _resident((d, d_ff)), _resident((d, d_ff)), _resident((d_ff, d)),
                  _resident((1, d)), _resident((1, d))],
        out_specs=pl.BlockSpec((tm, d), row),
        out_shape=jax.ShapeDtypeStruct((m, d), F32),
        scratch_shapes=[pltpu.VMEM((tm, d_ff), BF16)],
        compiler_params=_params(1),
        name="dense",
    )(x2d, *outs, w_out.astype(BF16), vec(g1), vec(b1), w_gate.astype(BF16), w_up.astype(BF16),
      w_down.astype(BF16), vec(g2), vec(b2))


def kernel(x, w_in, f_bias, conv_w, w_out, rel_bias, ln1_g, ln1_b, w_gate, w_up, w_down, ln2_g, ln2_b):
    b, s, d = x.shape
    depth = w_in.shape[0]
    assert d == 4 * GROUP and w_in.shape[2] == 12 * GROUP + N_HEADS
    assert s % PROJ_TM == 0 and s % (2 * CAUSAL_BQ) == 0 and s % COMB_ROWS == 0
    assert [dil for _, dil in DIL_PATTERNS][0] == 1
    for _, dil in DIL_PATTERNS:
        n_blk = (s // dil) // DIL_BLOCK
        assert n_blk * DIL_BLOCK * dil == s and PROJ_TM % (16 * dil) == 0
        assert ((n_blk - 1) % DIL_GEN_GROUP == 0 if dil == 1 else
                dil % (DIL_MID_GROUP if n_blk > 1 else DIL_RES_GROUP) == 0)
    alpha = (2 * depth) ** 0.25
    bias_tables = _dil_bias_tables(rel_bias)
    w_pad = jnp.pad(jnp.swapaxes(w_in, 1, 2), ((0, 0), (0, LANES - N_HEADS), (0, 0))).astype(BF16)
    x2d = x.reshape(b * s, d)
    for layer in range(depth):
        main, dl4, dl16, aug, out_d, side = _in_proj(x2d, w_pad[layer], f_bias[layer], conv_w[layer], b, s)
        main3 = main.reshape(b, s, main.shape[1])
        out_a, out_c = _causal_attn(main3, aug.reshape(b, s, aug.shape[1]),
                                    side.reshape((b, s // CAUSAL_BQ) + side.shape[1:]))
        out_b = _dil_attn(main3, dl4, dl16, bias_tables)
        outs = [o.reshape(b * s, GROUP) for o in (out_a, out_b, out_c)] + [out_d]
        x2d = _dense(x2d, outs, w_out[layer], ln1_g[layer], ln1_b[layer], w_gate[layer], w_up[layer],
                     w_down[layer], ln2_g[layer], ln2_b[layer], alpha)
    return x2d.reshape(b, s, d)
```

```python
import functools
import math

import jax
import jax.numpy as jnp
import numpy as np
from jax import lax
from jax.experimental import pallas as pl
from jax.experimental.pallas import tpu as pltpu

HEAD_DIM = 64
N_HEADS = 4
GROUP = N_HEADS * HEAD_DIM
LANES = 128
N_PAIRS = GROUP // LANES
CONV_K = 3
DIL_PATTERNS = ((128, 1), (512, 4), (2048, 16))
DIL_BLOCK = 128
DIL_GROUP0 = 6
REL_BUCKETS = 32
REL_MAX_DIST = 2048
LN_EPS = 1e-5
SCALE = HEAD_DIM ** -0.5
NEG = -1e30
SIGN_BIT = np.uint32(0x80000000)
VMEM_LIMIT = 56 * 1024 * 1024

BF16 = jnp.bfloat16
F32 = jnp.float32

PROJ_TM = 512
DENSE_TM = 1024
DENSE_SPLIT = 4
FFN_CHUNK = 256
CAUSAL_BQ = 256
SB_SUB = 128
SB_DEAD = 120.0
FOX_Q_COL = 6 * GROUP
FOX_K_COL = 7 * GROUP
FOX_DEAD = 120.0
FOX_NORM_SLACK = 1.02
DIL_GEN_GROUP = 15
DIL_MID_GROUP = 4
DIL_RES_GROUP = 8
COMB_ROWS = 256


def _dot(a, b):
    return jnp.dot(a, b, preferred_element_type=F32)


def _dot_nt(a, b):
    return lax.dot_general(a, b, (((1,), (1,)), ((), ())), preferred_element_type=F32)


def _resident(shape):
    nd = len(shape)
    return pl.BlockSpec(shape, lambda *_: (0,) * nd, pipeline_mode=pl.Buffered(1))


def _params(n_axes):
    return pltpu.CompilerParams(dimension_semantics=("arbitrary",) * n_axes,
                                vmem_limit_bytes=VMEM_LIMIT)


def _split_heads_rows(q, lane):
    zero = jnp.zeros_like(q)
    return jnp.concatenate([jnp.where(lane < HEAD_DIM, q, zero),
                            jnp.where(lane >= HEAD_DIM, q, zero)], axis=0)


def _merge_heads_rows(a, lane, rows):
    return jnp.where(lane < HEAD_DIM, a[:rows], a[rows:])


def _pair_lanes(p):
    return slice(p * LANES, (p + 1) * LANES)


def _group_spec(seq, g):
    return pl.BlockSpec((None, seq, GROUP), lambda bi: (bi, 0, g))


def _in_proj_kernel(x_ref, w_ref, fb_ref, cw_ref, sel_ref,
                    main_ref, dl4_ref, dl16_ref, aug_ref, od_ref, side_ref,
                    carry_ref, ubuf_ref, utail_ref, *stage_refs,
                    tiles_per_seq):
    tm = x_ref.shape[0]
    first = (pl.program_id(0) % tiles_per_seq) == 0
    xb = x_ref[...].astype(BF16)

    n_main = main_ref.shape[1]
    cv0 = n_main
    gate0 = n_main + 3 * GROUP
    cvb = _dot_nt(xb, w_ref[cv0:cv0 + GROUP, :])
    cvc = _dot_nt(xb, w_ref[cv0 + GROUP:cv0 + 2 * GROUP, :])
    cvh = _dot_nt(xb, w_ref[cv0 + 2 * GROUP:cv0 + 3 * GROUP, :])
    u = cvc * cvh

    ubuf_ref[0:8, :] = jnp.where(first, 0.0, utail_ref[...])
    ubuf_ref[8:8 + tm, :] = u
    y = (cw_ref[0:1, :] * ubuf_ref[6:6 + tm, :] + cw_ref[1:2, :] * ubuf_ref[7:7 + tm, :]
         + cw_ref[2:3, :] * u)
    od_ref[...] = (cvb * y).astype(BF16)
    utail_ref[...] = u[tm - 8:tm, :]

    dl_col0 = DIL_GROUP0 * LANES
    row_sq = {}
    for c0 in range(0, n_main, GROUP):
        r = _dot_nt(xb, w_ref[c0:c0 + GROUP, :])
        main_ref[:, c0:c0 + GROUP] = r.astype(BF16)
        if c0 in (FOX_Q_COL, FOX_K_COL):
            row_sq[c0] = jnp.sum(r * r, axis=1, keepdims=True)
        if dl_col0 <= c0 < dl_col0 + 3 * GROUP:
            j = (c0 - dl_col0) // LANES
            stage_refs[j][...] = r[:, :LANES]
            stage_refs[j + 1][...] = r[:, LANES:]

    for dil, ref in ((DIL_PATTERNS[1][1], dl4_ref), (DIL_PATTERNS[2][1], dl16_ref)):
        rows = tm // dil
        for c in range(dil):
            for j in range(len(stage_refs)):
                ref[c, :, j * LANES:(j + 1) * LANES] = (
                    stage_refs[j][pl.ds(c, rows, stride=dil), :].astype(BF16))

    g = _dot_nt(xb, w_ref[gate0:gate0 + LANES, :]) + fb_ref[...]
    logf = jnp.minimum(g, 0.0) - jnp.log(1.0 + jnp.exp(-jnp.abs(g)))
    r_i = lax.broadcasted_iota(jnp.int32, (tm, tm), 0)
    c_i = lax.broadcasted_iota(jnp.int32, (tm, tm), 1)
    tri = jnp.where(c_i <= r_i, 1.0, 0.0).astype(BF16)
    l_hi = logf.astype(BF16)
    l_lo = (logf - l_hi.astype(F32)).astype(BF16)
    both = _dot(tri, jnp.concatenate([l_hi, l_lo], axis=1))
    csum = both[:, :LANES] + both[:, LANES:]
    prev = jnp.where(first, 0.0, carry_ref[...])
    csum = csum + prev
    carry_ref[...] = csum[tm - 1:tm, :]
    for b in range(tm // CAUSAL_BQ):
        rows = slice(b * CAUSAL_BQ, (b + 1) * CAUSAL_BQ)
        for j, c0 in enumerate((FOX_Q_COL, FOX_K_COL)):
            side_ref[b, j:j + 1, :] = jnp.broadcast_to(jnp.max(row_sq[c0][rows], axis=0, keepdims=True), (1, LANES))
        side_ref[b, 2:3, :] = csum[b * CAUSAL_BQ:b * CAUSAL_BQ + 1, :]
        side_ref[b, 3:4, :] = csum[(b + 1) * CAUSAL_BQ - 1:(b + 1) * CAUSAL_BQ, :]
        side_ref[b, 4:8, :] = jnp.zeros((4, LANES), F32)
    c_hi = csum.astype(BF16)
    c_r = csum - c_hi.astype(F32)
    c_mid = c_r.astype(BF16)
    c_lo = (c_r - c_mid.astype(F32)).astype(BF16)
    ones = jnp.ones((tm, LANES), BF16)
    aug_ref[...] = _dot(jnp.concatenate([c_hi, c_mid, c_lo, ones], axis=1), sel_ref[...]).astype(BF16)


def _aug_selection():
    sel = np.zeros((4 * LANES, 4 * LANES), np.float32)
    ones_row = 3 * LANES
    for h in range(N_HEADS):
        bq = (h // 2) * LANES + (h % 2) * HEAD_DIM
        bk = 2 * LANES + bq
        for part in range(3):
            sel[part * LANES + h, bq + part] = 1.0
            sel[ones_row, bq + 3 + part] = 1.0
            sel[ones_row, bk + part] = 1.0
            sel[part * LANES + h, bk + 3 + part] = -1.0
    return sel


def _in_proj(x2d, w_pad, f_bias, conv_w, batch, seq):
    m, d = x2d.shape
    tm = PROJ_TM
    tps = seq // tm
    n_main = 9 * GROUP
    d4, d16 = DIL_PATTERNS[1][1], DIL_PATTERNS[2][1]
    fb = jnp.pad(f_bias, (0, LANES - N_HEADS))[None, :]
    cw = jnp.pad(conv_w, ((0, 8 - CONV_K), (0, 0)))
    sel = jnp.asarray(_aug_selection(), BF16)
    row = lambda i: (i, 0)
    res = lambda i: (i // tps, 0, i % tps, 0)
    return pl.pallas_call(
        functools.partial(_in_proj_kernel, tiles_per_seq=tps),
        grid=(m // tm,),
        in_specs=[pl.BlockSpec((tm, d), row), _resident(w_pad.shape),
                  _resident(fb.shape), _resident(cw.shape), _resident(sel.shape)],
        out_specs=[pl.BlockSpec((tm, n_main), row),
                   pl.BlockSpec((None, d4, tm // d4, 3 * GROUP), res),
                   pl.BlockSpec((None, d16, tm // d16, 3 * GROUP), res),
                   pl.BlockSpec((tm, 4 * LANES), row),
                   pl.BlockSpec((tm, GROUP), row),
                   pl.BlockSpec((tm // CAUSAL_BQ, 8, LANES), lambda i: (i, 0, 0))],
        out_shape=[jax.ShapeDtypeStruct((m, n_main), BF16),
                   jax.ShapeDtypeStruct((batch, d4, seq // d4, 3 * GROUP), BF16),
                   jax.ShapeDtypeStruct((batch, d16, seq // d16, 3 * GROUP), BF16),
                   jax.ShapeDtypeStruct((m, 4 * LANES), BF16),
                   jax.ShapeDtypeStruct((m, GROUP), BF16),
                   jax.ShapeDtypeStruct((m // CAUSAL_BQ, 8, LANES), F32)],
        scratch_shapes=[pltpu.VMEM((1, LANES), F32), pltpu.VMEM((tm + 8, GROUP), F32),
                        pltpu.VMEM((8, GROUP), F32)]
        + [pltpu.VMEM((tm, LANES), F32)] * (3 * N_PAIRS),
        compiler_params=_params(1),
        name="in_proj",
    )(x2d, w_pad, fb, cw, sel)


def _sb_ops(i, qstart, q_ref, k_ref, v_ref, t_ref, o_ref, acc_ref, carry_ref):
    bq, sub = CAUSAL_BQ, SB_SUB
    lane_q = lax.broadcasted_iota(jnp.int32, (bq, LANES), 1)
    qss = []
    for p in range(N_PAIRS):
        qss.append(_split_heads_rows(q_ref[pl.ds(qstart, bq), _pair_lanes(p)], lane_q) * SCALE)
        acc_ref[p] = jnp.zeros((2 * bq, LANES), F32)
        carry_ref[p] = jnp.zeros((2 * bq, sub), F32)

    def scores(p, start, g):
        return _dot_nt(qss[p], k_ref[pl.ds(start, g * sub), _pair_lanes(p)])

    def group(p, start, g, masked, z):
        n = g * sub
        neg_abs = pltpu.bitcast(pltpu.bitcast(z, jnp.uint32) | SIGN_BIT, F32)
        sp = jnp.log(1.0 + jnp.exp(neg_abs))
        log_beta = jnp.minimum(z, 0.0) - sp
        log_rest = log_beta - z
        if masked:
            row = lax.broadcasted_iota(jnp.int32, (2 * bq, n), 0)
            col = lax.broadcasted_iota(jnp.int32, (2 * bq, n), 1)
            strict = (start + col) < (qstart + jnp.where(row >= bq, row - bq, row))
            log_rest = jnp.where(strict, log_rest, 0.0)
        hi = log_rest.astype(BF16)
        lo = (log_rest - hi.astype(F32)).astype(BF16)
        carry = carry_ref[p]
        parts = [None] * g
        for j in reversed(range(g)):
            sl = slice(j * sub, (j + 1) * sub)
            cs = _dot(jnp.concatenate([hi[:, sl], lo[:, sl]], axis=1), t_ref[...])
            a = jnp.exp(log_beta[:, sl] + (cs[:, :sub] + carry))
            carry = carry + cs[:, sub:]
            if masked:
                a = jnp.where(strict[:, sl], a, 0.0)
            parts[j] = a.astype(BF16)
        carry_ref[p] = carry
        acc_ref[p] += _dot(jnp.concatenate(parts, axis=1), v_ref[pl.ds(start, n), _pair_lanes(p)])

    def groups(start, g, masked):
        for p in range(N_PAIRS):
            group(p, start, g, masked, scores(p, start, g))

    def top():
        return functools.reduce(jnp.maximum, [jnp.max(carry_ref[p]) for p in range(N_PAIRS)])

    def finish():
        for p in range(N_PAIRS):
            o_ref[pl.ds(qstart, bq), _pair_lanes(p)] = _merge_heads_rows(acc_ref[p], lane_q, bq).astype(BF16)

    return groups, top, finish


def _fox_ops(i, qstart, q_ref, aq_ref, k_ref, ak_ref, v_ref, side_ref, o_ref, m_ref, l_ref, acc_ref):
    bq = CAUSAL_BQ
    lane_q = lax.broadcasted_iota(jnp.int32, (bq, LANES), 1)

    def wide(a):
        return jnp.broadcast_to(a, (2 * bq, LANES))

    qss = [jnp.concatenate(
        [_split_heads_rows(q_ref[pl.ds(qstart, bq), _pair_lanes(p)], lane_q) * SCALE,
         _split_heads_rows(aq_ref[pl.ds(qstart, bq), _pair_lanes(p)], lane_q)], axis=1)
        for p in range(N_PAIRS)]

    def scores(p, start, n):
        rows = pl.ds(start, n)
        kk = jnp.concatenate([k_ref[rows, _pair_lanes(p)], ak_ref[rows, _pair_lanes(p)]], axis=1)
        return _dot_nt(qss[p], kk)

    def values(p, start, n):
        return v_ref[pl.ds(start, n), _pair_lanes(p)]

    def diag():
        row = lax.broadcasted_iota(jnp.int32, (2 * bq, bq), 0)
        col = lax.broadcasted_iota(jnp.int32, (2 * bq, bq), 1)
        causal = col <= jnp.where(row >= bq, row - bq, row)
        for p in range(N_PAIRS):
            z = scores(p, qstart, bq)
            vs = values(p, qstart, bq)
            z = jnp.where(causal, z, NEG)
            m0 = jnp.max(z, axis=1, keepdims=True)
            pr = jnp.exp(z - m0)
            m_ref[p] = wide(m0)
            l_ref[p] = wide(jnp.sum(pr, axis=1, keepdims=True))
            acc_ref[p] = _dot(pr.astype(BF16), vs)

    def step(start, n):
        for p in range(N_PAIRS):
            z = scores(p, start, n)
            vs = values(p, start, n)
            m_old = m_ref[p]
            m_new = jnp.maximum(m_old, jnp.max(z, axis=1, keepdims=True))
            alpha = jnp.exp(m_old - m_new)
            pr = jnp.exp(z - jnp.concatenate([m_new] * (n // LANES), axis=1))
            l_ref[p] = alpha * l_ref[p] + jnp.sum(pr, axis=1, keepdims=True)
            acc_ref[p] = alpha * acc_ref[p] + _dot(pr.astype(BF16), vs)
            m_ref[p] = m_new

    def margin(end_blk, k_norm_sq):
        q_side = side_ref[i]
        k_side = side_ref[end_blk]
        qk = (0.5 * SCALE * FOX_NORM_SLACK) * (q_side[0:1, :] + k_norm_sq)
        lane = lax.broadcasted_iota(jnp.int32, (1, LANES), 1)
        top = jnp.max(jnp.where(lane < N_HEADS, qk + (q_side[2:3, :] - k_side[3:4, :]), NEG))
        m_min = functools.reduce(jnp.minimum, [jnp.min(m_ref[p]) for p in range(N_PAIRS)])
        return top - m_min

    def finish():
        for p in range(N_PAIRS):
            o_ref[pl.ds(qstart, bq), _pair_lanes(p)] = _merge_heads_rows(
                acc_ref[p] / l_ref[p], lane_q, bq).astype(BF16)

    return diag, step, margin, finish


def _causal_kernel(sq_ref, sk_ref, sv_ref, t_ref, fq_ref, faq_ref, fk_ref, fak_ref, fv_ref, side_ref,
                   so_ref, fo_ref, s_acc, s_carry, s_live, f_m, f_l, f_acc, f_live, *, seq):
    bq = CAUSAL_BQ
    n_q = seq // bq
    k_norm_sq = functools.reduce(jnp.maximum, [side_ref[b, 1:2, :] for b in range(n_q)])

    def q_pair(j, outer):
        blocks = []
        for slot in range(2):
            i = 2 * j + slot
            qstart = pl.multiple_of(i * bq, bq)
            sb = _sb_ops(i, qstart, sq_ref, sk_ref, sv_ref, t_ref, so_ref, s_acc.at[slot], s_carry.at[slot])
            fox = _fox_ops(i, qstart, fq_ref, faq_ref, fk_ref, fak_ref, fv_ref, side_ref, fo_ref,
                           f_m.at[slot], f_l.at[slot], f_acc.at[slot])
            blocks.append((qstart, sb, fox))
        (q_even, sb_even, fox_even), (q_odd, sb_odd, fox_odd) = blocks

        for qstart, (sb_groups, _, _), _ in blocks:
            sb_groups(qstart, bq // SB_SUB, True)
        for _, _, (fox_diag, _, _, _) in blocks:
            fox_diag()
        sb_odd[0](q_even, bq // SB_SUB, False)
        fox_odd[1](q_even, bq)
        s_live[0] = 1
        s_live[1] = jnp.where(sb_odd[1]() >= -SB_DEAD, 1, 0)
        f_live[0] = 1

        def cond(state):
            t, sb_on, fox_on = state
            return (t < j) & ((sb_on == 1) | (fox_on == 1))

        def body(state):
            t = state[0]
            first_blk = 2 * (j - 1 - t)
            start = pl.multiple_of(first_blk * bq, 2 * bq)

            @pl.when(f_live[0] == 1)
            def _():
                for _, _, (_, fox_step, _, _) in blocks:
                    fox_step(start, 2 * bq)
                nxt = jnp.maximum(first_blk - 1, 0)
                worst = jnp.maximum(fox_even[2](nxt, k_norm_sq), fox_odd[2](nxt, k_norm_sq))
                f_live[0] = jnp.where(worst >= -FOX_DEAD, 1, 0)

            for half in (1, 0):
                for slot, (_, (sb_groups, sb_top, _), _) in enumerate(blocks):
                    @pl.when(s_live[slot] == 1)
                    def _(half=half, slot=slot, sb_groups=sb_groups, sb_top=sb_top):
                        sb_groups(pl.multiple_of(start + half * bq, bq), bq // SB_SUB, False)
                        s_live[slot] = jnp.where(sb_top() >= -SB_DEAD, 1, 0)
            return t + 1, s_live[0] | s_live[1], f_live[0]

        lax.while_loop(cond, body, (jnp.int32(0), s_live[0] | s_live[1], f_live[0]))
        for _, (_, _, sb_finish), (_, _, _, fox_finish) in blocks:
            sb_finish()
            fox_finish()
        return outer

    lax.fori_loop(0, n_q // 2, q_pair, 0)


def _sb_tail_matrix():
    sub = SB_SUB
    j = np.arange(2 * sub)[:, None] % sub
    s = np.arange(2 * sub)[None, :]
    return np.where(s < sub, j > s, True).astype(np.float32)


def _causal_attn(main3, aug3, side4):
    b, s, _ = main3.shape
    tmat = jnp.asarray(_sb_tail_matrix(), BF16)
    state = pltpu.VMEM((2, N_PAIRS, 2 * CAUSAL_BQ, LANES), F32)
    flag = pltpu.SMEM((2,), jnp.int32)
    out = jax.ShapeDtypeStruct((b, s, GROUP), BF16)
    return pl.pallas_call(
        functools.partial(_causal_kernel, seq=s),
        grid=(b,),
        in_specs=[_group_spec(s, 0), _group_spec(s, 1), _group_spec(s, 2), _resident(tmat.shape),
                  _group_spec(s, 6), _group_spec(s, 0), _group_spec(s, 7), _group_spec(s, 1), _group_spec(s, 8),
                  pl.BlockSpec((None,) + side4.shape[1:], lambda bi: (bi, 0, 0, 0))],
        out_specs=[_group_spec(s, 0), _group_spec(s, 0)],
        out_shape=[out, out],
        scratch_shapes=[state, pltpu.VMEM((2, N_PAIRS, 2 * CAUSAL_BQ, SB_SUB), F32), flag,
                        state, state, state, flag],
        compiler_params=_params(1),
        name="causal_attn",
    )(main3, main3, main3, tmat, main3, aug3, main3, aug3, main3, side4)


def _dil_kernel(*refs, seq):
    n_pat = len(DIL_PATTERNS)
    qkv = refs[:3 * n_pat]
    bias_ref = refs[3 * n_pat]
    o_ref = refs[3 * n_pat + 1]
    scratch = refs[3 * n_pat + 2:]
    n_state = n_pat * N_PAIRS
    u_refs, m_refs, l_refs = scratch[0:n_state], scratch[n_state:2 * n_state], scratch[2 * n_state:3 * n_state]
    blk = DIL_BLOCK
    lane = lax.broadcasted_iota(jnp.int32, (blk, LANES), 1)

    def block(p, pidx, dil, c, n, first):
        q_ref, k_ref, v_ref = qkv[3 * pidx:3 * pidx + 3]
        lanes = _pair_lanes(p)

        def rd(ref, start, size):
            return ref[pl.ds(start, size), lanes] if dil == 1 else ref[c, pl.ds(start, size), lanes]

        qstart = n * blk if isinstance(n, int) else pl.multiple_of(n * blk, blk)
        qs = _split_heads_rows(rd(q_ref, qstart, blk), lane) * SCALE
        h0, h1 = 2 * p, 2 * p + 1
        if first:
            keys, vals = rd(k_ref, 0, blk), rd(v_ref, 0, blk)
            bias = jnp.concatenate([bias_ref[pidx, h0, :, blk:], bias_ref[pidx, h1, :, blk:]], axis=0)
        else:
            kstart = (n - 1) * blk if isinstance(n, int) else pl.multiple_of((n - 1) * blk, blk)
            keys, vals = rd(k_ref, kstart, 2 * blk), rd(v_ref, kstart, 2 * blk)
            bias = jnp.concatenate([bias_ref[pidx, h0], bias_ref[pidx, h1]], axis=0)
        z = _dot_nt(qs, keys) + bias
        m = jnp.max(z, axis=1, keepdims=True)
        pr = jnp.exp(z - m)
        wide = lambda a: jnp.broadcast_to(a, (2 * blk, LANES))
        if (seq // dil) // blk == 1:
            den = wide(jnp.sum(pr, axis=1, keepdims=True))
            u = _dot(pr.astype(BF16), vals)
        else:
            ones = jnp.ones((vals.shape[0], LANES), BF16)
            u = _dot(pr.astype(BF16), jnp.concatenate([vals, ones], axis=1))
            u, den = u[:, :LANES], u[:, LANES:]
        idx = pl.ds(n * (blk * dil) + c, blk, stride=dil) if dil > 1 else pl.ds(qstart, blk)
        st = pidx * N_PAIRS + p
        u_refs[st][idx, :] = _merge_heads_rows(u, lane, blk)
        m_refs[st][idx, :] = _merge_heads_rows(wide(m), lane, blk)
        l_refs[st][idx, :] = _merge_heads_rows(den, lane, blk)

    def blocks(pidx, dil, c, n, first):
        for p in range(N_PAIRS):
            block(p, pidx, dil, c, n, first)

    for pidx, (_, dil) in enumerate(DIL_PATTERNS):
        n_blk = (seq // dil) // blk
        if dil == 1:
            blocks(pidx, dil, 0, 0, True)

            def body(g, carry, pidx=pidx, dil=dil):
                for jj in range(DIL_GEN_GROUP):
                    blocks(pidx, dil, 0, 1 + g * DIL_GEN_GROUP + jj, False)
                return carry
            lax.fori_loop(0, (n_blk - 1) // DIL_GEN_GROUP, body, 0)
        elif n_blk > 1:
            def body(g, carry, pidx=pidx, dil=dil, n_blk=n_blk):
                for jj in range(DIL_MID_GROUP):
                    c = g * DIL_MID_GROUP + jj
                    blocks(pidx, dil, c, 0, True)
                    for n in range(1, n_blk):
                        blocks(pidx, dil, c, n, False)
                return carry
            lax.fori_loop(0, dil // DIL_MID_GROUP, body, 0)
        else:
            def body(g, carry, pidx=pidx, dil=dil):
                for jj in range(DIL_RES_GROUP):
                    blocks(pidx, dil, g * DIL_RES_GROUP + jj, 0, True)
                return carry
            lax.fori_loop(0, dil // DIL_RES_GROUP, body, 0)

    def comb(j, carry):
        sl = pl.ds(pl.multiple_of(j * COMB_ROWS, COMB_ROWS), COMB_ROWS)
        for p in range(N_PAIRS):
            sts = [pidx * N_PAIRS + p for pidx in range(n_pat)]
            ms = [m_refs[st][sl, :] for st in sts]
            m_all = functools.reduce(jnp.maximum, ms)
            es = [jnp.exp(mp - m_all) for mp in ms]
            num = sum(e * u_refs[st][sl, :] for e, st in zip(es, sts))
            den = sum(e * l_refs[st][sl, :] for e, st in zip(es, sts))
            o_ref[sl, _pair_lanes(p)] = (num / den).astype(BF16)
        return carry
    lax.fori_loop(0, seq // COMB_ROWS, comb, 0)


def _t5_bucket_of(dist):
    max_exact = REL_BUCKETS // 2
    nf = jnp.maximum(dist, 1).astype(jnp.float32)
    large = max_exact + (jnp.log(nf / max_exact) / math.log(REL_MAX_DIST / max_exact)
                         * (REL_BUCKETS - max_exact)).astype(jnp.int32)
    large = jnp.minimum(large, REL_BUCKETS - 1)
    return jnp.where(dist < max_exact, dist, large)


def _dil_bias_tables(rel_bias):
    blk = DIL_BLOCK
    qi = jnp.arange(blk)[:, None]
    kj = jnp.arange(2 * blk)[None, :]
    sub = qi + blk - kj
    tables = []
    for window, dil in DIL_PATTERNS:
        in_band = (sub >= 0) & (sub <= window // dil)
        bucket = _t5_bucket_of(jnp.maximum(sub, 0) * dil)
        hit = bucket[None, :, :, None] == jnp.arange(REL_BUCKETS)
        bias = jnp.sum(jnp.where(hit, rel_bias.T.astype(F32)[:, None, None, :], 0.0), axis=-1)
        tables.append(jnp.where(in_band[None], bias, NEG))
    return jnp.stack(tables)


def _dil_attn(main3, dl4, dl16, bias_tables):
    b, s, _ = main3.shape
    n_pat = len(DIL_PATTERNS)
    operands, in_specs = [], []
    for role in range(3):
        operands.append(main3)
        in_specs.append(_group_spec(s, DIL_GROUP0 // N_PAIRS + role))
    for arr in (dl4, dl16):
        dil, rows = arr.shape[1], arr.shape[2]
        for role in range(3):
            operands.append(arr)
            in_specs.append(pl.BlockSpec((None, dil, rows, GROUP), lambda bi, role=role: (bi, 0, 0, role)))
    operands.append(bias_tables)
    in_specs.append(_resident(bias_tables.shape))
    return pl.pallas_call(
        functools.partial(_dil_kernel, seq=s),
        grid=(b,),
        in_specs=in_specs,
        out_specs=_group_spec(s, 0),
        out_shape=jax.ShapeDtypeStruct((b, s, GROUP), BF16),
        scratch_shapes=[pltpu.VMEM((s, LANES), F32)] * (3 * n_pat * N_PAIRS),
        compiler_params=_params(1),
        name="dil_attn",
    )(*operands)


def _layer_norm(v, g, b):
    mu = jnp.mean(v, axis=-1, keepdims=True)
    d = v - mu
    var = jnp.mean(d * d, axis=-1, keepdims=True)
    return d * lax.rsqrt(var + LN_EPS) * g + b


def _dense_kernel(x_ref, oa_ref, ob_ref, oc_ref, od_ref, wo_ref, g1_ref, b1_ref,
                  wg_ref, wu_ref, wd_ref, g2_ref, b2_ref, out_ref, h_ref, *, alpha):
    tm = x_ref.shape[0]
    d_ff = wg_ref.shape[1]
    halves = [slice(r0, r0 + tm // DENSE_SPLIT) for r0 in range(0, tm, tm // DENSE_SPLIT)]
    mixes = [(_dot(oa_ref[rows, :], wo_ref[0:GROUP, :]) + _dot(ob_ref[rows, :], wo_ref[GROUP:2 * GROUP, :])
              + _dot(oc_ref[rows, :], wo_ref[2 * GROUP:3 * GROUP, :])
              + _dot(od_ref[rows, :], wo_ref[3 * GROUP:4 * GROUP, :])) for rows in halves]
    x1s = [_layer_norm(alpha * x_ref[rows, :] + mix, g1_ref[...], b1_ref[...]) for rows, mix in zip(halves, mixes)]
    xbs = [x1.astype(BF16) for x1 in x1s]
    for c0 in range(0, d_ff, FFN_CHUNK):
        for rows, xb in zip(halves, xbs):
            g = _dot(xb, wg_ref[:, c0:c0 + FFN_CHUNK])
            u = _dot(xb, wu_ref[:, c0:c0 + FFN_CHUNK])
            h_ref[rows, c0:c0 + FFN_CHUNK] = (g * (1.0 / (1.0 + jnp.exp(-g))) * u).astype(BF16)
    ys = [_dot(h_ref[rows, :], wd_ref[...]) for rows in halves]
    for rows, x1, y in zip(halves, x1s, ys):
        out_ref[rows, :] = _layer_norm(alpha * x1 + y, g2_ref[...], b2_ref[...])


def _dense(x2d, outs, w_out, g1, b1, w_gate, w_up, w_down, g2, b2, alpha):
    m, d = x2d.shape
    tm = DENSE_TM
    d_ff = w_gate.shape[1]
    row = lambda i: (i, 0)
    vec = lambda a: a[None, :]
    o_spec = pl.BlockSpec((tm, GROUP), row)
    return pl.pallas_call(
        functools.partial(_dense_kernel, alpha=alpha),
        grid=(m // tm,),
        in_specs=[pl.BlockSpec((tm, d), row), o_spec, o_spec, o_spec, o_spec,
                  _resident((d, d)), _resident((1, d)), _resident((1, d)),
                  _resident((d, d_ff)), _resident((d, d_ff)), _resident((d_ff, d)),
                  _resident((1, d)), _resident((1, d))],
        out_specs=pl.BlockSpec((tm, d), row),
        out_shape=jax.ShapeDtypeStruct((m, d), F32),
        scratch_shapes=[pltpu.VMEM((tm, d_ff), BF16)],
        compiler_params=_params(1),
        name="dense",
    )(x2d, *outs, w_out.astype(BF16), vec(g1), vec(b1), w_gate.astype(BF16), w_up.astype(BF16),
      w_down.astype(BF16), vec(g2), vec(b2))


def kernel(x, w_in, f_bias, conv_w, w_out, rel_bias, ln1_g, ln1_b, w_gate, w_up, w_down, ln2_g, ln2_b):
    b, s, d = x.shape
    depth = w_in.shape[0]
    assert d == 4 * GROUP and w_in.shape[2] == 12 * GROUP + N_HEADS
    assert s % PROJ_TM == 0 and s % (2 * CAUSAL_BQ) == 0 and s % COMB_ROWS == 0
    assert [dil for _, dil in DIL_PATTERNS][0] == 1
    for _, dil in DIL_PATTERNS:
        n_blk = (s // dil) // DIL_BLOCK
        assert n_blk * DIL_BLOCK * dil == s and PROJ_TM % (16 * dil) == 0
        assert ((n_blk - 1) % DIL_GEN_GROUP == 0 if dil == 1 else
                dil % (DIL_MID_GROUP if n_blk > 1 else DIL_RES_GROUP) == 0)
    alpha = (2 * depth) ** 0.25
    bias_tables = _dil_bias_tables(rel_bias)
    w_pad = jnp.pad(jnp.swapaxes(w_in, 1, 2), ((0, 0), (0, LANES - N_HEADS), (0, 0))).astype(BF16)
    x2d = x.reshape(b * s, d)
    for layer in range(depth):
        main, dl4, dl16, aug, out_d, side = _in_proj(x2d, w_pad[layer], f_bias[layer], conv_w[layer], b, s)
        main3 = main.reshape(b, s, main.shape[1])
        out_a, out_c = _causal_attn(main3, aug.reshape(b, s, aug.shape[1]),
                                    side.reshape((b, s // CAUSAL_BQ) + side.shape[1:]))
        out_b = _dil_attn(main3, dl4, dl16, bias_tables)
        outs = [o.reshape(b * s, GROUP) for o in (out_a, out_b, out_c)] + [out_d]
        x2d = _dense(x2d, outs, w_out[layer], ln1_g[layer], ln1_b[layer], w_gate[layer], w_up[layer],
                     w_down[layer], ln2_g[layer], ln2_b[layer], alpha)
    return x2d.reshape(b, s, d)
```

```python
import functools
import math

import jax
import jax.numpy as jnp
import numpy as np
from jax import lax
from jax.experimental import pallas as pl
from jax.experimental.pallas import tpu as pltpu

HEAD_DIM = 64
N_HEADS = 4
GROUP = N_HEADS * HEAD_DIM
LANES = 128
N_PAIRS = GROUP // LANES
CONV_K = 3
DIL_PATTERNS = ((128, 1), (512, 4), (2048, 16))
DIL_BLOCK = 128
DIL_GROUP0 = 6
REL_BUCKETS = 32
REL_MAX_DIST = 2048
LN_EPS = 1e-5
SCALE = HEAD_DIM ** -0.5
NEG = -1e30
SIGN_BIT = np.uint32(0x80000000)
VMEM_LIMIT = 56 * 1024 * 1024

BF16 = jnp.bfloat16
F32 = jnp.float32

PROJ_TM = 512
DENSE_TM = 1024
DENSE_SPLIT = 4
FFN_CHUNK = 256
CAUSAL_BQ = 256
SB_SUB = 128
SB_DEAD = 120.0
FOX_Q_COL = 6 * GROUP
FOX_K_COL = 7 * GROUP
FOX_DEAD = 120.0
FOX_NORM_SLACK = 1.02
DIL_GEN_GROUP = 15
DIL_MID_GROUP = 4
DIL_RES_GROUP = 8
COMB_ROWS = 256


def _dot(a, b):
    return jnp.dot(a, b, preferred_element_type=F32)


def _dot_nt(a, b):
    return lax.dot_general(a, b, (((1,), (1,)), ((), ())), preferred_element_type=F32)


def _resident(shape):
    nd = len(shape)
    return pl.BlockSpec(shape, lambda *_: (0,) * nd, pipeline_mode=pl.Buffered(1))


def _layer_resident(stacked_shape, layer):
    nd = len(stacked_shape) - 1
    return pl.BlockSpec((None,) + tuple(stacked_shape[1:]), lambda *_: (layer,) + (0,) * nd,
                        pipeline_mode=pl.Buffered(1))


def _params(n_axes):
    return pltpu.CompilerParams(dimension_semantics=("arbitrary",) * n_axes,
                                vmem_limit_bytes=VMEM_LIMIT)


def _split_heads_rows(q, lane):
    zero = jnp.zeros_like(q)
    return jnp.concatenate([jnp.where(lane < HEAD_DIM, q, zero),
                            jnp.where(lane >= HEAD_DIM, q, zero)], axis=0)


def _merge_heads_rows(a, lane, rows):
    return jnp.where(lane < HEAD_DIM, a[:rows], a[rows:])


def _pair_lanes(p):
    return slice(p * LANES, (p + 1) * LANES)


def _group_spec(seq, g):
    return pl.BlockSpec((None, seq, GROUP), lambda bi: (bi, 0, g))


def _in_proj_kernel(x_ref, w_ref, fb_ref, cw_ref, sel_ref,
                    main_ref, dl4_ref, dl16_ref, aug_ref, od_ref, side_ref,
                    carry_ref, ubuf_ref, utail_ref, *stage_refs,
                    tiles_per_seq):
    tm = x_ref.shape[0]
    first = (pl.program_id(0) % tiles_per_seq) == 0
    xb = x_ref[...].astype(BF16)

    n_main = main_ref.shape[1]
    cv0 = n_main
    gate0 = n_main + 3 * GROUP
    cvb = _dot_nt(xb, w_ref[cv0:cv0 + GROUP, :])
    cvc = _dot_nt(xb, w_ref[cv0 + GROUP:cv0 + 2 * GROUP, :])
    cvh = _dot_nt(xb, w_ref[cv0 + 2 * GROUP:cv0 + 3 * GROUP, :])
    u = cvc * cvh

    ubuf_ref[0:8, :] = jnp.where(first, 0.0, utail_ref[...])
    ubuf_ref[8:8 + tm, :] = u
    y = (cw_ref[0:1, :] * ubuf_ref[6:6 + tm, :] + cw_ref[1:2, :] * ubuf_ref[7:7 + tm, :]
         + cw_ref[2:3, :] * u)
    od_ref[...] = (cvb * y).astype(BF16)
    utail_ref[...] = u[tm - 8:tm, :]

    dl_col0 = DIL_GROUP0 * LANES
    row_sq = {}
    for c0 in range(0, n_main, GROUP):
        r = _dot_nt(xb, w_ref[c0:c0 + GROUP, :])
        main_ref[:, c0:c0 + GROUP] = r.astype(BF16)
        if c0 in (FOX_Q_COL, FOX_K_COL):
            row_sq[c0] = jnp.sum(r * r, axis=1, keepdims=True)
        if dl_col0 <= c0 < dl_col0 + 3 * GROUP:
            j = (c0 - dl_col0) // LANES
            stage_refs[j][...] = r[:, :LANES]
            stage_refs[j + 1][...] = r[:, LANES:]

    for dil, ref in ((DIL_PATTERNS[1][1], dl4_ref), (DIL_PATTERNS[2][1], dl16_ref)):
        rows = tm // dil
        for c in range(dil):
            for j in range(len(stage_refs)):
                ref[c, :, j * LANES:(j + 1) * LANES] = (
                    stage_refs[j][pl.ds(c, rows, stride=dil), :].astype(BF16))

    g = _dot_nt(xb, w_ref[gate0:gate0 + LANES, :]) + fb_ref[...]
    logf = jnp.minimum(g, 0.0) - jnp.log(1.0 + jnp.exp(-jnp.abs(g)))
    r_i = lax.broadcasted_iota(jnp.int32, (tm, tm), 0)
    c_i = lax.broadcasted_iota(jnp.int32, (tm, tm), 1)
    tri = jnp.where(c_i <= r_i, 1.0, 0.0).astype(BF16)
    l_hi = logf.astype(BF16)
    l_lo = (logf - l_hi.astype(F32)).astype(BF16)
    both = _dot(tri, jnp.concatenate([l_hi, l_lo], axis=1))
    csum = both[:, :LANES] + both[:, LANES:]
    prev = jnp.where(first, 0.0, carry_ref[...])
    csum = csum + prev
    carry_ref[...] = csum[tm - 1:tm, :]
    for b in range(tm // CAUSAL_BQ):
        rows = slice(b * CAUSAL_BQ, (b + 1) * CAUSAL_BQ)
        for j, c0 in enumerate((FOX_Q_COL, FOX_K_COL)):
            side_ref[b, j:j + 1, :] = jnp.broadcast_to(jnp.max(row_sq[c0][rows], axis=0, keepdims=True), (1, LANES))
        side_ref[b, 2:3, :] = csum[b * CAUSAL_BQ:b * CAUSAL_BQ + 1, :]
        side_ref[b, 3:4, :] = csum[(b + 1) * CAUSAL_BQ - 1:(b + 1) * CAUSAL_BQ, :]
        side_ref[b, 4:8, :] = jnp.zeros((4, LANES), F32)
    c_hi = csum.astype(BF16)
    c_r = csum - c_hi.astype(F32)
    c_mid = c_r.astype(BF16)
    c_lo = (c_r - c_mid.astype(F32)).astype(BF16)
    ones = jnp.ones((tm, LANES), BF16)
    aug_ref[...] = _dot(jnp.concatenate([c_hi, c_mid, c_lo, ones], axis=1), sel_ref[...]).astype(BF16)


def _aug_selection():
    sel = np.zeros((4 * LANES, 4 * LANES), np.float32)
    ones_row = 3 * LANES
    for h in range(N_HEADS):
        bq = (h // 2) * LANES + (h % 2) * HEAD_DIM
        bk = 2 * LANES + bq
        for part in range(3):
            sel[part * LANES + h, bq + part] = 1.0
            sel[ones_row, bq + 3 + part] = 1.0
            sel[ones_row, bk + part] = 1.0
            sel[part * LANES + h, bk + 3 + part] = -1.0
    return sel


def _in_proj(x2d, w_pad, layer, f_bias, conv_w, batch, seq):
    m, d = x2d.shape
    tm = PROJ_TM
    tps = seq // tm
    n_main = 9 * GROUP
    d4, d16 = DIL_PATTERNS[1][1], DIL_PATTERNS[2][1]
    fb = jnp.pad(f_bias, (0, LANES - N_HEADS))[None, :]
    cw = jnp.pad(conv_w, ((0, 8 - CONV_K), (0, 0)))
    sel = jnp.asarray(_aug_selection(), BF16)
    row = lambda i: (i, 0)
    res = lambda i: (i // tps, 0, i % tps, 0)
    return pl.pallas_call(
        functools.partial(_in_proj_kernel, tiles_per_seq=tps),
        grid=(m // tm,),
        in_specs=[pl.BlockSpec((tm, d), row), _layer_resident(w_pad.shape, layer),
                  _resident(fb.shape), _resident(cw.shape), _resident(sel.shape)],
        out_specs=[pl.BlockSpec((tm, n_main), row),
                   pl.BlockSpec((None, d4, tm // d4, 3 * GROUP), res),
                   pl.BlockSpec((None, d16, tm // d16, 3 * GROUP), res),
                   pl.BlockSpec((tm, 4 * LANES), row),
                   pl.BlockSpec((tm, GROUP), row),
                   pl.BlockSpec((tm // CAUSAL_BQ, 8, LANES), lambda i: (i, 0, 0))],
        out_shape=[jax.ShapeDtypeStruct((m, n_main), BF16),
                   jax.ShapeDtypeStruct((batch, d4, seq // d4, 3 * GROUP), BF16),
                   jax.ShapeDtypeStruct((batch, d16, seq // d16, 3 * GROUP), BF16),
                   jax.ShapeDtypeStruct((m, 4 * LANES), BF16),
                   jax.ShapeDtypeStruct((m, GROUP), BF16),
                   jax.ShapeDtypeStruct((m // CAUSAL_BQ, 8, LANES), F32)],
        scratch_shapes=[pltpu.VMEM((1, LANES), F32), pltpu.VMEM((tm + 8, GROUP), F32),
                        pltpu.VMEM((8, GROUP), F32)]
        + [pltpu.VMEM((tm, LANES), F32)] * (3 * N_PAIRS),
        compiler_params=_params(1),
        name="in_proj",
    )(x2d, w_pad, fb, cw, sel)


def _sb_ops(i, qstart, q_ref, k_ref, v_ref, t_ref, o_ref, acc_ref, carry_ref):
    bq, sub = CAUSAL_BQ, SB_SUB
    lane_q = lax.broadcasted_iota(jnp.int32, (bq, LANES), 1)
    qss = []
    for p in range(N_PAIRS):
        qss.append(_split_heads_rows(q_ref[pl.ds(qstart, bq), _pair_lanes(p)], lane_q) * SCALE)
        acc_ref[p] = jnp.zeros((2 * bq, LANES), F32)
        carry_ref[p] = jnp.zeros((2 * bq, sub), F32)

    def scores(p, start, g):
        return _dot_nt(qss[p], k_ref[pl.ds(start, g * sub), _pair_lanes(p)])

    def group(p, start, g, masked, z):
        n = g * sub
        neg_abs = pltpu.bitcast(pltpu.bitcast(z, jnp.uint32) | SIGN_BIT, F32)
        sp = jnp.log(1.0 + jnp.exp(neg_abs))
        log_beta = jnp.minimum(z, 0.0) - sp
        log_rest = log_beta - z
        if masked:
            row = lax.broadcasted_iota(jnp.int32, (2 * bq, n), 0)
            col = lax.broadcasted_iota(jnp.int32, (2 * bq, n), 1)
            strict = (start + col) < (qstart + jnp.where(row >= bq, row - bq, row))
            log_rest = jnp.where(strict, log_rest, 0.0)
        hi = log_rest.astype(BF16)
        lo = (log_rest - hi.astype(F32)).astype(BF16)
        carry = carry_ref[p]
        parts = [None] * g
        for j in reversed(range(g)):
            sl = slice(j * sub, (j + 1) * sub)
            cs = _dot(jnp.concatenate([hi[:, sl], lo[:, sl]], axis=1), t_ref[...])
            a = jnp.exp(log_beta[:, sl] + (cs[:, :sub] + carry))
            carry = carry + cs[:, sub:]
            if masked:
                a = jnp.where(strict[:, sl], a, 0.0)
            parts[j] = a.astype(BF16)
        carry_ref[p] = carry
        acc_ref[p] += _dot(jnp.concatenate(parts, axis=1), v_ref[pl.ds(start, n), _pair_lanes(p)])

    def groups(start, g, masked):
        for p in range(N_PAIRS):
            group(p, start, g, masked, scores(p, start, g))

    def top():
        return functools.reduce(jnp.maximum, [jnp.max(carry_ref[p]) for p in range(N_PAIRS)])

    def finish():
        for p in range(N_PAIRS):
            o_ref[pl.ds(qstart, bq), _pair_lanes(p)] = _merge_heads_rows(acc_ref[p], lane_q, bq).astype(BF16)

    return groups, top, finish


def _fox_ops(i, qstart, q_ref, aq_ref, k_ref, ak_ref, v_ref, side_ref, o_ref, m_ref, l_ref, acc_ref):
    bq = CAUSAL_BQ
    lane_q = lax.broadcasted_iota(jnp.int32, (bq, LANES), 1)

    def wide(a):
        return jnp.broadcast_to(a, (2 * bq, LANES))

    qss = [jnp.concatenate(
        [_split_heads_rows(q_ref[pl.ds(qstart, bq), _pair_lanes(p)], lane_q) * SCALE,
         _split_heads_rows(aq_ref[pl.ds(qstart, bq), _pair_lanes(p)], lane_q)], axis=1)
        for p in range(N_PAIRS)]

    def scores(p, start, n):
        rows = pl.ds(start, n)
        kk = jnp.concatenate([k_ref[rows, _pair_lanes(p)], ak_ref[rows, _pair_lanes(p)]], axis=1)
        return _dot_nt(qss[p], kk)

    def values(p, start, n):
        return v_ref[pl.ds(start, n), _pair_lanes(p)]

    def diag():
        row = lax.broadcasted_iota(jnp.int32, (2 * bq, bq), 0)
        col = lax.broadcasted_iota(jnp.int32, (2 * bq, bq), 1)
        causal = col <= jnp.where(row >= bq, row - bq, row)
        for p in range(N_PAIRS):
            z = scores(p, qstart, bq)
            vs = values(p, qstart, bq)
            z = jnp.where(causal, z, NEG)
            m0 = jnp.max(z, axis=1, keepdims=True)
            pr = jnp.exp(z - m0)
            m_ref[p] = wide(m0)
            l_ref[p] = wide(jnp.sum(pr, axis=1, keepdims=True))
            acc_ref[p] = _dot(pr.astype(BF16), vs)

    def step(start, n):
        for p in range(N_PAIRS):
            z = scores(p, start, n)
            vs = values(p, start, n)
            m_old = m_ref[p]
            m_new = jnp.maximum(m_old, jnp.max(z, axis=1, keepdims=True))
            alpha = jnp.exp(m_old - m_new)
            pr = jnp.exp(z - jnp.concatenate([m_new] * (n // LANES), axis=1))
            l_ref[p] = alpha * l_ref[p] + jnp.sum(pr, axis=1, keepdims=True)
            acc_ref[p] = alpha * acc_ref[p] + _dot(pr.astype(BF16), vs)
            m_ref[p] = m_new

    def margin(end_blk, k_norm_sq):
        q_side = side_ref[i]
        k_side = side_ref[end_blk]
        qk = (0.5 * SCALE * FOX_NORM_SLACK) * (q_side[0:1, :] + k_norm_sq)
        lane = lax.broadcasted_iota(jnp.int32, (1, LANES), 1)
        top = jnp.max(jnp.where(lane < N_HEADS, qk + (q_side[2:3, :] - k_side[3:4, :]), NEG))
        m_min = functools.reduce(jnp.minimum, [jnp.min(m_ref[p]) for p in range(N_PAIRS)])
        return top - m_min

    def finish():
        for p in range(N_PAIRS):
            o_ref[pl.ds(qstart, bq), _pair_lanes(p)] = _merge_heads_rows(
                acc_ref[p] / l_ref[p], lane_q, bq).astype(BF16)

    return diag, step, margin, finish


def _causal_kernel(sq_ref, sk_ref, sv_ref, t_ref, fq_ref, faq_ref, fk_ref, fak_ref, fv_ref, side_ref,
                   so_ref, fo_ref, s_acc, s_carry, s_live, f_m, f_l, f_acc, f_live, *, seq):
    bq = CAUSAL_BQ
    n_q = seq // bq
    k_norm_sq = functools.reduce(jnp.maximum, [side_ref[b, 1:2, :] for b in range(n_q)])

    def q_pair(j, outer):
        blocks = []
        for slot in range(2):
            i = 2 * j + slot
            qstart = pl.multiple_of(i * bq, bq)
            sb = _sb_ops(i, qstart, sq_ref, sk_ref, sv_ref, t_ref, so_ref, s_acc.at[slot], s_carry.at[slot])
            fox = _fox_ops(i, qstart, fq_ref, faq_ref, fk_ref, fak_ref, fv_ref, side_ref, fo_ref,
                           f_m.at[slot], f_l.at[slot], f_acc.at[slot])
            blocks.append((qstart, sb, fox))
        (q_even, sb_even, fox_even), (q_odd, sb_odd, fox_odd) = blocks

        for qstart, (sb_groups, _, _), _ in blocks:
            sb_groups(qstart, bq // SB_SUB, True)
        for _, _, (fox_diag, _, _, _) in blocks:
            fox_diag()
        sb_odd[0](q_even, bq // SB_SUB, False)
        fox_odd[1](q_even, bq)
        s_live[0] = 0
        s_live[1] = jnp.where(sb_odd[1]() >= -SB_DEAD, 1, 0)
        f_live[0] = 0

        def fox_trip(start, first_blk):
            for _, _, (_, fox_step, _, _) in blocks:
                fox_step(start, 2 * bq)
            nxt = jnp.maximum(first_blk - 1, 0)
            worst = jnp.maximum(fox_even[2](nxt, k_norm_sq), fox_odd[2](nxt, k_norm_sq))
            f_live[0] = jnp.where(worst >= -FOX_DEAD, 1, 0)

        def sb_half(slot, start, half):
            sb_groups, sb_top, _ = blocks[slot][1]
            sb_groups(pl.multiple_of(start + half * bq, bq), bq // SB_SUB, False)
            s_live[slot] = jnp.where(sb_top() >= -SB_DEAD, 1, 0)

        def sb_rest(start, todo):
            for half, slot in todo:
                pl.when(s_live[slot] == 1)(functools.partial(sb_half, slot, start, half))

        @pl.when(j > 0)
        def _():
            first_blk = 2 * (j - 1)
            start = pl.multiple_of(first_blk * bq, 2 * bq)
            fox_trip(start, first_blk)
            sb_half(0, start, 1)
            sb_rest(start, ((1, 1), (0, 0), (0, 1)))

        def cond(state):
            t, sb_on, fox_on = state
            return (t < j) & ((sb_on == 1) | (fox_on == 1))

        def body(state):
            t = state[0]
            first_blk = 2 * (j - 1 - t)
            start = pl.multiple_of(first_blk * bq, 2 * bq)
            pl.when(f_live[0] == 1)(functools.partial(fox_trip, start, first_blk))
            sb_rest(start, ((1, 0), (1, 1), (0, 0), (0, 1)))
            return t + 1, s_live[0] | s_live[1], f_live[0]

        lax.while_loop(cond, body, (jnp.int32(1), s_live[0] | s_live[1], f_live[0]))
        for _, (_, _, sb_finish), (_, _, _, fox_finish) in blocks:
            sb_finish()
            fox_finish()
        return outer

    lax.fori_loop(0, n_q // 2, q_pair, 0)


def _sb_tail_matrix():
    sub = SB_SUB
    j = np.arange(2 * sub)[:, None] % sub
    s = np.arange(2 * sub)[None, :]
    return np.where(s < sub, j > s, True).astype(np.float32)


def _causal_attn(main3, aug3, side4):
    b, s, _ = main3.shape
    tmat = jnp.asarray(_sb_tail_matrix(), BF16)
    state = pltpu.VMEM((2, N_PAIRS, 2 * CAUSAL_BQ, LANES), F32)
    flag = pltpu.SMEM((2,), jnp.int32)
    out = jax.ShapeDtypeStruct((b, s, GROUP), BF16)
    return pl.pallas_call(
        functools.partial(_causal_kernel, seq=s),
        grid=(b,),
        in_specs=[_group_spec(s, 0), _group_spec(s, 1), _group_spec(s, 2), _resident(tmat.shape),
                  _group_spec(s, 6), _group_spec(s, 0), _group_spec(s, 7), _group_spec(s, 1), _group_spec(s, 8),
                  pl.BlockSpec((None,) + side4.shape[1:], lambda bi: (bi, 0, 0, 0))],
        out_specs=[_group_spec(s, 0), _group_spec(s, 0)],
        out_shape=[out, out],
        scratch_shapes=[state, pltpu.VMEM((2, N_PAIRS, 2 * CAUSAL_BQ, SB_SUB), F32), flag,
                        state, state, state, flag],
        compiler_params=_params(1),
        name="causal_attn",
    )(main3, main3, main3, tmat, main3, aug3, main3, aug3, main3, side4)


def _dil_kernel(*refs, seq):
    n_pat = len(DIL_PATTERNS)
    qkv = refs[:3 * n_pat]
    bias_ref = refs[3 * n_pat]
    o_ref = refs[3 * n_pat + 1]
    scratch = refs[3 * n_pat + 2:]
    n_state = n_pat * N_PAIRS
    u_refs, m_refs, l_refs = scratch[0:n_state], scratch[n_state:2 * n_state], scratch[2 * n_state:3 * n_state]
    blk = DIL_BLOCK
    lane = lax.broadcasted_iota(jnp.int32, (blk, LANES), 1)

    def block(p, pidx, dil, c, n, first):
        q_ref, k_ref, v_ref = qkv[3 * pidx:3 * pidx + 3]
        lanes = _pair_lanes(p)

        def rd(ref, start, size):
            return ref[pl.ds(start, size), lanes] if dil == 1 else ref[c, pl.ds(start, size), lanes]

        qstart = n * blk if isinstance(n, int) else pl.multiple_of(n * blk, blk)
        qs = _split_heads_rows(rd(q_ref, qstart, blk), lane) * SCALE
        h0, h1 = 2 * p, 2 * p + 1
        if first:
            keys, vals = rd(k_ref, 0, blk), rd(v_ref, 0, blk)
            bias = jnp.concatenate([bias_ref[pidx, h0, :, blk:], bias_ref[pidx, h1, :, blk:]], axis=0)
        else:
            kstart = (n - 1) * blk if isinstance(n, int) else pl.multiple_of((n - 1) * blk, blk)
            keys, vals = rd(k_ref, kstart, 2 * blk), rd(v_ref, kstart, 2 * blk)
            bias = jnp.concatenate([bias_ref[pidx, h0], bias_ref[pidx, h1]], axis=0)
        z = _dot_nt(qs, keys) + bias
        m = jnp.max(z, axis=1, keepdims=True)
        pr = jnp.exp(z - m)
        wide = lambda a: jnp.broadcast_to(a, (2 * blk, LANES))
        if (seq // dil) // blk == 1:
            den = wide(jnp.sum(pr, axis=1, keepdims=True))
            u = _dot(pr.astype(BF16), vals)
        else:
            ones = jnp.ones((vals.shape[0], LANES), BF16)
            u = _dot(pr.astype(BF16), jnp.concatenate([vals, ones], axis=1))
            u, den = u[:, :LANES], u[:, LANES:]
        idx = pl.ds(n * (blk * dil) + c, blk, stride=dil) if dil > 1 else pl.ds(qstart, blk)
        st = pidx * N_PAIRS + p
        u_refs[st][idx, :] = _merge_heads_rows(u, lane, blk)
        m_refs[st][idx, :] = _merge_heads_rows(wide(m), lane, blk)
        l_refs[st][idx, :] = _merge_heads_rows(den, lane, blk)

    def blocks(pidx, dil, c, n, first):
        for p in range(N_PAIRS):
            block(p, pidx, dil, c, n, first)

    for pidx, (_, dil) in enumerate(DIL_PATTERNS):
        n_blk = (seq // dil) // blk
        if dil == 1:
            blocks(pidx, dil, 0, 0, True)

            def body(g, carry, pidx=pidx, dil=dil):
                for jj in range(DIL_GEN_GROUP):
                    blocks(pidx, dil, 0, 1 + g * DIL_GEN_GROUP + jj, False)
                return carry
            lax.fori_loop(0, (n_blk - 1) // DIL_GEN_GROUP, body, 0)
        elif n_blk > 1:
            def body(g, carry, pidx=pidx, dil=dil, n_blk=n_blk):
                for jj in range(DIL_MID_GROUP):
                    c = g * DIL_MID_GROUP + jj
                    blocks(pidx, dil, c, 0, True)
                    for n in range(1, n_blk):
                        blocks(pidx, dil, c, n, False)
                return carry
            lax.fori_loop(0, dil // DIL_MID_GROUP, body, 0)
        else:
            def body(g, carry, pidx=pidx, dil=dil):
                for jj in range(DIL_RES_GROUP):
                    blocks(pidx, dil, g * DIL_RES_GROUP + jj, 0, True)
                return carry
            lax.fori_loop(0, dil // DIL_RES_GROUP, body, 0)

    def comb(j, carry):
        sl = pl.ds(pl.multiple_of(j * COMB_ROWS, COMB_ROWS), COMB_ROWS)
        for p in range(N_PAIRS):
            sts = [pidx * N_PAIRS + p for pidx in range(n_pat)]
            ms = [m_refs[st][sl, :] for st in sts]
            m_all = functools.reduce(jnp.maximum, ms)
            es = [jnp.exp(mp - m_all) for mp in ms]
            num = sum(e * u_refs[st][sl, :] for e, st in zip(es, sts))
            den = sum(e * l_refs[st][sl, :] for e, st in zip(es, sts))
            o_ref[sl, _pair_lanes(p)] = (num / den).astype(BF16)
        return carry
    lax.fori_loop(0, seq // COMB_ROWS, comb, 0)


def _t5_bucket_of(dist):
    max_exact = REL_BUCKETS // 2
    nf = jnp.maximum(dist, 1).astype(jnp.float32)
    large = max_exact + (jnp.log(nf / max_exact) / math.log(REL_MAX_DIST / max_exact)
                         * (REL_BUCKETS - max_exact)).astype(jnp.int32)
    large = jnp.minimum(large, REL_BUCKETS - 1)
    return jnp.where(dist < max_exact, dist, large)


def _dil_bias_tables(rel_bias):
    blk = DIL_BLOCK
    qi = jnp.arange(blk)[:, None]
    kj = jnp.arange(2 * blk)[None, :]
    sub = qi + blk - kj
    tables = []
    for window, dil in DIL_PATTERNS:
        in_band = (sub >= 0) & (sub <= window // dil)
        bucket = _t5_bucket_of(jnp.maximum(sub, 0) * dil)
        hit = bucket[None, :, :, None] == jnp.arange(REL_BUCKETS)
        bias = jnp.sum(jnp.where(hit, rel_bias.T.astype(F32)[:, None, None, :], 0.0), axis=-1)
        tables.append(jnp.where(in_band[None], bias, NEG))
    return jnp.stack(tables)


def _dil_attn(main3, dl4, dl16, bias_tables):
    b, s, _ = main3.shape
    n_pat = len(DIL_PATTERNS)
    operands, in_specs = [], []
    for role in range(3):
        operands.append(main3)
        in_specs.append(_group_spec(s, DIL_GROUP0 // N_PAIRS + role))
    for arr in (dl4, dl16):
        dil, rows = arr.shape[1], arr.shape[2]
        for role in range(3):
            operands.append(arr)
            in_specs.append(pl.BlockSpec((None, dil, rows, GROUP), lambda bi, role=role: (bi, 0, 0, role)))
    operands.append(bias_tables)
    in_specs.append(_resident(bias_tables.shape))
    return pl.pallas_call(
        functools.partial(_dil_kernel, seq=s),
        grid=(b,),
        in_specs=in_specs,
        out_specs=_group_spec(s, 0),
        out_shape=jax.ShapeDtypeStruct((b, s, GROUP), BF16),
        scratch_shapes=[pltpu.VMEM((s, LANES), F32)] * (3 * n_pat * N_PAIRS),
        compiler_params=_params(1),
        name="dil_attn",
    )(*operands)


def _layer_norm(v, g, b):
    mu = jnp.mean(v, axis=-1, keepdims=True)
    d = v - mu
    var = jnp.mean(d * d, axis=-1, keepdims=True)
    return d * lax.rsqrt(var + LN_EPS) * g + b


def _dense_kernel(x_ref, oa_ref, ob_ref, oc_ref, od_ref, wo_ref, g1_ref, b1_ref,
                  wg_ref, wu_ref, wd_ref, g2_ref, b2_ref, out_ref, h_ref, *, alpha):
    tm = x_ref.shape[0]
    d_ff = wg_ref.shape[1]
    halves = [slice(r0, r0 + tm // DENSE_SPLIT) for r0 in range(0, tm, tm // DENSE_SPLIT)]
    mixes = [(_dot(oa_ref[rows, :], wo_ref[0:GROUP, :]) + _dot(ob_ref[rows, :], wo_ref[GROUP:2 * GROUP, :])
              + _dot(oc_ref[rows, :], wo_ref[2 * GROUP:3 * GROUP, :])
              + _dot(od_ref[rows, :], wo_ref[3 * GROUP:4 * GROUP, :])) for rows in halves]
    x1s = [_layer_norm(alpha * x_ref[rows, :] + mix, g1_ref[...], b1_ref[...]) for rows, mix in zip(halves, mixes)]
    xbs = [x1.astype(BF16) for x1 in x1s]
    for c0 in range(0, d_ff, FFN_CHUNK):
        for rows, xb in zip(halves, xbs):
            g = _dot(xb, wg_ref[:, c0:c0 + FFN_CHUNK])
            u = _dot(xb, wu_ref[:, c0:c0 + FFN_CHUNK])
            h_ref[rows, c0:c0 + FFN_CHUNK] = (g * (1.0 / (1.0 + jnp.exp(-g))) * u).astype(BF16)
    ys = [_dot(h_ref[rows, :], wd_ref[...]) for rows in halves]
    for rows, x1, y in zip(halves, x1s, ys):
        out_ref[rows, :] = _layer_norm(alpha * x1 + y, g2_ref[...], b2_ref[...])


def _dense(x2d, outs, layer, w_out, g1, b1, w_gate, w_up, w_down, g2, b2, alpha):
    m, d = x2d.shape
    tm = DENSE_TM
    d_ff = w_gate.shape[2]
    row = lambda i: (i, 0)
    o_spec = pl.BlockSpec((tm, GROUP), row)
    params = (w_out, g1, b1, w_gate, w_up, w_down, g2, b2)
    return pl.pallas_call(
        functools.partial(_dense_kernel, alpha=alpha),
        grid=(m // tm,),
        in_specs=[pl.BlockSpec((tm, d), row), o_spec, o_spec, o_spec, o_spec]
        + [_layer_resident(a.shape, layer) for a in params],
        out_specs=pl.BlockSpec((tm, d), row),
        out_shape=jax.ShapeDtypeStruct((m, d), F32),
        scratch_shapes=[pltpu.VMEM((tm, d_ff), BF16)],
        compiler_params=_params(1),
        name="dense",
    )(x2d, *outs, *params)


def kernel(x, w_in, f_bias, conv_w, w_out, rel_bias, ln1_g, ln1_b, w_gate, w_up, w_down, ln2_g, ln2_b):
    b, s, d = x.shape
    depth = w_in.shape[0]
    assert d == 4 * GROUP and w_in.shape[2] == 12 * GROUP + N_HEADS
    assert s % PROJ_TM == 0 and s % (2 * CAUSAL_BQ) == 0 and s % COMB_ROWS == 0
    assert [dil for _, dil in DIL_PATTERNS][0] == 1
    for _, dil in DIL_PATTERNS:
        n_blk = (s // dil) // DIL_BLOCK
        assert n_blk * DIL_BLOCK * dil == s and PROJ_TM % (16 * dil) == 0
        assert ((n_blk - 1) % DIL_GEN_GROUP == 0 if dil == 1 else
                dil % (DIL_MID_GROUP if n_blk > 1 else DIL_RES_GROUP) == 0)
    alpha = (2 * depth) ** 0.25
    bias_tables = _dil_bias_tables(rel_bias)
    w_pad = jnp.pad(jnp.swapaxes(w_in, 1, 2), ((0, 0), (0, LANES - N_HEADS), (0, 0))).astype(BF16)
    dense_params = (w_out.astype(BF16), ln1_g[:, None, :], ln1_b[:, None, :], w_gate.astype(BF16),
                    w_up.astype(BF16), w_down.astype(BF16), ln2_g[:, None, :], ln2_b[:, None, :])
    x2d = x.reshape(b * s, d)
    for layer in range(depth):
        main, dl4, dl16, aug, out_d, side = _in_proj(x2d, w_pad, layer, f_bias[layer], conv_w[layer], b, s)
        main3 = main.reshape(b, s, main.shape[1])
        out_a, out_c = _causal_attn(main3, aug.reshape(b, s, aug.shape[1]),
                                    side.reshape((b, s // CAUSAL_BQ) + side.shape[1:]))
        out_b = _dil_attn(main3, dl4, dl16, bias_tables)
        outs = [o.reshape(b * s, GROUP) for o in (out_a, out_b, out_c)] + [out_d]
        x2d = _dense(x2d, outs, layer, *dense_params, alpha)
    return x2d.reshape(b, s, d)
```

```python
import functools
import math

import jax
import jax.numpy as jnp
import numpy as np
from jax import lax
from jax.experimental import pallas as pl
from jax.experimental.pallas import tpu as pltpu

HEAD_DIM = 64
N_HEADS = 4
GROUP = N_HEADS * HEAD_DIM
LANES = 128
N_PAIRS = GROUP // LANES
CONV_K = 3
DIL_PATTERNS = ((128, 1), (512, 4), (2048, 16))
DIL_BLOCK = 128
DIL_GROUP0 = 6
REL_BUCKETS = 32
REL_MAX_DIST = 2048
LN_EPS = 1e-5
SCALE = HEAD_DIM ** -0.5
NEG = -1e30
SIGN_BIT = np.uint32(0x80000000)
VMEM_LIMIT = 56 * 1024 * 1024

BF16 = jnp.bfloat16
F32 = jnp.float32

PROJ_TM = 512
DENSE_TM = 1024
DENSE_SPLIT = 4
FFN_CHUNK = 256
CAUSAL_BQ = 256
SB_SUB = 128
SB_DEAD = 120.0
FOX_Q_COL = 6 * GROUP
FOX_K_COL = 7 * GROUP
FOX_DEAD = 120.0
FOX_NORM_SLACK = 1.02
DIL_GEN_GROUP = 15
DIL_MID_GROUP = 4
DIL_RES_GROUP = 8
COMB_ROWS = 256


def _dot(a, b):
    return jnp.dot(a, b, preferred_element_type=F32)


def _dot_nt(a, b):
    return lax.dot_general(a, b, (((1,), (1,)), ((), ())), preferred_element_type=F32)


def _resident(shape):
    nd = len(shape)
    return pl.BlockSpec(shape, lambda *_: (0,) * nd, pipeline_mode=pl.Buffered(1))


def _layer_resident(stacked_shape, layer):
    nd = len(stacked_shape) - 1
    return pl.BlockSpec((None,) + tuple(stacked_shape[1:]), lambda *_: (layer,) + (0,) * nd,
                        pipeline_mode=pl.Buffered(1))


def _params(n_axes):
    return pltpu.CompilerParams(dimension_semantics=("arbitrary",) * n_axes,
                                vmem_limit_bytes=VMEM_LIMIT)


def _split_heads_rows(q, lane):
    zero = jnp.zeros_like(q)
    return jnp.concatenate([jnp.where(lane < HEAD_DIM, q, zero),
                            jnp.where(lane >= HEAD_DIM, q, zero)], axis=0)


def _merge_heads_rows(a, lane, rows):
    return jnp.where(lane < HEAD_DIM, a[:rows], a[rows:])


def _pair_lanes(p):
    return slice(p * LANES, (p + 1) * LANES)


def _group_spec(seq, g):
    return pl.BlockSpec((None, seq, GROUP), lambda bi: (bi, 0, g))


def _in_proj_kernel(x_ref, w_ref, fb_ref, cw_ref, sel_ref,
                    main_ref, dl4_ref, dl16_ref, aug_ref, od_ref, side_ref,
                    carry_ref, ubuf_ref, utail_ref, *stage_refs,
                    tiles_per_seq):
    tm = x_ref.shape[0]
    first = (pl.program_id(0) % tiles_per_seq) == 0
    xb = x_ref[...].astype(BF16)

    n_main = main_ref.shape[1]
    cv0 = n_main
    gate0 = n_main + 3 * GROUP
    cvb = _dot_nt(xb, w_ref[cv0:cv0 + GROUP, :])
    cvc = _dot_nt(xb, w_ref[cv0 + GROUP:cv0 + 2 * GROUP, :])
    cvh = _dot_nt(xb, w_ref[cv0 + 2 * GROUP:cv0 + 3 * GROUP, :])
    u = cvc * cvh

    ubuf_ref[0:8, :] = jnp.where(first, 0.0, utail_ref[...])
    ubuf_ref[8:8 + tm, :] = u
    y = (cw_ref[0:1, :] * ubuf_ref[6:6 + tm, :] + cw_ref[1:2, :] * ubuf_ref[7:7 + tm, :]
         + cw_ref[2:3, :] * u)
    od_ref[...] = (cvb * y).astype(BF16)
    utail_ref[...] = u[tm - 8:tm, :]

    dl_col0 = DIL_GROUP0 * LANES
    row_sq = {}
    for c0 in range(0, n_main, GROUP):
        r = _dot_nt(xb, w_ref[c0:c0 + GROUP, :])
        main_ref[:, c0:c0 + GROUP] = r.astype(BF16)
        if c0 in (FOX_Q_COL, FOX_K_COL):
            row_sq[c0] = jnp.sum(r * r, axis=1, keepdims=True)
        if dl_col0 <= c0 < dl_col0 + 3 * GROUP:
            j = (c0 - dl_col0) // LANES
            stage_refs[j][...] = r[:, :LANES]
            stage_refs[j + 1][...] = r[:, LANES:]

    for dil, ref in ((DIL_PATTERNS[1][1], dl4_ref), (DIL_PATTERNS[2][1], dl16_ref)):
        rows = tm // dil
        for c in range(dil):
            for j in range(len(stage_refs)):
                ref[c, :, j * LANES:(j + 1) * LANES] = (
                    stage_refs[j][pl.ds(c, rows, stride=dil), :].astype(BF16))

    g = _dot_nt(xb, w_ref[gate0:gate0 + LANES, :]) + fb_ref[...]
    logf = jnp.minimum(g, 0.0) - jnp.log(1.0 + jnp.exp(-jnp.abs(g)))
    r_i = lax.broadcasted_iota(jnp.int32, (tm, tm), 0)
    c_i = lax.broadcasted_iota(jnp.int32, (tm, tm), 1)
    tri = jnp.where(c_i <= r_i, 1.0, 0.0).astype(BF16)
    l_hi = logf.astype(BF16)
    l_lo = (logf - l_hi.astype(F32)).astype(BF16)
    both = _dot(tri, jnp.concatenate([l_hi, l_lo], axis=1))
    csum = both[:, :LANES] + both[:, LANES:]
    prev = jnp.where(first, 0.0, carry_ref[...])
    csum = csum + prev
    carry_ref[...] = csum[tm - 1:tm, :]
    for b in range(tm // CAUSAL_BQ):
        rows = slice(b * CAUSAL_BQ, (b + 1) * CAUSAL_BQ)
        for j, c0 in enumerate((FOX_Q_COL, FOX_K_COL)):
            side_ref[b, j:j + 1, :] = jnp.broadcast_to(jnp.max(row_sq[c0][rows], axis=0, keepdims=True), (1, LANES))
        side_ref[b, 2:3, :] = csum[b * CAUSAL_BQ:b * CAUSAL_BQ + 1, :]
        side_ref[b, 3:4, :] = csum[(b + 1) * CAUSAL_BQ - 1:(b + 1) * CAUSAL_BQ, :]
        side_ref[b, 4:8, :] = jnp.zeros((4, LANES), F32)
    c_hi = csum.astype(BF16)
    c_r = csum - c_hi.astype(F32)
    c_mid = c_r.astype(BF16)
    c_lo = (c_r - c_mid.astype(F32)).astype(BF16)
    ones = jnp.ones((tm, LANES), BF16)
    aug_ref[...] = _dot(jnp.concatenate([c_hi, c_mid, c_lo, ones], axis=1), sel_ref[...]).astype(BF16)


def _aug_selection():
    sel = np.zeros((4 * LANES, 4 * LANES), np.float32)
    ones_row = 3 * LANES
    for h in range(N_HEADS):
        bq = (h // 2) * LANES + (h % 2) * HEAD_DIM
        bk = 2 * LANES + bq
        for part in range(3):
            sel[part * LANES + h, bq + part] = 1.0
            sel[ones_row, bq + 3 + part] = 1.0
            sel[ones_row, bk + part] = 1.0
            sel[part * LANES + h, bk + 3 + part] = -1.0
    return sel


def _in_proj(x2d, w_pad, layer, f_bias, conv_w, batch, seq):
    m, d = x2d.shape
    tm = PROJ_TM
    tps = seq // tm
    n_main = 9 * GROUP
    d4, d16 = DIL_PATTERNS[1][1], DIL_PATTERNS[2][1]
    fb = jnp.pad(f_bias, (0, LANES - N_HEADS))[None, :]
    cw = jnp.pad(conv_w, ((0, 8 - CONV_K), (0, 0)))
    sel = jnp.asarray(_aug_selection(), BF16)
    row = lambda i: (i, 0)
    res = lambda i: (i // tps, 0, i % tps, 0)
    return pl.pallas_call(
        functools.partial(_in_proj_kernel, tiles_per_seq=tps),
        grid=(m // tm,),
        in_specs=[pl.BlockSpec((tm, d), row), _layer_resident(w_pad.shape, layer),
                  _resident(fb.shape), _resident(cw.shape), _resident(sel.shape)],
        out_specs=[pl.BlockSpec((tm, n_main), row),
                   pl.BlockSpec((None, d4, tm // d4, 3 * GROUP), res),
                   pl.BlockSpec((None, d16, tm // d16, 3 * GROUP), res),
                   pl.BlockSpec((tm, 4 * LANES), row),
                   pl.BlockSpec((tm, GROUP), row),
                   pl.BlockSpec((tm // CAUSAL_BQ, 8, LANES), lambda i: (i, 0, 0))],
        out_shape=[jax.ShapeDtypeStruct((m, n_main), BF16),
                   jax.ShapeDtypeStruct((batch, d4, seq // d4, 3 * GROUP), BF16),
                   jax.ShapeDtypeStruct((batch, d16, seq // d16, 3 * GROUP), BF16),
                   jax.ShapeDtypeStruct((m, 4 * LANES), BF16),
                   jax.ShapeDtypeStruct((m, GROUP), BF16),
                   jax.ShapeDtypeStruct((m // CAUSAL_BQ, 8, LANES), F32)],
        scratch_shapes=[pltpu.VMEM((1, LANES), F32), pltpu.VMEM((tm + 8, GROUP), F32),
                        pltpu.VMEM((8, GROUP), F32)]
        + [pltpu.VMEM((tm, LANES), F32)] * (3 * N_PAIRS),
        compiler_params=_params(1),
        name="in_proj",
    )(x2d, w_pad, fb, cw, sel)


def _sb_ops(i, qstart, q_ref, k_ref, v_ref, t_ref, o_ref, acc_ref, carry_ref):
    bq, sub = CAUSAL_BQ, SB_SUB
    lane_q = lax.broadcasted_iota(jnp.int32, (bq, LANES), 1)
    qss = []
    for p in range(N_PAIRS):
        qss.append(_split_heads_rows(q_ref[pl.ds(qstart, bq), _pair_lanes(p)], lane_q) * SCALE)
        acc_ref[p] = jnp.zeros((2 * bq, LANES), F32)
        carry_ref[p] = jnp.zeros((2 * bq, sub), F32)

    def scores(p, start, g):
        return _dot_nt(qss[p], k_ref[pl.ds(start, g * sub), _pair_lanes(p)])

    def group(p, start, g, masked, z):
        n = g * sub
        neg_abs = pltpu.bitcast(pltpu.bitcast(z, jnp.uint32) | SIGN_BIT, F32)
        sp = jnp.log(1.0 + jnp.exp(neg_abs))
        log_beta = jnp.minimum(z, 0.0) - sp
        log_rest = log_beta - z
        if masked:
            row = lax.broadcasted_iota(jnp.int32, (2 * bq, n), 0)
            col = lax.broadcasted_iota(jnp.int32, (2 * bq, n), 1)
            strict = (start + col) < (qstart + jnp.where(row >= bq, row - bq, row))
            log_rest = jnp.where(strict, log_rest, 0.0)
        hi = log_rest.astype(BF16)
        lo = (log_rest - hi.astype(F32)).astype(BF16)
        carry = carry_ref[p]
        parts = [None] * g
        for j in reversed(range(g)):
            sl = slice(j * sub, (j + 1) * sub)
            cs = _dot(jnp.concatenate([hi[:, sl], lo[:, sl]], axis=1), t_ref[...])
            a = jnp.exp(log_beta[:, sl] + (cs[:, :sub] + carry))
            carry = carry + cs[:, sub:]
            if masked:
                a = jnp.where(strict[:, sl], a, 0.0)
            parts[j] = a.astype(BF16)
        carry_ref[p] = carry
        acc_ref[p] += _dot(jnp.concatenate(parts, axis=1), v_ref[pl.ds(start, n), _pair_lanes(p)])

    def groups(start, g, masked):
        for p in range(N_PAIRS):
            group(p, start, g, masked, scores(p, start, g))

    def top():
        return functools.reduce(jnp.maximum, [jnp.max(carry_ref[p]) for p in range(N_PAIRS)])

    def finish():
        for p in range(N_PAIRS):
            o_ref[pl.ds(qstart, bq), _pair_lanes(p)] = _merge_heads_rows(acc_ref[p], lane_q, bq).astype(BF16)

    return groups, top, finish


def _fox_ops(i, qstart, q_ref, aq_ref, k_ref, ak_ref, v_ref, side_ref, o_ref, m_ref, l_ref, acc_ref):
    bq = CAUSAL_BQ
    lane_q = lax.broadcasted_iota(jnp.int32, (bq, LANES), 1)

    def wide(a):
        return jnp.broadcast_to(a, (2 * bq, LANES))

    qss = [jnp.concatenate(
        [_split_heads_rows(q_ref[pl.ds(qstart, bq), _pair_lanes(p)], lane_q) * SCALE,
         _split_heads_rows(aq_ref[pl.ds(qstart, bq), _pair_lanes(p)], lane_q)], axis=1)
        for p in range(N_PAIRS)]

    def scores(p, start, n):
        rows = pl.ds(start, n)
        kk = jnp.concatenate([k_ref[rows, _pair_lanes(p)], ak_ref[rows, _pair_lanes(p)]], axis=1)
        return _dot_nt(qss[p], kk)

    def values(p, start, n):
        return v_ref[pl.ds(start, n), _pair_lanes(p)]

    def diag():
        row = lax.broadcasted_iota(jnp.int32, (2 * bq, bq), 0)
        col = lax.broadcasted_iota(jnp.int32, (2 * bq, bq), 1)
        causal = col <= jnp.where(row >= bq, row - bq, row)
        for p in range(N_PAIRS):
            z = scores(p, qstart, bq)
            vs = values(p, qstart, bq)
            z = jnp.where(causal, z, NEG)
            m0 = jnp.max(z, axis=1, keepdims=True)
            pr = jnp.exp(z - m0)
            m_ref[p] = wide(m0)
            l_ref[p] = wide(jnp.sum(pr, axis=1, keepdims=True))
            acc_ref[p] = _dot(pr.astype(BF16), vs)

    def step(start, n):
        for p in range(N_PAIRS):
            z = scores(p, start, n)
            vs = values(p, start, n)
            m_old = m_ref[p]
            m_new = jnp.maximum(m_old, jnp.max(z, axis=1, keepdims=True))
            alpha = jnp.exp(m_old - m_new)
            pr = jnp.exp(z - jnp.concatenate([m_new] * (n // LANES), axis=1))
            l_ref[p] = alpha * l_ref[p] + jnp.sum(pr, axis=1, keepdims=True)
            acc_ref[p] = alpha * acc_ref[p] + _dot(pr.astype(BF16), vs)
            m_ref[p] = m_new

    def margin(end_blk, k_norm_sq):
        q_side = side_ref[i]
        k_side = side_ref[end_blk]
        qk = (0.5 * SCALE * FOX_NORM_SLACK) * (q_side[0:1, :] + k_norm_sq)
        lane = lax.broadcasted_iota(jnp.int32, (1, LANES), 1)
        top = jnp.max(jnp.where(lane < N_HEADS, qk + (q_side[2:3, :] - k_side[3:4, :]), NEG))
        m_min = functools.reduce(jnp.minimum, [jnp.min(m_ref[p]) for p in range(N_PAIRS)])
        return top - m_min

    def finish():
        for p in range(N_PAIRS):
            o_ref[pl.ds(qstart, bq), _pair_lanes(p)] = _merge_heads_rows(
                acc_ref[p] / l_ref[p], lane_q, bq).astype(BF16)

    return diag, step, margin, finish


def _causal_kernel(sq_ref, sk_ref, sv_ref, t_ref, fq_ref, faq_ref, fk_ref, fak_ref, fv_ref, side_ref,
                   so_ref, fo_ref, s_acc, s_carry, s_live, f_m, f_l, f_acc, f_live, *, seq):
    bq = CAUSAL_BQ
    n_q = seq // bq
    k_norm_sq = functools.reduce(jnp.maximum, [side_ref[b, 1:2, :] for b in range(n_q)])

    def q_pair(j, outer):
        blocks = []
        for slot in range(2):
            i = 2 * j + slot
            qstart = pl.multiple_of(i * bq, bq)
            sb = _sb_ops(i, qstart, sq_ref, sk_ref, sv_ref, t_ref, so_ref, s_acc.at[slot], s_carry.at[slot])
            fox = _fox_ops(i, qstart, fq_ref, faq_ref, fk_ref, fak_ref, fv_ref, side_ref, fo_ref,
                           f_m.at[slot], f_l.at[slot], f_acc.at[slot])
            blocks.append((qstart, sb, fox))
        (q_even, sb_even, fox_even), (q_odd, sb_odd, fox_odd) = blocks

        for qstart, (sb_groups, _, _), _ in blocks:
            sb_groups(qstart, bq // SB_SUB, True)
        for _, _, (fox_diag, _, _, _) in blocks:
            fox_diag()
        sb_odd[0](q_even, bq // SB_SUB, False)
        fox_odd[1](q_even, bq)
        s_live[0] = 0
        s_live[1] = jnp.where(sb_odd[1]() >= -SB_DEAD, 1, 0)
        f_live[0] = 0

        def fox_trip(start, first_blk):
            for _, _, (_, fox_step, _, _) in blocks:
                fox_step(start, 2 * bq)
            nxt = jnp.maximum(first_blk - 1, 0)
            worst = jnp.maximum(fox_even[2](nxt, k_norm_sq), fox_odd[2](nxt, k_norm_sq))
            f_live[0] = jnp.where(worst >= -FOX_DEAD, 1, 0)

        def sb_half(slot, start, half):
            sb_groups, sb_top, _ = blocks[slot][1]
            sb_groups(pl.multiple_of(start + half * bq, bq), bq // SB_SUB, False)
            s_live[slot] = jnp.where(sb_top() >= -SB_DEAD, 1, 0)

        def sb_rest(start, todo):
            for half, slot in todo:
                pl.when(s_live[slot] == 1)(functools.partial(sb_half, slot, start, half))

        @pl.when(j > 0)
        def _():
            first_blk = 2 * (j - 1)
            start = pl.multiple_of(first_blk * bq, 2 * bq)
            fox_trip(start, first_blk)
            sb_half(0, start, 1)
            sb_rest(start, ((1, 1), (0, 0), (0, 1)))

        def cond(state):
            t, sb_on, fox_on = state
            return (t < j) & ((sb_on == 1) | (fox_on == 1))

        def body(state):
            t = state[0]
            first_blk = 2 * (j - 1 - t)
            start = pl.multiple_of(first_blk * bq, 2 * bq)
            pl.when(f_live[0] == 1)(functools.partial(fox_trip, start, first_blk))
            sb_rest(start, ((1, 0), (1, 1), (0, 0), (0, 1)))
            return t + 1, s_live[0] | s_live[1], f_live[0]

        lax.while_loop(cond, body, (jnp.int32(1), s_live[0] | s_live[1], f_live[0]))
        for _, (_, _, sb_finish), (_, _, _, fox_finish) in blocks:
            sb_finish()
            fox_finish()
        return outer

    lax.fori_loop(0, n_q // 2, q_pair, 0)


def _sb_tail_matrix():
    sub = SB_SUB
    j = np.arange(2 * sub)[:, None] % sub
    s = np.arange(2 * sub)[None, :]
    return np.where(s < sub, j > s, True).astype(np.float32)


def _causal_attn(main3, aug3, side4):
    b, s, _ = main3.shape
    tmat = jnp.asarray(_sb_tail_matrix(), BF16)
    state = pltpu.VMEM((2, N_PAIRS, 2 * CAUSAL_BQ, LANES), F32)
    flag = pltpu.SMEM((2,), jnp.int32)
    out = jax.ShapeDtypeStruct((b, s, GROUP), BF16)
    return pl.pallas_call(
        functools.partial(_causal_kernel, seq=s),
        grid=(b,),
        in_specs=[_group_spec(s, 0), _group_spec(s, 1), _group_spec(s, 2), _resident(tmat.shape),
                  _group_spec(s, 6), _group_spec(s, 0), _group_spec(s, 7), _group_spec(s, 1), _group_spec(s, 8),
                  pl.BlockSpec((None,) + side4.shape[1:], lambda bi: (bi, 0, 0, 0))],
        out_specs=[_group_spec(s, 0), _group_spec(s, 0)],
        out_shape=[out, out],
        scratch_shapes=[state, pltpu.VMEM((2, N_PAIRS, 2 * CAUSAL_BQ, SB_SUB), F32), flag,
                        state, state, state, flag],
        compiler_params=_params(1),
        name="causal_attn",
    )(main3, main3, main3, tmat, main3, aug3, main3, aug3, main3, side4)


def _dil_kernel(*refs, seq):
    n_pat = len(DIL_PATTERNS)
    qkv = refs[:3 * n_pat]
    bias_ref = refs[3 * n_pat]
    o_ref = refs[3 * n_pat + 1]
    scratch = refs[3 * n_pat + 2:]
    n_state = n_pat * N_PAIRS
    u_refs, m_refs, l_refs = scratch[0:n_state], scratch[n_state:2 * n_state], scratch[2 * n_state:3 * n_state]
    blk = DIL_BLOCK
    lane = lax.broadcasted_iota(jnp.int32, (blk, LANES), 1)

    def block(p, pidx, dil, c, n, first):
        q_ref, k_ref, v_ref = qkv[3 * pidx:3 * pidx + 3]
        lanes = _pair_lanes(p)

        def rd(ref, start, size):
            return ref[pl.ds(start, size), lanes] if dil == 1 else ref[c, pl.ds(start, size), lanes]

        qstart = n * blk if isinstance(n, int) else pl.multiple_of(n * blk, blk)
        qs = _split_heads_rows(rd(q_ref, qstart, blk), lane) * SCALE
        h0, h1 = 2 * p, 2 * p + 1
        if first:
            keys, vals = rd(k_ref, 0, blk), rd(v_ref, 0, blk)
            bias = jnp.concatenate([bias_ref[pidx, h0, :, blk:], bias_ref[pidx, h1, :, blk:]], axis=0)
        else:
            kstart = (n - 1) * blk if isinstance(n, int) else pl.multiple_of((n - 1) * blk, blk)
            keys, vals = rd(k_ref, kstart, 2 * blk), rd(v_ref, kstart, 2 * blk)
            bias = jnp.concatenate([bias_ref[pidx, h0], bias_ref[pidx, h1]], axis=0)
        z = _dot_nt(qs, keys) + bias
        m = jnp.max(z, axis=1, keepdims=True)
        pr = jnp.exp(z - m)
        wide = lambda a: jnp.broadcast_to(a, (2 * blk, LANES))
        if (seq // dil) // blk == 1:
            den = wide(jnp.sum(pr, axis=1, keepdims=True))
            u = _dot(pr.astype(BF16), vals)
        else:
            ones = jnp.ones((vals.shape[0], LANES), BF16)
            u = _dot(pr.astype(BF16), jnp.concatenate([vals, ones], axis=1))
            u, den = u[:, :LANES], u[:, LANES:]
        idx = pl.ds(n * (blk * dil) + c, blk, stride=dil) if dil > 1 else pl.ds(qstart, blk)
        st = pidx * N_PAIRS + p
        u_refs[st][idx, :] = _merge_heads_rows(u, lane, blk)
        m_refs[st][idx, :] = _merge_heads_rows(wide(m), lane, blk)
        l_refs[st][idx, :] = _merge_heads_rows(den, lane, blk)

    def blocks(pidx, dil, c, n, first):
        for p in range(N_PAIRS):
            block(p, pidx, dil, c, n, first)

    for pidx, (_, dil) in enumerate(DIL_PATTERNS):
        n_blk = (seq // dil) // blk
        if dil == 1:
            blocks(pidx, dil, 0, 0, True)

            def body(g, carry, pidx=pidx, dil=dil):
                for jj in range(DIL_GEN_GROUP):
                    blocks(pidx, dil, 0, 1 + g * DIL_GEN_GROUP + jj, False)
                return carry
            lax.fori_loop(0, (n_blk - 1) // DIL_GEN_GROUP, body, 0)
        elif n_blk > 1:
            def body(g, carry, pidx=pidx, dil=dil, n_blk=n_blk):
                for jj in range(DIL_MID_GROUP):
                    c = g * DIL_MID_GROUP + jj
                    blocks(pidx, dil, c, 0, True)
                    for n in range(1, n_blk):
                        blocks(pidx, dil, c, n, False)
                return carry
            lax.fori_loop(0, dil // DIL_MID_GROUP, body, 0)
        else:
            def body(g, carry, pidx=pidx, dil=dil):
                for jj in range(DIL_RES_GROUP):
                    blocks(pidx, dil, g * DIL_RES_GROUP + jj, 0, True)
                return carry
            lax.fori_loop(0, dil // DIL_RES_GROUP, body, 0)

    def comb(j, carry):
        sl = pl.ds(pl.multiple_of(j * COMB_ROWS, COMB_ROWS), COMB_ROWS)
        for p in range(N_PAIRS):
            sts = [pidx * N_PAIRS + p for pidx in range(n_pat)]
            ms = [m_refs[st][sl, :] for st in sts]
            m_all = functools.reduce(jnp.maximum, ms)
            es = [jnp.exp(mp - m_all) for mp in ms]
            num = sum(e * u_refs[st][sl, :] for e, st in zip(es, sts))
            den = sum(e * l_refs[st][sl, :] for e, st in zip(es, sts))
            o_ref[sl, _pair_lanes(p)] = (num / den).astype(BF16)
        return carry
    lax.fori_loop(0, seq // COMB_ROWS, comb, 0)


def _t5_bucket_of(dist):
    max_exact = REL_BUCKETS // 2
    nf = jnp.maximum(dist, 1).astype(jnp.float32)
    large = max_exact + (jnp.log(nf / max_exact) / math.log(REL_MAX_DIST / max_exact)
                         * (REL_BUCKETS - max_exact)).astype(jnp.int32)
    large = jnp.minimum(large, REL_BUCKETS - 1)
    return jnp.where(dist < max_exact, dist, large)


def _dil_bias_tables(rel_bias):
    blk = DIL_BLOCK
    n_off = 3 * blk - 1
    sub = (2 * blk - 1) - jnp.arange(n_off)
    tables = []
    for window, dil in DIL_PATTERNS:
        in_band = (sub >= 0) & (sub <= window // dil)
        bucket = _t5_bucket_of(jnp.maximum(sub, 0) * dil)
        hit = bucket[None, :, None] == jnp.arange(REL_BUCKETS)
        bias = jnp.sum(jnp.where(hit, rel_bias.T.astype(F32)[:, None, :], 0.0), axis=-1)
        line = jnp.where(in_band[None], bias, NEG)
        wide = jnp.broadcast_to(jnp.pad(line, ((0, 0), (0, 1)))[:, None, :], (line.shape[0], blk, n_off + 1))
        skew = wide.reshape(line.shape[0], -1)[:, blk - 1:blk - 1 + blk * n_off].reshape(line.shape[0], blk, n_off)
        tables.append(skew[:, :, :2 * blk])
    return jnp.stack(tables)


def _dil_attn(main3, dl4, dl16, bias_tables):
    b, s, _ = main3.shape
    n_pat = len(DIL_PATTERNS)
    operands, in_specs = [], []
    for role in range(3):
        operands.append(main3)
        in_specs.append(_group_spec(s, DIL_GROUP0 // N_PAIRS + role))
    for arr in (dl4, dl16):
        dil, rows = arr.shape[1], arr.shape[2]
        for role in range(3):
            operands.append(arr)
            in_specs.append(pl.BlockSpec((None, dil, rows, GROUP), lambda bi, role=role: (bi, 0, 0, role)))
    operands.append(bias_tables)
    in_specs.append(_resident(bias_tables.shape))
    return pl.pallas_call(
        functools.partial(_dil_kernel, seq=s),
        grid=(b,),
        in_specs=in_specs,
        out_specs=_group_spec(s, 0),
        out_shape=jax.ShapeDtypeStruct((b, s, GROUP), BF16),
        scratch_shapes=[pltpu.VMEM((s, LANES), F32)] * (3 * n_pat * N_PAIRS),
        compiler_params=_params(1),
        name="dil_attn",
    )(*operands)


def _layer_norm(v, g, b):
    mu = jnp.mean(v, axis=-1, keepdims=True)
    d = v - mu
    var = jnp.mean(d * d, axis=-1, keepdims=True)
    return d * lax.rsqrt(var + LN_EPS) * g + b


def _dense_kernel(x_ref, oa_ref, ob_ref, oc_ref, od_ref, wo_ref, g1_ref, b1_ref,
                  wg_ref, wu_ref, wd_ref, g2_ref, b2_ref, out_ref, h_ref, *, alpha):
    tm = x_ref.shape[0]
    d_ff = wg_ref.shape[1]
    halves = [slice(r0, r0 + tm // DENSE_SPLIT) for r0 in range(0, tm, tm // DENSE_SPLIT)]
    mixes = [(_dot(oa_ref[rows, :], wo_ref[0:GROUP, :]) + _dot(ob_ref[rows, :], wo_ref[GROUP:2 * GROUP, :])
              + _dot(oc_ref[rows, :], wo_ref[2 * GROUP:3 * GROUP, :])
              + _dot(od_ref[rows, :], wo_ref[3 * GROUP:4 * GROUP, :])) for rows in halves]
    x1s = [_layer_norm(alpha * x_ref[rows, :] + mix, g1_ref[...], b1_ref[...]) for rows, mix in zip(halves, mixes)]
    xbs = [x1.astype(BF16) for x1 in x1s]
    for c0 in range(0, d_ff, FFN_CHUNK):
        for rows, xb in zip(halves, xbs):
            g = _dot(xb, wg_ref[:, c0:c0 + FFN_CHUNK])
            u = _dot(xb, wu_ref[:, c0:c0 + FFN_CHUNK])
            h_ref[rows, c0:c0 + FFN_CHUNK] = (g * (1.0 / (1.0 + jnp.exp(-g))) * u).astype(BF16)
    ys = [_dot(h_ref[rows, :], wd_ref[...]) for rows in halves]
    for rows, x1, y in zip(halves, x1s, ys):
        out_ref[rows, :] = _layer_norm(alpha * x1 + y, g2_ref[...], b2_ref[...])


def _dense(x2d, outs, layer, w_out, g1, b1, w_gate, w_up, w_down, g2, b2, alpha):
    m, d = x2d.shape
    tm = DENSE_TM
    d_ff = w_gate.shape[2]
    row = lambda i: (i, 0)
    o_spec = pl.BlockSpec((tm, GROUP), row)
    params = (w_out, g1, b1, w_gate, w_up, w_down, g2, b2)
    return pl.pallas_call(
        functools.partial(_dense_kernel, alpha=alpha),
        grid=(m // tm,),
        in_specs=[pl.BlockSpec((tm, d), row), o_spec, o_spec, o_spec, o_spec]
        + [_layer_resident(a.shape, layer) for a in params],
        out_specs=pl.BlockSpec((tm, d), row),
        out_shape=jax.ShapeDtypeStruct((m, d), F32),
        scratch_shapes=[pltpu.VMEM((tm, d_ff), BF16)],
        compiler_params=_params(1),
        name="dense",
    )(x2d, *outs, *params)


def kernel(x, w_in, f_bias, conv_w, w_out, rel_bias, ln1_g, ln1_b, w_gate, w_up, w_down, ln2_g, ln2_b):
    b, s, d = x.shape
    depth = w_in.shape[0]
    assert d == 4 * GROUP and w_in.shape[2] == 12 * GROUP + N_HEADS
    assert s % PROJ_TM == 0 and s % (2 * CAUSAL_BQ) == 0 and s % COMB_ROWS == 0
    assert [dil for _, dil in DIL_PATTERNS][0] == 1
    for _, dil in DIL_PATTERNS:
        n_blk = (s // dil) // DIL_BLOCK
        assert n_blk * DIL_BLOCK * dil == s and PROJ_TM % (16 * dil) == 0
        assert ((n_blk - 1) % DIL_GEN_GROUP == 0 if dil == 1 else
                dil % (DIL_MID_GROUP if n_blk > 1 else DIL_RES_GROUP) == 0)
    alpha = (2 * depth) ** 0.25
    bias_tables = _dil_bias_tables(rel_bias)
    w_pad = jnp.pad(jnp.swapaxes(w_in, 1, 2), ((0, 0), (0, LANES - N_HEADS), (0, 0))).astype(BF16)
    dense_params = (w_out.astype(BF16), ln1_g[:, None, :], ln1_b[:, None, :], w_gate.astype(BF16),
                    w_up.astype(BF16), w_down.astype(BF16), ln2_g[:, None, :], ln2_b[:, None, :])
    x2d = x.reshape(b * s, d)
    for layer in range(depth):
        main, dl4, dl16, aug, out_d, side = _in_proj(x2d, w_pad, layer, f_bias[layer], conv_w[layer], b, s)
        main3 = main.reshape(b, s, main.shape[1])
        out_a, out_c = _causal_attn(main3, aug.reshape(b, s, aug.shape[1]),
                                    side.reshape((b, s // CAUSAL_BQ) + side.shape[1:]))
        out_b = _dil_attn(main3, dl4, dl16, bias_tables)
        outs = [o.reshape(b * s, GROUP) for o in (out_a, out_b, out_c)] + [out_d]
        x2d = _dense(x2d, outs, layer, *dense_params, alpha)
    return x2d.reshape(b, s, d)
```

```python
import functools
import math

import jax
import jax.numpy as jnp
import numpy as np
from jax import lax
from jax.experimental import pallas as pl
from jax.experimental.pallas import tpu as pltpu

HEAD_DIM = 64
N_HEADS = 4
GROUP = N_HEADS * HEAD_DIM
LANES = 128
N_PAIRS = GROUP // LANES
CONV_K = 3
DIL_PATTERNS = ((128, 1), (512, 4), (2048, 16))
DIL_BLOCK = 128
DIL_GROUP0 = 6
REL_BUCKETS = 32
REL_MAX_DIST = 2048
LN_EPS = 1e-5
SCALE = HEAD_DIM ** -0.5
NEG = -1e30
SIGN_BIT = np.uint32(0x80000000)
VMEM_LIMIT = 56 * 1024 * 1024

BF16 = jnp.bfloat16
F32 = jnp.float32

PROJ_TM = 512
X_RING = 3
DENSE_TM = 1024
DENSE_SPLIT = 4
FFN_CHUNK = 256
CAUSAL_BQ = 256
SB_SUB = 128
SB_DEAD = 120.0
FOX_Q_COL = 6 * GROUP
FOX_K_COL = 7 * GROUP
FOX_DEAD = 120.0
FOX_NORM_SLACK = 1.02
DIL_GEN_GROUP = 15
DIL_MID_GROUP = 4
DIL_RES_GROUP = 8
COMB_ROWS = 256


def _dot(a, b):
    return jnp.dot(a, b, preferred_element_type=F32)


def _dot_nt(a, b):
    return lax.dot_general(a, b, (((1,), (1,)), ((), ())), preferred_element_type=F32)


def _resident(shape):
    nd = len(shape)
    return pl.BlockSpec(shape, lambda *_: (0,) * nd, pipeline_mode=pl.Buffered(1))


def _layer_resident(stacked_shape, layer):
    nd = len(stacked_shape) - 1
    return pl.BlockSpec((None,) + tuple(stacked_shape[1:]), lambda *_: (layer,) + (0,) * nd,
                        pipeline_mode=pl.Buffered(1))


def _params(n_axes):
    return pltpu.CompilerParams(dimension_semantics=("arbitrary",) * n_axes,
                                vmem_limit_bytes=VMEM_LIMIT)


def _split_heads_rows(q, lane):
    zero = jnp.zeros_like(q)
    return jnp.concatenate([jnp.where(lane < HEAD_DIM, q, zero),
                            jnp.where(lane >= HEAD_DIM, q, zero)], axis=0)


def _merge_heads_rows(a, lane, rows):
    return jnp.where(lane < HEAD_DIM, a[:rows], a[rows:])


def _pair_lanes(p):
    return slice(p * LANES, (p + 1) * LANES)


def _group_spec(seq, g):
    return pl.BlockSpec((None, seq, GROUP), lambda bi: (bi, 0, g))


def _in_proj_kernel(x_hbm, w_ref, fb_ref, cw_ref, sel_ref,
                    main_ref, dl4_ref, dl16_ref, aug_ref, od_ref, side_ref,
                    carry_ref, ubuf_ref, utail_ref, xbuf_ref, xsem, *stage_refs,
                    tiles_per_seq):
    tm = xbuf_ref.shape[1]
    step = pl.program_id(0)
    n_steps = pl.num_programs(0)
    first = (step % tiles_per_seq) == 0

    def x_copy(s):
        start = s * tm if isinstance(s, int) else pl.multiple_of(s * tm, tm)
        slot = s % X_RING
        return pltpu.make_async_copy(x_hbm.at[pl.ds(start, tm), :], xbuf_ref.at[slot], xsem.at[slot])

    @pl.when(step == 0)
    def _():
        x_copy(0).start()
        x_copy(1).start()

    @pl.when(step + 2 < n_steps)
    def _():
        x_copy(step + 2).start()

    x_copy(step).wait()
    xb = xbuf_ref[step % X_RING].astype(BF16)

    n_main = main_ref.shape[1]
    cv0 = n_main
    gate0 = n_main + 3 * GROUP
    cvb = _dot_nt(xb, w_ref[cv0:cv0 + GROUP, :])
    cvc = _dot_nt(xb, w_ref[cv0 + GROUP:cv0 + 2 * GROUP, :])
    cvh = _dot_nt(xb, w_ref[cv0 + 2 * GROUP:cv0 + 3 * GROUP, :])
    u = cvc * cvh

    ubuf_ref[0:8, :] = jnp.where(first, 0.0, utail_ref[...])
    ubuf_ref[8:8 + tm, :] = u
    y = (cw_ref[0:1, :] * ubuf_ref[6:6 + tm, :] + cw_ref[1:2, :] * ubuf_ref[7:7 + tm, :]
         + cw_ref[2:3, :] * u)
    od_ref[...] = (cvb * y).astype(BF16)
    utail_ref[...] = u[tm - 8:tm, :]

    dl_col0 = DIL_GROUP0 * LANES
    row_sq = {}
    for c0 in range(0, n_main, GROUP):
        r = _dot_nt(xb, w_ref[c0:c0 + GROUP, :])
        main_ref[:, c0:c0 + GROUP] = r.astype(BF16)
        if c0 in (FOX_Q_COL, FOX_K_COL):
            row_sq[c0] = jnp.sum(r * r, axis=1, keepdims=True)
        if dl_col0 <= c0 < dl_col0 + 3 * GROUP:
            j = (c0 - dl_col0) // LANES
            stage_refs[j][...] = r[:, :LANES]
            stage_refs[j + 1][...] = r[:, LANES:]

    for dil, ref in ((DIL_PATTERNS[1][1], dl4_ref), (DIL_PATTERNS[2][1], dl16_ref)):
        rows = tm // dil
        for c in range(dil):
            for j in range(len(stage_refs)):
                ref[c, :, j * LANES:(j + 1) * LANES] = (
                    stage_refs[j][pl.ds(c, rows, stride=dil), :].astype(BF16))

    g = _dot_nt(xb, w_ref[gate0:gate0 + LANES, :]) + fb_ref[...]
    logf = jnp.minimum(g, 0.0) - jnp.log(1.0 + jnp.exp(-jnp.abs(g)))
    r_i = lax.broadcasted_iota(jnp.int32, (tm, tm), 0)
    c_i = lax.broadcasted_iota(jnp.int32, (tm, tm), 1)
    tri = jnp.where(c_i <= r_i, 1.0, 0.0).astype(BF16)
    l_hi = logf.astype(BF16)
    l_lo = (logf - l_hi.astype(F32)).astype(BF16)
    both = _dot(tri, jnp.concatenate([l_hi, l_lo], axis=1))
    csum = both[:, :LANES] + both[:, LANES:]
    prev = jnp.where(first, 0.0, carry_ref[...])
    csum = csum + prev
    carry_ref[...] = csum[tm - 1:tm, :]
    for b in range(tm // CAUSAL_BQ):
        rows = slice(b * CAUSAL_BQ, (b + 1) * CAUSAL_BQ)
        for j, c0 in enumerate((FOX_Q_COL, FOX_K_COL)):
            side_ref[b, j:j + 1, :] = jnp.broadcast_to(jnp.max(row_sq[c0][rows], axis=0, keepdims=True), (1, LANES))
        side_ref[b, 2:3, :] = csum[b * CAUSAL_BQ:b * CAUSAL_BQ + 1, :]
        side_ref[b, 3:4, :] = csum[(b + 1) * CAUSAL_BQ - 1:(b + 1) * CAUSAL_BQ, :]
        side_ref[b, 4:8, :] = jnp.zeros((4, LANES), F32)
    c_hi = csum.astype(BF16)
    c_r = csum - c_hi.astype(F32)
    c_mid = c_r.astype(BF16)
    c_lo = (c_r - c_mid.astype(F32)).astype(BF16)
    ones = jnp.ones((tm, LANES), BF16)
    aug_ref[...] = _dot(jnp.concatenate([c_hi, c_mid, c_lo, ones], axis=1), sel_ref[...]).astype(BF16)


def _aug_selection():
    sel = np.zeros((4 * LANES, 4 * LANES), np.float32)
    ones_row = 3 * LANES
    for h in range(N_HEADS):
        bq = (h // 2) * LANES + (h % 2) * HEAD_DIM
        bk = 2 * LANES + bq
        for part in range(3):
            sel[part * LANES + h, bq + part] = 1.0
            sel[ones_row, bq + 3 + part] = 1.0
            sel[ones_row, bk + part] = 1.0
            sel[part * LANES + h, bk + 3 + part] = -1.0
    return sel


def _in_proj(x2d, w_pad, layer, f_bias, conv_w, batch, seq):
    m, d = x2d.shape
    tm = PROJ_TM
    tps = seq // tm
    n_main = 9 * GROUP
    d4, d16 = DIL_PATTERNS[1][1], DIL_PATTERNS[2][1]
    fb = jnp.pad(f_bias, (0, LANES - N_HEADS))[None, :]
    cw = jnp.pad(conv_w, ((0, 8 - CONV_K), (0, 0)))
    sel = jnp.asarray(_aug_selection(), BF16)
    row = lambda i: (i, 0)
    res = lambda i: (i // tps, 0, i % tps, 0)
    return pl.pallas_call(
        functools.partial(_in_proj_kernel, tiles_per_seq=tps),
        grid=(m // tm,),
        in_specs=[pl.BlockSpec(memory_space=pl.ANY), _layer_resident(w_pad.shape, layer),
                  _resident(fb.shape), _resident(cw.shape), _resident(sel.shape)],
        out_specs=[pl.BlockSpec((tm, n_main), row),
                   pl.BlockSpec((None, d4, tm // d4, 3 * GROUP), res),
                   pl.BlockSpec((None, d16, tm // d16, 3 * GROUP), res),
                   pl.BlockSpec((tm, 4 * LANES), row),
                   pl.BlockSpec((tm, GROUP), row),
                   pl.BlockSpec((tm // CAUSAL_BQ, 8, LANES), lambda i: (i, 0, 0))],
        out_shape=[jax.ShapeDtypeStruct((m, n_main), BF16),
                   jax.ShapeDtypeStruct((batch, d4, seq // d4, 3 * GROUP), BF16),
                   jax.ShapeDtypeStruct((batch, d16, seq // d16, 3 * GROUP), BF16),
                   jax.ShapeDtypeStruct((m, 4 * LANES), BF16),
                   jax.ShapeDtypeStruct((m, GROUP), BF16),
                   jax.ShapeDtypeStruct((m // CAUSAL_BQ, 8, LANES), F32)],
        scratch_shapes=[pltpu.VMEM((1, LANES), F32), pltpu.VMEM((tm + 8, GROUP), F32),
                        pltpu.VMEM((8, GROUP), F32),
                        pltpu.VMEM((X_RING, tm, d), F32), pltpu.SemaphoreType.DMA((X_RING,))]
        + [pltpu.VMEM((tm, LANES), F32)] * (3 * N_PAIRS),
        compiler_params=_params(1),
        name="in_proj",
    )(x2d, w_pad, fb, cw, sel)


def _sb_ops(i, qstart, q_ref, k_ref, v_ref, t_ref, o_ref, acc_ref, carry_ref):
    bq, sub = CAUSAL_BQ, SB_SUB
    lane_q = lax.broadcasted_iota(jnp.int32, (bq, LANES), 1)
    qss = []
    for p in range(N_PAIRS):
        qss.append(_split_heads_rows(q_ref[pl.ds(qstart, bq), _pair_lanes(p)], lane_q) * SCALE)
        acc_ref[p] = jnp.zeros((2 * bq, LANES), F32)
        carry_ref[p] = jnp.zeros((2 * bq, sub), F32)

    def scores(p, start, g):
        return _dot_nt(qss[p], k_ref[pl.ds(start, g * sub), _pair_lanes(p)])

    def group(p, start, g, masked, z):
        n = g * sub
        neg_abs = pltpu.bitcast(pltpu.bitcast(z, jnp.uint32) | SIGN_BIT, F32)
        sp = jnp.log(1.0 + jnp.exp(neg_abs))
        log_beta = jnp.minimum(z, 0.0) - sp
        log_rest = log_beta - z
        if masked:
            row = lax.broadcasted_iota(jnp.int32, (2 * bq, n), 0)
            col = lax.broadcasted_iota(jnp.int32, (2 * bq, n), 1)
            strict = (start + col) < (qstart + jnp.where(row >= bq, row - bq, row))
            log_rest = jnp.where(strict, log_rest, 0.0)
        hi = log_rest.astype(BF16)
        lo = (log_rest - hi.astype(F32)).astype(BF16)
        carry = carry_ref[p]
        parts = [None] * g
        for j in reversed(range(g)):
            sl = slice(j * sub, (j + 1) * sub)
            cs = _dot(jnp.concatenate([hi[:, sl], lo[:, sl]], axis=1), t_ref[...])
            a = jnp.exp(log_beta[:, sl] + (cs[:, :sub] + carry))
            carry = carry + cs[:, sub:]
            if masked:
                a = jnp.where(strict[:, sl], a, 0.0)
            parts[j] = a.astype(BF16)
        carry_ref[p] = carry
        acc_ref[p] += _dot(jnp.concatenate(parts, axis=1), v_ref[pl.ds(start, n), _pair_lanes(p)])

    def groups(start, g, masked):
        for p in range(N_PAIRS):
            group(p, start, g, masked, scores(p, start, g))

    def top():
        return functools.reduce(jnp.maximum, [jnp.max(carry_ref[p]) for p in range(N_PAIRS)])

    def finish():
        for p in range(N_PAIRS):
            o_ref[pl.ds(qstart, bq), _pair_lanes(p)] = _merge_heads_rows(acc_ref[p], lane_q, bq).astype(BF16)

    return groups, top, finish


def _fox_ops(i, qstart, q_ref, aq_ref, k_ref, ak_ref, v_ref, side_ref, o_ref, m_ref, l_ref, acc_ref):
    bq = CAUSAL_BQ
    lane_q = lax.broadcasted_iota(jnp.int32, (bq, LANES), 1)

    def wide(a):
        return jnp.broadcast_to(a, (2 * bq, LANES))

    qss = [jnp.concatenate(
        [_split_heads_rows(q_ref[pl.ds(qstart, bq), _pair_lanes(p)], lane_q) * SCALE,
         _split_heads_rows(aq_ref[pl.ds(qstart, bq), _pair_lanes(p)], lane_q)], axis=1)
        for p in range(N_PAIRS)]

    def scores(p, start, n):
        rows = pl.ds(start, n)
        kk = jnp.concatenate([k_ref[rows, _pair_lanes(p)], ak_ref[rows, _pair_lanes(p)]], axis=1)
        return _dot_nt(qss[p], kk)

    def values(p, start, n):
        return v_ref[pl.ds(start, n), _pair_lanes(p)]

    def diag():
        row = lax.broadcasted_iota(jnp.int32, (2 * bq, bq), 0)
        col = lax.broadcasted_iota(jnp.int32, (2 * bq, bq), 1)
        causal = col <= jnp.where(row >= bq, row - bq, row)
        for p in range(N_PAIRS):
            z = scores(p, qstart, bq)
            vs = values(p, qstart, bq)
            z = jnp.where(causal, z, NEG)
            m0 = jnp.max(z, axis=1, keepdims=True)
            pr = jnp.exp(z - m0)
            m_ref[p] = wide(m0)
            l_ref[p] = wide(jnp.sum(pr, axis=1, keepdims=True))
            acc_ref[p] = _dot(pr.astype(BF16), vs)

    def step(start, n):
        for p in range(N_PAIRS):
            z = scores(p, start, n)
            vs = values(p, start, n)
            m_old = m_ref[p]
            m_new = jnp.maximum(m_old, jnp.max(z, axis=1, keepdims=True))
            alpha = jnp.exp(m_old - m_new)
            pr = jnp.exp(z - jnp.concatenate([m_new] * (n // LANES), axis=1))
            l_ref[p] = alpha * l_ref[p] + jnp.sum(pr, axis=1, keepdims=True)
            acc_ref[p] = alpha * acc_ref[p] + _dot(pr.astype(BF16), vs)
            m_ref[p] = m_new

    def margin(end_blk, k_norm_sq):
        q_side = side_ref[i]
        k_side = side_ref[end_blk]
        qk = (0.5 * SCALE * FOX_NORM_SLACK) * (q_side[0:1, :] + k_norm_sq)
        lane = lax.broadcasted_iota(jnp.int32, (1, LANES), 1)
        top = jnp.max(jnp.where(lane < N_HEADS, qk + (q_side[2:3, :] - k_side[3:4, :]), NEG))
        m_min = functools.reduce(jnp.minimum, [jnp.min(m_ref[p]) for p in range(N_PAIRS)])
        return top - m_min

    def finish():
        for p in range(N_PAIRS):
            o_ref[pl.ds(qstart, bq), _pair_lanes(p)] = _merge_heads_rows(
                acc_ref[p] / l_ref[p], lane_q, bq).astype(BF16)

    return diag, step, margin, finish


def _causal_kernel(sq_ref, sk_ref, sv_ref, t_ref, fq_ref, faq_ref, fk_ref, fak_ref, fv_ref, side_ref,
                   so_ref, fo_ref, s_acc, s_carry, s_live, f_m, f_l, f_acc, f_live, *, seq):
    bq = CAUSAL_BQ
    n_q = seq // bq
    k_norm_sq = functools.reduce(jnp.maximum, [side_ref[b, 1:2, :] for b in range(n_q)])

    def q_pair(j, outer):
        blocks = []
        for slot in range(2):
            i = 2 * j + slot
            qstart = pl.multiple_of(i * bq, bq)
            sb = _sb_ops(i, qstart, sq_ref, sk_ref, sv_ref, t_ref, so_ref, s_acc.at[slot], s_carry.at[slot])
            fox = _fox_ops(i, qstart, fq_ref, faq_ref, fk_ref, fak_ref, fv_ref, side_ref, fo_ref,
                           f_m.at[slot], f_l.at[slot], f_acc.at[slot])
            blocks.append((qstart, sb, fox))
        (q_even, sb_even, fox_even), (q_odd, sb_odd, fox_odd) = blocks

        for qstart, (sb_groups, _, _), _ in blocks:
            sb_groups(qstart, bq // SB_SUB, True)
        for _, _, (fox_diag, _, _, _) in blocks:
            fox_diag()
        sb_odd[0](q_even, bq // SB_SUB, False)
        fox_odd[1](q_even, bq)
        s_live[0] = 0
        s_live[1] = jnp.where(sb_odd[1]() >= -SB_DEAD, 1, 0)
        f_live[0] = 0

        def fox_trip(start, first_blk):
            for _, _, (_, fox_step, _, _) in blocks:
                fox_step(start, 2 * bq)
            nxt = jnp.maximum(first_blk - 1, 0)
            worst = jnp.maximum(fox_even[2](nxt, k_norm_sq), fox_odd[2](nxt, k_norm_sq))
            f_live[0] = jnp.where(worst >= -FOX_DEAD, 1, 0)

        def sb_half(slot, start, half):
            sb_groups, sb_top, _ = blocks[slot][1]
            sb_groups(pl.multiple_of(start + half * bq, bq), bq // SB_SUB, False)
            s_live[slot] = jnp.where(sb_top() >= -SB_DEAD, 1, 0)

        def sb_rest(start, todo):
            for half, slot in todo:
                pl.when(s_live[slot] == 1)(functools.partial(sb_half, slot, start, half))

        @pl.when(j > 0)
        def _():
            first_blk = 2 * (j - 1)
            start = pl.multiple_of(first_blk * bq, 2 * bq)
            fox_trip(start, first_blk)
            sb_half(0, start, 1)
            sb_rest(start, ((1, 1), (0, 0), (0, 1)))

        def cond(state):
            t, sb_on, fox_on = state
            return (t < j) & ((sb_on == 1) | (fox_on == 1))

        def body(state):
            t = state[0]
            first_blk = 2 * (j - 1 - t)
            start = pl.multiple_of(first_blk * bq, 2 * bq)
            pl.when(f_live[0] == 1)(functools.partial(fox_trip, start, first_blk))
            sb_rest(start, ((1, 0), (1, 1), (0, 0), (0, 1)))
            return t + 1, s_live[0] | s_live[1], f_live[0]

        lax.while_loop(cond, body, (jnp.int32(1), s_live[0] | s_live[1], f_live[0]))
        for _, (_, _, sb_finish), (_, _, _, fox_finish) in blocks:
            sb_finish()
            fox_finish()
        return outer

    lax.fori_loop(0, n_q // 2, q_pair, 0)


def _sb_tail_matrix():
    sub = SB_SUB
    j = np.arange(2 * sub)[:, None] % sub
    s = np.arange(2 * sub)[None, :]
    return np.where(s < sub, j > s, True).astype(np.float32)


def _causal_attn(main3, aug3, side4):
    b, s, _ = main3.shape
    tmat = jnp.asarray(_sb_tail_matrix(), BF16)
    state = pltpu.VMEM((2, N_PAIRS, 2 * CAUSAL_BQ, LANES), F32)
    flag = pltpu.SMEM((2,), jnp.int32)
    out = jax.ShapeDtypeStruct((b, s, GROUP), BF16)
    return pl.pallas_call(
        functools.partial(_causal_kernel, seq=s),
        grid=(b,),
        in_specs=[_group_spec(s, 0), _group_spec(s, 1), _group_spec(s, 2), _resident(tmat.shape),
                  _group_spec(s, 6), _group_spec(s, 0), _group_spec(s, 7), _group_spec(s, 1), _group_spec(s, 8),
                  pl.BlockSpec((None,) + side4.shape[1:], lambda bi: (bi, 0, 0, 0))],
        out_specs=[_group_spec(s, 0), _group_spec(s, 0)],
        out_shape=[out, out],
        scratch_shapes=[state, pltpu.VMEM((2, N_PAIRS, 2 * CAUSAL_BQ, SB_SUB), F32), flag,
                        state, state, state, flag],
        compiler_params=_params(1),
        name="causal_attn",
    )(main3, main3, main3, tmat, main3, aug3, main3, aug3, main3, side4)


def _dil_kernel(*refs, seq):
    n_pat = len(DIL_PATTERNS)
    qkv = refs[:3 * n_pat]
    bias_ref = refs[3 * n_pat]
    o_ref = refs[3 * n_pat + 1]
    scratch = refs[3 * n_pat + 2:]
    n_state = n_pat * N_PAIRS
    u_refs, m_refs, l_refs = scratch[0:n_state], scratch[n_state:2 * n_state], scratch[2 * n_state:3 * n_state]
    blk = DIL_BLOCK
    lane = lax.broadcasted_iota(jnp.int32, (blk, LANES), 1)

    def block(p, pidx, dil, c, n, first):
        q_ref, k_ref, v_ref = qkv[3 * pidx:3 * pidx + 3]
        lanes = _pair_lanes(p)

        def rd(ref, start, size):
            return ref[pl.ds(start, size), lanes] if dil == 1 else ref[c, pl.ds(start, size), lanes]

        qstart = n * blk if isinstance(n, int) else pl.multiple_of(n * blk, blk)
        qs = _split_heads_rows(rd(q_ref, qstart, blk), lane) * SCALE
        h0, h1 = 2 * p, 2 * p + 1
        if first:
            keys, vals = rd(k_ref, 0, blk), rd(v_ref, 0, blk)
            bias = jnp.concatenate([bias_ref[pidx, h0, :, blk:], bias_ref[pidx, h1, :, blk:]], axis=0)
        else:
            kstart = (n - 1) * blk if isinstance(n, int) else pl.multiple_of((n - 1) * blk, blk)
            keys, vals = rd(k_ref, kstart, 2 * blk), rd(v_ref, kstart, 2 * blk)
            bias = jnp.concatenate([bias_ref[pidx, h0], bias_ref[pidx, h1]], axis=0)
        z = _dot_nt(qs, keys) + bias
        m = jnp.max(z, axis=1, keepdims=True)
        pr = jnp.exp(z - m)
        wide = lambda a: jnp.broadcast_to(a, (2 * blk, LANES))
        if (seq // dil) // blk == 1:
            den = wide(jnp.sum(pr, axis=1, keepdims=True))
            u = _dot(pr.astype(BF16), vals)
        else:
            ones = jnp.ones((vals.shape[0], LANES), BF16)
            u = _dot(pr.astype(BF16), jnp.concatenate([vals, ones], axis=1))
            u, den = u[:, :LANES], u[:, LANES:]
        idx = pl.ds(n * (blk * dil) + c, blk, stride=dil) if dil > 1 else pl.ds(qstart, blk)
        st = pidx * N_PAIRS + p
        u_refs[st][idx, :] = _merge_heads_rows(u, lane, blk)
        m_refs[st][idx, :] = _merge_heads_rows(wide(m), lane, blk)
        l_refs[st][idx, :] = _merge_heads_rows(den, lane, blk)

    def blocks(pidx, dil, c, n, first):
        for p in range(N_PAIRS):
            block(p, pidx, dil, c, n, first)

    for pidx, (_, dil) in enumerate(DIL_PATTERNS):
        n_blk = (seq // dil) // blk
        if dil == 1:
            blocks(pidx, dil, 0, 0, True)

            def body(g, carry, pidx=pidx, dil=dil):
                for jj in range(DIL_GEN_GROUP):
                    blocks(pidx, dil, 0, 1 + g * DIL_GEN_GROUP + jj, False)
                return carry
            lax.fori_loop(0, (n_blk - 1) // DIL_GEN_GROUP, body, 0)
        elif n_blk > 1:
            def body(g, carry, pidx=pidx, dil=dil, n_blk=n_blk):
                for jj in range(DIL_MID_GROUP):
                    c = g * DIL_MID_GROUP + jj
                    blocks(pidx, dil, c, 0, True)
                    for n in range(1, n_blk):
                        blocks(pidx, dil, c, n, False)
                return carry
            lax.fori_loop(0, dil // DIL_MID_GROUP, body, 0)
        else:
            def body(g, carry, pidx=pidx, dil=dil):
                for jj in range(DIL_RES_GROUP):
                    blocks(pidx, dil, g * DIL_RES_GROUP + jj, 0, True)
                return carry
            lax.fori_loop(0, dil // DIL_RES_GROUP, body, 0)

    def comb(j, carry):
        sl = pl.ds(pl.multiple_of(j * COMB_ROWS, COMB_ROWS), COMB_ROWS)
        for p in range(N_PAIRS):
            sts = [pidx * N_PAIRS + p for pidx in range(n_pat)]
            ms = [m_refs[st][sl, :] for st in sts]
            m_all = functools.reduce(jnp.maximum, ms)
            es = [jnp.exp(mp - m_all) for mp in ms]
            num = sum(e * u_refs[st][sl, :] for e, st in zip(es, sts))
            den = sum(e * l_refs[st][sl, :] for e, st in zip(es, sts))
            o_ref[sl, _pair_lanes(p)] = (num / den).astype(BF16)
        return carry
    lax.fori_loop(0, seq // COMB_ROWS, comb, 0)


def _t5_bucket_of(dist):
    max_exact = REL_BUCKETS // 2
    nf = jnp.maximum(dist, 1).astype(jnp.float32)
    large = max_exact + (jnp.log(nf / max_exact) / math.log(REL_MAX_DIST / max_exact)
                         * (REL_BUCKETS - max_exact)).astype(jnp.int32)
    large = jnp.minimum(large, REL_BUCKETS - 1)
    return jnp.where(dist < max_exact, dist, large)


def _dil_bias_tables(rel_bias):
    blk = DIL_BLOCK
    n_off = 3 * blk - 1
    sub = (2 * blk - 1) - jnp.arange(n_off)
    tables = []
    for window, dil in DIL_PATTERNS:
        in_band = (sub >= 0) & (sub <= window // dil)
        bucket = _t5_bucket_of(jnp.maximum(sub, 0) * dil)
        hit = bucket[None, :, None] == jnp.arange(REL_BUCKETS)
        bias = jnp.sum(jnp.where(hit, rel_bias.T.astype(F32)[:, None, :], 0.0), axis=-1)
        line = jnp.where(in_band[None], bias, NEG)
        wide = jnp.broadcast_to(jnp.pad(line, ((0, 0), (0, 1)))[:, None, :], (line.shape[0], blk, n_off + 1))
        skew = wide.reshape(line.shape[0], -1)[:, blk - 1:blk - 1 + blk * n_off].reshape(line.shape[0], blk, n_off)
        tables.append(skew[:, :, :2 * blk])
    return jnp.stack(tables)


def _dil_attn(main3, dl4, dl16, bias_tables):
    b, s, _ = main3.shape
    n_pat = len(DIL_PATTERNS)
    operands, in_specs = [], []
    for role in range(3):
        operands.append(main3)
        in_specs.append(_group_spec(s, DIL_GROUP0 // N_PAIRS + role))
    for arr in (dl4, dl16):
        dil, rows = arr.shape[1], arr.shape[2]
        for role in range(3):
            operands.append(arr)
            in_specs.append(pl.BlockSpec((None, dil, rows, GROUP), lambda bi, role=role: (bi, 0, 0, role)))
    operands.append(bias_tables)
    in_specs.append(_resident(bias_tables.shape))
    return pl.pallas_call(
        functools.partial(_dil_kernel, seq=s),
        grid=(b,),
        in_specs=in_specs,
        out_specs=_group_spec(s, 0),
        out_shape=jax.ShapeDtypeStruct((b, s, GROUP), BF16),
        scratch_shapes=[pltpu.VMEM((s, LANES), F32)] * (3 * n_pat * N_PAIRS),
        compiler_params=_params(1),
        name="dil_attn",
    )(*operands)


def _layer_norm(v, g, b):
    mu = jnp.mean(v, axis=-1, keepdims=True)
    d = v - mu
    var = jnp.mean(d * d, axis=-1, keepdims=True)
    return d * lax.rsqrt(var + LN_EPS) * g + b


def _dense_kernel(x_ref, oa_ref, ob_ref, oc_ref, od_ref, wo_ref, g1_ref, b1_ref,
                  wg_ref, wu_ref, wd_ref, g2_ref, b2_ref, out_ref, h_ref, *, alpha):
    tm = x_ref.shape[0]
    d_ff = wg_ref.shape[1]
    halves = [slice(r0, r0 + tm // DENSE_SPLIT) for r0 in range(0, tm, tm // DENSE_SPLIT)]
    mixes = [(_dot(oa_ref[rows, :], wo_ref[0:GROUP, :]) + _dot(ob_ref[rows, :], wo_ref[GROUP:2 * GROUP, :])
              + _dot(oc_ref[rows, :], wo_ref[2 * GROUP:3 * GROUP, :])
              + _dot(od_ref[rows, :], wo_ref[3 * GROUP:4 * GROUP, :])) for rows in halves]
    x1s = [_layer_norm(alpha * x_ref[rows, :] + mix, g1_ref[...], b1_ref[...]) for rows, mix in zip(halves, mixes)]
    xbs = [x1.astype(BF16) for x1 in x1s]
    for c0 in range(0, d_ff, FFN_CHUNK):
        for rows, xb in zip(halves, xbs):
            g = _dot(xb, wg_ref[:, c0:c0 + FFN_CHUNK])
            u = _dot(xb, wu_ref[:, c0:c0 + FFN_CHUNK])
            h_ref[rows, c0:c0 + FFN_CHUNK] = (g * (1.0 / (1.0 + jnp.exp(-g))) * u).astype(BF16)
    ys = [_dot(h_ref[rows, :], wd_ref[...]) for rows in halves]
    for rows, x1, y in zip(halves, x1s, ys):
        out_ref[rows, :] = _layer_norm(alpha * x1 + y, g2_ref[...], b2_ref[...])


def _dense(x2d, outs, layer, w_out, g1, b1, w_gate, w_up, w_down, g2, b2, alpha):
    m, d = x2d.shape
    tm = DENSE_TM
    d_ff = w_gate.shape[2]
    row = lambda i: (i, 0)
    o_spec = pl.BlockSpec((tm, GROUP), row)
    params = (w_out, g1, b1, w_gate, w_up, w_down, g2, b2)
    return pl.pallas_call(
        functools.partial(_dense_kernel, alpha=alpha),
        grid=(m // tm,),
        in_specs=[pl.BlockSpec((tm, d), row), o_spec, o_spec, o_spec, o_spec]
        + [_layer_resident(a.shape, layer) for a in params],
        out_specs=pl.BlockSpec((tm, d), row),
        out_shape=jax.ShapeDtypeStruct((m, d), F32),
        scratch_shapes=[pltpu.VMEM((tm, d_ff), BF16)],
        compiler_params=_params(1),
        name="dense",
    )(x2d, *outs, *params)


def kernel(x, w_in, f_bias, conv_w, w_out, rel_bias, ln1_g, ln1_b, w_gate, w_up, w_down, ln2_g, ln2_b):
    b, s, d = x.shape
    depth = w_in.shape[0]
    assert d == 4 * GROUP and w_in.shape[2] == 12 * GROUP + N_HEADS
    assert s % PROJ_TM == 0 and s % (2 * CAUSAL_BQ) == 0 and s % COMB_ROWS == 0
    assert b * s >= 2 * PROJ_TM
    assert [dil for _, dil in DIL_PATTERNS][0] == 1
    for _, dil in DIL_PATTERNS:
        n_blk = (s // dil) // DIL_BLOCK
        assert n_blk * DIL_BLOCK * dil == s and PROJ_TM % (16 * dil) == 0
        assert ((n_blk - 1) % DIL_GEN_GROUP == 0 if dil == 1 else
                dil % (DIL_MID_GROUP if n_blk > 1 else DIL_RES_GROUP) == 0)
    alpha = (2 * depth) ** 0.25
    bias_tables = _dil_bias_tables(rel_bias)
    w_pad = jnp.pad(jnp.swapaxes(w_in, 1, 2), ((0, 0), (0, LANES - N_HEADS), (0, 0))).astype(BF16)
    dense_params = (w_out.astype(BF16), ln1_g[:, None, :], ln1_b[:, None, :], w_gate.astype(BF16),
                    w_up.astype(BF16), w_down.astype(BF16), ln2_g[:, None, :], ln2_b[:, None, :])
    x2d = x.reshape(b * s, d)
    for layer in range(depth):
        main, dl4, dl16, aug, out_d, side = _in_proj(x2d, w_pad, layer, f_bias[layer], conv_w[layer], b, s)
        main3 = main.reshape(b, s, main.shape[1])
        out_a, out_c = _causal_attn(main3, aug.reshape(b, s, aug.shape[1]),
                                    side.reshape((b, s // CAUSAL_BQ) + side.shape[1:]))
        out_b = _dil_attn(main3, dl4, dl16, bias_tables)
        outs = [o.reshape(b * s, GROUP) for o in (out_a, out_b, out_c)] + [out_d]
        x2d = _dense(x2d, outs, layer, *dense_params, alpha)
    return x2d.reshape(b, s, d)
```
